```python
import jax
import jax.numpy as jnp
from jax import lax
import numpy as np

D_MODEL = 2048
BATCH = 8
SEQ = 2048
DEPTH = 1

NORM_EPS = 1e-5
ROPE_THETA = 10000.0
NEG_INF = -1e30

NSA_HEADS = 8
NSA_KV_GROUPS = 2
NSA_HPG = NSA_HEADS // NSA_KV_GROUPS
NSA_HEAD_DIM = 128
CMP_LEN = 32
CMP_STRIDE = 16
CMP_HIDDEN = 256
SEL_LEN = 64
SEL_TOPK = 16
SEL_QBLOCK = 32
SEL_FORCE = 1e3
WINDOW = 512
WIN_QBLOCK = 128

GLA_HEADS = 4
GLA_DK = 128
GLA_DV = 256
GLA_GATE_RANK = 16
GLA_TAU = 16.0
GLA_CHUNK = 64

N_EXPERTS = 32
TOP_K = 4
MOE_FF = D_MODEL
SWIGLU_LIMIT = 7.0
SWIGLU_ALPHA = 1.702
MOE_BLOCK = 256

NSA_Q = NSA_HEADS * NSA_HEAD_DIM
NSA_KV = NSA_KV_GROUPS * NSA_HEAD_DIM
GLA_QK = GLA_HEADS * GLA_DK
GLA_V = GLA_HEADS * GLA_DV
IN_SPLITS = (NSA_Q, 6 * NSA_KV, 3 * NSA_HEADS, GLA_QK, GLA_QK, GLA_V, GLA_V, GLA_GATE_RANK)
IN_WIDTH = NSA_Q + 6 * NSA_KV + 3 * NSA_HEADS + 2 * GLA_QK + 2 * GLA_V + GLA_GATE_RANK

kernel_name = 'hybrid_nsa_gla_moe_block'


def rms_norm(x, gain):
    xf = x.astype(jnp.float32)
    y = xf * lax.rsqrt(jnp.mean(xf * xf, axis=-1, keepdims=True) + NORM_EPS)
    return (y * gain.astype(jnp.float32)).astype(x.dtype)


def split_cols(t, sizes):
    out, off = [], 0
    for n in sizes:
        out.append(t[..., off:off + n])
        off += n
    return out


def rope(x, pos):
    half = x.shape[-1] // 2
    inv_freq = ROPE_THETA ** (-jnp.arange(half, dtype=jnp.float32) / half)
    ang = pos.astype(jnp.float32)[:, None] * inv_freq[None, :]
    shape = (1, pos.shape[0]) + (1,) * (x.ndim - 3) + (half,)
    cos = jnp.cos(ang).reshape(shape)
    sin = jnp.sin(ang).reshape(shape)
    xf = x.astype(jnp.float32)
    x1, x2 = xf[..., :half], xf[..., half:]
    return jnp.concatenate([x1 * cos - x2 * sin, x2 * cos + x1 * sin], axis=-1).astype(x.dtype)


def masked_softmax(s, mask, axis=-1):
    s = jnp.where(mask, s.astype(jnp.float32), NEG_INF)
    return jnp.where(mask, jax.nn.softmax(s, axis=axis), 0.0)


def compress_kv(kv, pos_emb, w1, w2):
    b, s, g, dh = kv.shape
    n_cmp = (s - CMP_LEN) // CMP_STRIDE + 1
    idx = jnp.arange(n_cmp)[:, None] * CMP_STRIDE + jnp.arange(CMP_LEN)[None, :]
    blocks = kv[:, idx] + pos_emb[None, None, :, None, :]
    flat = blocks.transpose(0, 1, 3, 2, 4).reshape(b, n_cmp, g, CMP_LEN * dh)
    return jax.nn.gelu(flat @ w1) @ w2


def nsa_attention(q, kv, gate_logits, cmp_pos_k, cmp_pos_v, w_ck1, w_ck2, w_cv1, w_cv2):
    b, s = q.shape[:2]
    dt = q.dtype
    G, HPG, DH = NSA_KV_GROUPS, NSA_HPG, NSA_HEAD_DIM
    scale = DH ** -0.5
    pos = jnp.arange(s)
    qg = q.reshape(b, s, G, HPG, DH)
    k_c, v_c, k_s, v_s, k_w, v_w = [t.reshape(b, s, G, DH) for t in jnp.split(kv, 6, axis=-1)]

    k_cmp = compress_kv(k_c, cmp_pos_k, w_ck1, w_ck2)
    v_cmp = compress_kv(v_c, cmp_pos_v, w_cv1, w_cv2)
    n_cmp = k_cmp.shape[1]
    cmp_start = jnp.arange(n_cmp) * CMP_STRIDE
    cmp_mask = (cmp_start + CMP_LEN - 1)[None, :] <= pos[:, None]
    s_cmp = jnp.einsum('bsghd,bngd->bghsn', qg, k_cmp).astype(jnp.float32) * scale
    p_cmp = masked_softmax(s_cmp, cmp_mask)
    o_cmp = jnp.einsum('bghsn,bngd->bsghd', p_cmp.astype(dt), v_cmp)

    n_blk = s // SEL_LEN
    n_sel = min(SEL_TOPK, n_blk)
    blk = jnp.arange(n_blk)
    cover = ((cmp_start[:, None] < (blk[None, :] + 1) * SEL_LEN)
             & (cmp_start[:, None] + CMP_LEN > blk[None, :] * SEL_LEN)).astype(jnp.float32)
    importance = jnp.einsum('bghsn,nj->bgsj', p_cmp, cover)
    t_blk = (pos // SEL_LEN)[:, None]
    forced = (blk[None, :] == 0) | (blk[None, :] == t_blk) | (blk[None, :] == t_blk - 1)
    bonus = jnp.where(blk[None, :] > t_blk, -SEL_FORCE, jnp.where(forced, SEL_FORCE, 0.0))
    _, sel_idx = lax.top_k(importance + bonus, n_sel)

    q_r = rope(qg, pos)
    k_sr = rope(k_s, pos)
    k_wr = rope(k_w, pos)
    k_blocks = k_sr.reshape(b, n_blk, SEL_LEN, G, DH).transpose(0, 3, 1, 2, 4)
    v_blocks = v_s.reshape(b, n_blk, SEL_LEN, G, DH).transpose(0, 3, 1, 2, 4)
    nqb = s // SEL_QBLOCK
    q_blk = q_r.reshape(b, nqb, SEL_QBLOCK, G, HPG, DH).transpose(1, 0, 3, 2, 4, 5)
    i_blk = sel_idx.reshape(b, G, nqb, SEL_QBLOCK, n_sel).transpose(2, 0, 1, 3, 4)
    t_q = pos.reshape(nqb, SEL_QBLOCK)
    b_ix = jnp.arange(b)[:, None, None, None]
    g_ix = jnp.arange(G)[None, :, None, None]

    def sel_block(args):
        qb, ib, tq = args
        kg = k_blocks[b_ix, g_ix, ib]
        vg = v_blocks[b_ix, g_ix, ib]
        sc = jnp.einsum('bgqhd,bgqnld->bgqhnl', qb, kg).astype(jnp.float32) * scale
        kpos = ib[..., None] * SEL_LEN + jnp.arange(SEL_LEN)
        mask = (kpos <= tq[None, None, :, None, None])[:, :, :, None]
        p = masked_softmax(sc, mask, axis=(-2, -1))
        return jnp.einsum('bgqhnl,bgqnld->bgqhd', p.astype(dt), vg)

    o_sel = lax.map(sel_block, (q_blk, i_blk, t_q))
    o_sel = o_sel.transpose(1, 0, 3, 2, 4, 5).reshape(b, s, G, HPG, DH)

    nwb = s // WIN_QBLOCK
    span = WIN_QBLOCK + WINDOW
    k_pad = jnp.pad(k_wr, ((0, 0), (WINDOW, 0), (0, 0), (0, 0)))
    v_pad = jnp.pad(v_w, ((0, 0), (WINDOW, 0), (0, 0), (0, 0)))
    kidx = jnp.arange(nwb)[:, None] * WIN_QBLOCK + jnp.arange(span)[None, :]
    k_win = k_pad[:, kidx]
    v_win = v_pad[:, kidx]
    q_win = q_r.reshape(b, nwb, WIN_QBLOCK, G, HPG, DH)
    sc = jnp.einsum('bnqghd,bnkgd->bnghqk', q_win, k_win).astype(jnp.float32) * scale
    k_real = (kidx - WINDOW)[:, None, :]
    tq = pos.reshape(nwb, WIN_QBLOCK)[:, :, None]
    wmask = (k_real <= tq) & (k_real > tq - WINDOW) & (k_real >= 0)
    p = masked_softmax(sc, wmask[None, :, None, None])
    o_win = jnp.einsum('bnghqk,bnkgd->bnqghd', p.astype(dt), v_win).reshape(b, s, G, HPG, DH)

    g = jax.nn.sigmoid(gate_logits).reshape(b, s, G, HPG, 3)
    o = g[..., 0:1] * o_cmp + g[..., 1:2] * o_sel + g[..., 2:3] * o_win
    return o.reshape(b, s, NSA_Q)


def gla_attention(q, k, v, r, a_lr, w_alpha, b_alpha, norm_gain):
    b, s = q.shape[:2]
    dt = q.dtype
    H, C = GLA_HEADS, GLA_CHUNK
    nc = s // C
    log_a = jax.nn.log_sigmoid((a_lr @ w_alpha + b_alpha).astype(jnp.float32)) / GLA_TAU

    def chunks(t, d):
        return t.astype(jnp.float32).reshape(b, nc, C, H, d)

    qc = chunks(q, GLA_DK) * (GLA_DK ** -0.5)
    kc = chunks(k, GLA_DK)
    vc = chunks(v, GLA_DV)
    cum = jnp.cumsum(chunks(log_a, GLA_DK), axis=2)
    cum_last = cum[:, :, -1]
    q_t = qc * jnp.exp(cum)
    k_t = kc * jnp.exp(-cum)
    causal = jnp.tril(jnp.ones((C, C), dtype=bool))
    attn = jnp.where(causal, jnp.einsum('bnihd,bnjhd->bnhij', q_t, k_t), 0.0)
    o_intra = jnp.einsum('bnhij,bnjhv->bnihv', attn, vc)
    k_state = kc * jnp.exp(cum_last[:, :, None] - cum)
    d_state = jnp.einsum('bnchd,bnchv->nbhdv', k_state, vc)
    decay = jnp.exp(cum_last).transpose(1, 0, 2, 3)

    def step(state, inp):
        dec, ds = inp
        return dec[..., None] * state + ds, state

    state0 = jnp.zeros((b, H, GLA_DK, GLA_DV), jnp.float32)
    _, prev_states = lax.scan(step, state0, (decay, d_state))
    o_inter = jnp.einsum('bnihd,nbhdv->bnihv', q_t, prev_states)
    o = (o_intra + o_inter).reshape(b, s, H, GLA_DV)
    o = o * lax.rsqrt(jnp.mean(o * o, axis=-1, keepdims=True) + NORM_EPS)
    o = o.reshape(b, s, GLA_V) * norm_gain.astype(jnp.float32)
    return (o * jax.nn.silu(r.astype(jnp.float32))).astype(dt)


def moe_ffn(h, w_router, b_router, w_gate_up, b_gate_up, w_down, b_down):
    b, s, d = h.shape
    n = b * s
    nk = n * TOP_K
    hf = h.reshape(n, d)
    logits = (hf @ w_router + b_router).astype(jnp.float32)
    top_v, top_e = lax.top_k(logits, TOP_K)
    top_w = jax.nn.softmax(top_v, axis=-1)
    flat_e = top_e.reshape(nk)
    flat_tok = jnp.arange(nk) // TOP_K
    flat_w = top_w.reshape(nk)
    order = jnp.argsort(flat_e)
    se = flat_e[order]
    counts = jnp.bincount(flat_e, length=N_EXPERTS)
    padded = (counts + MOE_BLOCK - 1) // MOE_BLOCK * MOE_BLOCK
    pad_end = jnp.cumsum(padded)
    pad_start = pad_end - padded
    start = jnp.cumsum(counts) - counts
    dest = pad_start[se] + jnp.arange(nk) - start[se]
    n_blocks = -(-nk // MOE_BLOCK) + N_EXPERTS
    n_slots = n_blocks * MOE_BLOCK
    slot_tok = jnp.zeros((n_slots,), jnp.int32).at[dest].set(flat_tok[order])
    slot_w = jnp.zeros((n_slots,), jnp.float32).at[dest].set(flat_w[order])
    blk_e = jnp.minimum(jnp.searchsorted(pad_end, jnp.arange(n_blocks) * MOE_BLOCK, side='right'), N_EXPERTS - 1)

    def expert_block(y, args):
        e, tok, wt = args
        xb = hf[tok]
        gu = xb @ w_gate_up[e] + b_gate_up[e]
        gate = jnp.minimum(gu[:, :MOE_FF], SWIGLU_LIMIT)
        up = jnp.clip(gu[:, MOE_FF:], -SWIGLU_LIMIT, SWIGLU_LIMIT)
        act = (up + 1.0) * gate * jax.nn.sigmoid(gate * SWIGLU_ALPHA)
        out = act @ w_down[e] + b_down[e]
        return y.at[tok].add(out.astype(jnp.float32) * wt[:, None]), None

    y, _ = lax.scan(expert_block, jnp.zeros((n, d), jnp.float32),
                    (blk_e, slot_tok.reshape(n_blocks, MOE_BLOCK), slot_w.reshape(n_blocks, MOE_BLOCK)))
    return y.astype(h.dtype).reshape(b, s, d)


def setup_inputs(seed: int = 0) -> dict:
    key = jax.random.key(seed)
    ks = jax.random.split(key, 26)
    L, D, E, F = DEPTH, D_MODEL, N_EXPERTS, MOE_FF

    def nrm(k, shape, scale):
        return jax.random.normal(k, shape, jnp.float32) * scale

    return {
        'x': nrm(ks[0], (BATCH, SEQ, D), 1.0),
        'norm_mix': 1.0 + nrm(ks[1], (L, D), 0.02),
        'w_in': nrm(ks[2], (L, D, IN_WIDTH), D ** -0.5),
        'cmp_pos_k': nrm(ks[3], (L, CMP_LEN, NSA_HEAD_DIM), 0.1),
        'cmp_pos_v': nrm(ks[4], (L, CMP_LEN, NSA_HEAD_DIM), 0.1),
        'w_cmp_k1': nrm(ks[5], (L, CMP_LEN * NSA_HEAD_DIM, CMP_HIDDEN), (CMP_LEN * NSA_HEAD_DIM) ** -0.5),
        'w_cmp_k2': nrm(ks[6], (L, CMP_HIDDEN, NSA_HEAD_DIM), CMP_HIDDEN ** -0.5),
        'w_cmp_v1': nrm(ks[7], (L, CMP_LEN * NSA_HEAD_DIM, CMP_HIDDEN), (CMP_LEN * NSA_HEAD_DIM) ** -0.5),
        'w_cmp_v2': nrm(ks[8], (L, CMP_HIDDEN, NSA_HEAD_DIM), CMP_HIDDEN ** -0.5),
        'w_gla_alpha': nrm(ks[9], (L, GLA_GATE_RANK, GLA_QK), GLA_GATE_RANK ** -0.5),
        'b_gla_alpha': nrm(ks[10], (L, GLA_QK), 0.1),
        'gla_norm': 1.0 + nrm(ks[11], (L, GLA_V), 0.02),
        'w_proj_nsa': nrm(ks[12], (L, NSA_Q, D), NSA_Q ** -0.5),
        'w_proj_gla': nrm(ks[13], (L, GLA_V, D), GLA_V ** -0.5),
        'w_merge_gate': nrm(ks[14], (L, D, 2 * D), D ** -0.5),
        'b_merge_gate': nrm(ks[15], (L, 2 * D), 0.1),
        'w_out': nrm(ks[16], (L, D, D), D ** -0.5),
        'norm_moe': 1.0 + nrm(ks[17], (L, D), 0.02),
        'w_router': nrm(ks[18], (L, D, E), D ** -0.5),
        'b_router': nrm(ks[19], (L, E), 0.01),
        'w_gate_up': nrm(ks[20], (L, E, D, 2 * F), D ** -0.5),
        'b_gate_up': nrm(ks[21], (L, E, 2 * F), 0.01),
        'w_down': nrm(ks[22], (L, E, F, D), F ** -0.5),
        'b_down': nrm(ks[23], (L, E, D), 0.01),
        'norm_final': 1.0 + nrm(ks[24], (D,), 0.02),
    }


def reference(x, norm_mix, w_in, cmp_pos_k, cmp_pos_v, w_cmp_k1, w_cmp_k2, w_cmp_v1, w_cmp_v2,
              w_gla_alpha, b_gla_alpha, gla_norm, w_proj_nsa, w_proj_gla, w_merge_gate, b_merge_gate,
              w_out, norm_moe, w_router, b_router, w_gate_up, b_gate_up, w_down, b_down, norm_final):
    for l in range(DEPTH):
        h = rms_norm(x, norm_mix[l])
        q_n, kv_n, g_n, q_g, k_g, v_g, r_g, a_g = split_cols(h @ w_in[l], IN_SPLITS)
        o_nsa = nsa_attention(q_n, kv_n, g_n, cmp_pos_k[l], cmp_pos_v[l],
                              w_cmp_k1[l], w_cmp_k2[l], w_cmp_v1[l], w_cmp_v2[l])
        o_gla = gla_attention(q_g, k_g, v_g, r_g, a_g, w_gla_alpha[l], b_gla_alpha[l], gla_norm[l])
        merge = jax.nn.sigmoid(h @ w_merge_gate[l] + b_merge_gate[l])
        mixed = (merge[..., :D_MODEL] * (o_nsa @ w_proj_nsa[l])
                 + merge[..., D_MODEL:] * (o_gla @ w_proj_gla[l]))
        x = x + mixed @ w_out[l]
        x = x + moe_ffn(rms_norm(x, norm_moe[l]), w_router[l], b_router[l],
                        w_gate_up[l], b_gate_up[l], w_down[l], b_down[l])
    return rms_norm(x, norm_final)
```

```python
import functools

import jax
import jax.numpy as jnp
from jax import lax
from jax.experimental import pallas as pl
from jax.experimental.pallas import tpu as pltpu

F32 = jnp.float32
BF16 = jnp.bfloat16

NORM_EPS = 1e-5
ROPE_THETA = 10000.0
NEG_INF = -1e30

NSA_HEADS = 8
NSA_GROUPS = 2
NSA_HPG = NSA_HEADS // NSA_GROUPS
DH = 128
CMP_LEN = 32
CMP_STRIDE = 16
SEL_LEN = 64
SEL_TOPK = 16
SEL_FORCE = 1e3
WINDOW = 512

GLA_HEADS = 4
GLA_DK = 128
GLA_DV = 256
GLA_RANK = 16
GLA_TAU = 16.0
GLA_CHUNK = 64

TOP_K = 4
SWIGLU_LIMIT = 7.0
SWIGLU_ALPHA = 1.702

LANES = 128
VMEM_LIMIT = 56 * 1024 * 1024

NSA_Q = NSA_HEADS * DH
NSA_KV = NSA_GROUPS * DH
GLA_QK = GLA_HEADS * GLA_DK
GLA_V = GLA_HEADS * GLA_DV
MAIN_W = NSA_Q + 6 * NSA_KV + 2 * GLA_QK + 2 * GLA_V


def _cparams(sem, vmem=VMEM_LIMIT):
    return pltpu.CompilerParams(dimension_semantics=sem, vmem_limit_bytes=vmem)


def _rope(x, cos, sin_signed):
    return x * cos + pltpu.roll(x, DH // 2, axis=1) * sin_signed


def _proj_kernel(x_ref, gain_ref, w_ref, wsmall_ref, bias_ref, cos_ref, sin_ref,
                 big_ref, small_ref, h_scr, *, n_merge_tiles, rope_tiles):
    j = pl.program_id(1)

    @pl.when(j == 0)
    def _():
        x = x_ref[...]
        ms = jnp.mean(x * x, axis=-1, keepdims=True)
        hb = (x * lax.rsqrt(ms + NORM_EPS) * gain_ref[...]).astype(BF16)
        h_scr[...] = hb
        small_ref[...] = jnp.dot(hb, wsmall_ref[...], preferred_element_type=F32)

    acc = jnp.dot(h_scr[...], w_ref[...], preferred_element_type=F32)
    is_merge = j < n_merge_tiles
    is_rope = (j == rope_tiles[0]) | (j == rope_tiles[1])

    @pl.when(is_merge)
    def _():
        big_ref[...] = jax.nn.sigmoid(acc + bias_ref[...]).astype(BF16)

    @pl.when(is_rope)
    def _():
        cos = cos_ref[...]
        sin = sin_ref[...]
        for g in range(NSA_GROUPS):
            sl = slice(g * DH, (g + 1) * DH)
            big_ref[:, sl] = _rope(acc[:, sl], cos, sin).astype(BF16)
        big_ref[:, NSA_KV:] = acc[:, NSA_KV:].astype(BF16)

    @pl.when(jnp.logical_not(is_merge | is_rope))
    def _():
        big_ref[...] = acc.astype(BF16)


def _proj(x2d, gain, w_all, w_small, bias_all, cos2, sin2, seq, d_merge):
    n, d = x2d.shape
    width = w_all.shape[1]
    tm = min(1024, seq)
    tn = 512
    n_merge_tiles = d_merge // tn
    rope_tiles = (n_merge_tiles + 3, n_merge_tiles + 4)
    nsb = seq // tm
    kern = functools.partial(_proj_kernel, n_merge_tiles=n_merge_tiles, rope_tiles=rope_tiles)
    return pl.pallas_call(
        kern,
        grid=(n // tm, width // tn),
        in_specs=[
            pl.BlockSpec((tm, d), lambda i, j: (i, 0)),
            pl.BlockSpec((1, d), lambda i, j: (0, 0)),
            pl.BlockSpec((d, tn), lambda i, j: (0, j)),
            pl.BlockSpec((d, LANES), lambda i, j: (0, 0)),
            pl.BlockSpec((1, tn), lambda i, j: (0, j)),
            pl.BlockSpec((tm, DH), lambda i, j: (i % nsb, 0)),
            pl.BlockSpec((tm, DH), lambda i, j: (i % nsb, 0)),
        ],
        out_specs=[
            pl.BlockSpec((tm, tn), lambda i, j: (i, j)),
            pl.BlockSpec((tm, LANES), lambda i, j: (i, 0)),
        ],
        out_shape=[
            jax.ShapeDtypeStruct((n, width), BF16),
            jax.ShapeDtypeStruct((n, LANES), F32),
        ],
        scratch_shapes=[pltpu.VMEM((tm, d), BF16)],
        compiler_params=_cparams(("parallel", "arbitrary")),
        name="proj",
    )(x2d, gain, w_all, w_small, bias_all, cos2, sin2)


def _compress_kernel(kv_ref, pos_ref, w1_ref, w2_ref, out_ref, scr, *, seq, ncp):
    nreal = seq // CMP_STRIDE
    scr[0:seq, :] = kv_ref[...].astype(F32)
    scr[seq:seq + CMP_LEN, :] = jnp.zeros((CMP_LEN, DH), F32)
    acc = jnp.zeros((nreal, w1_ref.shape[1]), F32)
    for l in range(CMP_LEN):
        a = scr[pl.ds(l, nreal, stride=CMP_STRIDE), :] + pos_ref[l:l + 1, :]
        acc = acc + jnp.dot(a.astype(BF16), w1_ref[l * DH:(l + 1) * DH, :],
                            preferred_element_type=F32)
    hid = jax.nn.gelu(acc)
    out = jnp.dot(hid.astype(BF16), w2_ref[...], preferred_element_type=F32)
    row = lax.broadcasted_iota(jnp.int32, out.shape, 0)
    out = jnp.where(row < nreal - 1, out, 0.0).astype(BF16)
    if ncp > nreal:
        out = jnp.concatenate([out, jnp.zeros((ncp - nreal, DH), BF16)], axis=0)
    out_ref[...] = out


def _compress(big, pos, w1, w2, batch, seq, col0):
    ncp = max(seq // CMP_STRIDE, LANES)
    kern = functools.partial(_compress_kernel, seq=seq, ncp=ncp)
    hid = w1.shape[2]
    return pl.pallas_call(
        kern,
        grid=(batch, NSA_GROUPS, 2),
        in_specs=[
            pl.BlockSpec((seq, DH), lambda b, g, t: (b, col0 + 2 * t + g)),
            pl.BlockSpec((None, CMP_LEN, DH), lambda b, g, t: (t, 0, 0)),
            pl.BlockSpec((None, CMP_LEN * DH, hid), lambda b, g, t: (t, 0, 0)),
            pl.BlockSpec((None, hid, DH), lambda b, g, t: (t, 0, 0)),
        ],
        out_specs=pl.BlockSpec((None, None, None, ncp, DH), lambda b, g, t: (b, g, t, 0, 0)),
        out_shape=jax.ShapeDtypeStruct((batch, NSA_GROUPS, 2, ncp, DH), BF16),
        scratch_shapes=[pltpu.VMEM((seq + CMP_LEN, DH), F32)],
        compiler_params=_cparams(("parallel", "parallel", "arbitrary")),
        name="compress",
    )(big, pos, w1, w2)


def _stack_heads(t):
    return jnp.concatenate([t[:, h * DH:(h + 1) * DH] for h in range(NSA_HPG)], axis=0)


def _nsa_kernel(q_ref, kc_ref, vc_ref, ks_ref, vs_ref, kw_ref, vw_ref, cos_ref, sin_ref,
                small_ref, cover_ref, expand_ref, o_ref, bias_scr, *, seq, tq, ck):
    i = pl.program_id(2)
    g = pl.program_id(1)
    q0 = i * tq
    scale = DH ** -0.5
    rows = NSA_HPG * tq
    ncp = kc_ref.shape[0]
    n_blk = seq // SEL_LEN
    wspan = tq + WINDOW

    q = q_ref[...]
    qs = _stack_heads(q)
    pos_q = q0 + lax.broadcasted_iota(jnp.int32, (tq, 1), 0)
    pos_rows = jnp.concatenate([pos_q] * NSA_HPG, axis=0)

    s = lax.dot_general(qs, kc_ref[...], (((1,), (1,)), ((), ())),
                        preferred_element_type=F32) * scale
    n_idx = lax.broadcasted_iota(jnp.int32, (1, ncp), 1)
    cmask = (n_idx * CMP_STRIDE + (CMP_LEN - 1)) <= pos_rows
    s = jnp.where(cmask, s, NEG_INF)
    m = jnp.max(s, axis=-1, keepdims=True)
    e = jnp.exp(s - m)
    p = jnp.where(cmask, e / jnp.sum(e, axis=-1, keepdims=True), 0.0)
    o_cmp = jnp.dot(p.astype(BF16), vc_ref[...], preferred_element_type=F32)

    psum = p[0:tq]
    for h in range(1, NSA_HPG):
        psum = psum + p[h * tq:(h + 1) * tq]
    imp = jnp.dot(psum, cover_ref[...], preferred_element_type=F32,
                  precision=lax.Precision.HIGHEST)
    blk = lax.broadcasted_iota(jnp.int32, (tq, LANES), 1)
    t_blk = pos_q // SEL_LEN
    forced = (blk == 0) | (blk == t_blk) | (blk == t_blk - 1)
    bonus = jnp.where(blk > t_blk, -SEL_FORCE, jnp.where(forced, SEL_FORCE, 0.0))
    val = jnp.where(blk < n_blk, imp + bonus, -3e38)
    rank = jnp.zeros((tq, LANES), F32)
    for c in range(n_blk):
        vc = val[:, c:c + 1]
        beats = (vc > val) | ((vc == val) & (blk > c))
        rank = rank + jnp.where(beats, 1.0, 0.0)
    n_sel = min(SEL_TOPK, n_blk)
    sel = jnp.where((rank < n_sel) & (blk < n_blk), 1.0, 0.0).astype(BF16)
    selfull = jnp.dot(sel, expand_ref[...], preferred_element_type=F32)
    kpos = lax.broadcasted_iota(jnp.int32, (1, seq), 1)
    bias = jnp.where((selfull > 0.5) & (kpos <= pos_q), 0.0, NEG_INF)
    for c in range(seq // ck):
        bias_scr[c] = bias[:, c * ck:(c + 1) * ck]

    cos = cos_ref[...]
    sin = sin_ref[...]
    qr = jnp.concatenate(
        [(_rope(q[:, h * DH:(h + 1) * DH].astype(F32), cos, sin) * scale).astype(BF16)
         for h in range(NSA_HPG)], axis=0)

    n_chunks = (q0 + tq + ck - 1) // ck

    def sel_body(c, carry):
        m_i, l_i, acc = carry
        k0 = pl.multiple_of(c * ck, ck)
        kblk = ks_ref[pl.ds(k0, ck), :]
        vblk = vs_ref[pl.ds(k0, ck), :]
        sc = lax.dot_general(qr, kblk, (((1,), (1,)), ((), ())), preferred_element_type=F32)
        b = bias_scr[c]
        sc = sc + jnp.concatenate([b] * NSA_HPG, axis=0)
        m_new = jnp.maximum(m_i, jnp.max(sc, axis=-1, keepdims=True))
        alpha = jnp.exp(m_i - m_new)
        pc = jnp.exp(sc - m_new)
        l_new = alpha * l_i + jnp.sum(pc, axis=-1, keepdims=True)
        acc = alpha * acc + jnp.dot(pc.astype(BF16), vblk, preferred_element_type=F32)
        return m_new, l_new, acc

    m0 = jnp.full((rows, 1), NEG_INF, F32)
    l0 = jnp.zeros((rows, 1), F32)
    a0 = jnp.zeros((rows, DH), F32)
    _, l_s, acc_s = lax.fori_loop(0, n_chunks, sel_body, (m0, l0, a0))
    o_sel = acc_s / l_s

    kstart = pl.multiple_of(jnp.maximum(q0 - WINDOW, 0), tq)
    kw = kw_ref[pl.ds(kstart, wspan), :]
    vw = vw_ref[pl.ds(kstart, wspan), :]
    sw = lax.dot_general(qr, kw, (((1,), (1,)), ((), ())), preferred_element_type=F32)
    kp = kstart + lax.broadcasted_iota(jnp.int32, (1, wspan), 1)
    wmask = (kp <= pos_rows) & (kp > pos_rows - WINDOW)
    sw = jnp.where(wmask, sw, NEG_INF)
    mw = jnp.max(sw, axis=-1, keepdims=True)
    ew = jnp.exp(sw - mw)
    pw = ew / jnp.sum(ew, axis=-1, keepdims=True)
    o_win = jnp.dot(pw.astype(BF16), vw, preferred_element_type=F32)

    gates = jax.nn.sigmoid(small_ref[...])
    outs = []
    for h in range(NSA_HPG):
        base = (g * NSA_HPG + h) * 3
        r = slice(h * tq, (h + 1) * tq)
        o_h = (_lane_col(gates, base, tq) * o_cmp[r]
               + _lane_col(gates, base + 1, tq) * o_sel[r]
               + _lane_col(gates, base + 2, tq) * o_win[r])
        outs.append(o_h)
    o_ref[...] = jnp.concatenate(outs, axis=1).astype(BF16)


def _lane_col(t, col, rows):
    lane = lax.broadcasted_iota(jnp.int32, (rows, LANES), 1)
    return jnp.sum(jnp.where(lane == col, t, 0.0), axis=-1, keepdims=True)


def _nsa(big, cmp_kv, small, cos2, sin2, cover, expand, batch, seq, main0):
    tq = 128
    ck = min(512, seq)
    nq = seq // tq
    ncp = cmp_kv.shape[3]
    c128 = main0 // DH
    kern = functools.partial(_nsa_kernel, seq=seq, tq=tq, ck=ck)
    kv_col = c128 + NSA_Q // DH

    def kvspec(which):
        return pl.BlockSpec((seq, DH), lambda b, g, i: (b, kv_col + 2 * which + g))

    return pl.pallas_call(
        kern,
        grid=(batch, NSA_GROUPS, nq),
        in_specs=[
            pl.BlockSpec((tq, NSA_HPG * DH), lambda b, g, i: (b * nq + i, main0 // (NSA_HPG * DH) + g)),
            pl.BlockSpec((None, None, None, ncp, DH), lambda b, g, i: (b, g, 0, 0, 0)),
            pl.BlockSpec((None, None, None, ncp, DH), lambda b, g, i: (b, g, 1, 0, 0)),
            kvspec(2), kvspec(3), kvspec(4), kvspec(5),
            pl.BlockSpec((tq, DH), lambda b, g, i: (i, 0)),
            pl.BlockSpec((tq, DH), lambda b, g, i: (i, 0)),
            pl.BlockSpec((tq, LANES), lambda b, g, i: (b * nq + i, 0)),
            pl.BlockSpec((ncp, LANES), lambda b, g, i: (0, 0)),
            pl.BlockSpec((LANES, seq), lambda b, g, i: (0, 0)),
        ],
        out_specs=pl.BlockSpec((tq, NSA_HPG * DH), lambda b, g, i: (b * nq + i, g)),
        out_shape=jax.ShapeDtypeStruct((batch * seq, NSA_Q), BF16),
        scratch_shapes=[pltpu.VMEM((seq // ck, tq, ck), F32)],
        compiler_params=_cparams(("parallel", "parallel", "arbitrary")),
        name="nsa",
    )(big, cmp_kv, cmp_kv, big, big, big, big, cos2, sin2, small, cover, expand)


def _gla_kernel(q_ref, k_ref, v_ref, r_ref, small_ref, wa_ref, ba_ref, gain_ref, tri_ref,
                o_ref, la_scr, st_scr, *, seq):
    C = GLA_CHUNK
    nc = seq // C
    z = jnp.dot(small_ref[...], wa_ref[...], preferred_element_type=F32,
                precision=lax.Precision.HIGHEST) + ba_ref[...]
    la_scr[...] = jax.nn.log_sigmoid(z) / GLA_TAU
    st_scr[...] = jnp.zeros_like(st_scr)
    ri = lax.broadcasted_iota(jnp.int32, (C, C), 0)
    ci = lax.broadcasted_iota(jnp.int32, (C, C), 1)
    causal = ri >= ci
    qscale = GLA_DK ** -0.5

    def body(c, carry):
        r0 = pl.multiple_of(c * C, C)
        la = la_scr[pl.ds(r0, C), :]
        cum = jnp.dot(tri_ref[...], la, preferred_element_type=F32,
                      precision=lax.Precision.HIGHEST)
        cum_last = cum[C - 1:C, :]
        qc = q_ref[pl.ds(r0, C), :].astype(F32) * qscale
        kc = k_ref[pl.ds(r0, C), :].astype(F32)
        vc = v_ref[pl.ds(r0, C), :]
        q_t = (qc * jnp.exp(cum)).astype(BF16)
        k_t = (kc * jnp.exp(-cum)).astype(BF16)
        attn = lax.dot_general(q_t, k_t, (((1,), (1,)), ((), ())), preferred_element_type=F32)
        attn = jnp.where(causal, attn, 0.0)
        o = jnp.dot(attn.astype(BF16), vc, preferred_element_type=F32)
        st = st_scr[...]
        o = o + lax.dot_general(q_t, st.astype(BF16), (((1,), (1,)), ((), ())),
                                preferred_element_type=F32)
        k_state = (kc * jnp.exp(cum_last - cum)).astype(BF16)
        d_st = lax.dot_general(vc, k_state, (((0,), (0,)), ((), ())),
                               preferred_element_type=F32)
        st_scr[...] = st * jnp.exp(cum_last) + d_st
        o = o * lax.rsqrt(jnp.mean(o * o, axis=-1, keepdims=True) + NORM_EPS)
        o = o * gain_ref[...]
        rr = r_ref[pl.ds(r0, C), :].astype(F32)
        o_ref[pl.ds(r0, C), :] = (o * (rr * jax.nn.sigmoid(rr))).astype(BF16)
        return carry

    lax.fori_loop(0, nc, body, 0)


def _gla(big, small, w_alpha, b_alpha, gain, tri, batch, seq, main0):
    qcol = (main0 + NSA_Q + 6 * NSA_KV) // GLA_DK
    kcol = qcol + GLA_QK // GLA_DK
    vcol = (main0 + NSA_Q + 6 * NSA_KV + 2 * GLA_QK) // GLA_DV
    rcol = vcol + GLA_V // GLA_DV
    kern = functools.partial(_gla_kernel, seq=seq)
    return pl.pallas_call(
        kern,
        grid=(batch, GLA_HEADS),
        in_specs=[
            pl.BlockSpec((seq, GLA_DK), lambda b, h: (b, qcol + h)),
            pl.BlockSpec((seq, GLA_DK), lambda b, h: (b, kcol + h)),
            pl.BlockSpec((seq, GLA_DV), lambda b, h: (b, vcol + h)),
            pl.BlockSpec((seq, GLA_DV), lambda b, h: (b, rcol + h)),
            pl.BlockSpec((seq, LANES), lambda b, h: (b, 0)),
            pl.BlockSpec((LANES, GLA_DK), lambda b, h: (0, h)),
            pl.BlockSpec((1, GLA_DK), lambda b, h: (0, h)),
            pl.BlockSpec((1, GLA_DV), lambda b, h: (0, h)),
            pl.BlockSpec((GLA_CHUNK, GLA_CHUNK), lambda b, h: (0, 0)),
        ],
        out_specs=pl.BlockSpec((seq, GLA_DV), lambda b, h: (b, h)),
        out_shape=jax.ShapeDtypeStruct((batch * seq, GLA_V), BF16),
        scratch_shapes=[pltpu.VMEM((seq, GLA_DK), F32), pltpu.VMEM((GLA_DV, GLA_DK), F32)],
        compiler_params=_cparams(("parallel", "parallel")),
        name="gla",
    )(big, big, big, big, small, w_alpha, b_alpha, gain, tri)


def _mix_kernel(on_ref, og_ref, ma_ref, mb_ref, x_ref, wpn_ref, wpg_ref, wo_ref, gain_ref,
                wrh_ref, wrl_ref, br_ref, x2_ref, h2_ref, lg_ref):
    a = jnp.dot(on_ref[...], wpn_ref[...], preferred_element_type=F32)
    b = jnp.dot(og_ref[...], wpg_ref[...], preferred_element_type=F32)
    mixed = ma_ref[...].astype(F32) * a + mb_ref[...].astype(F32) * b
    x2 = x_ref[...] + jnp.dot(mixed.astype(BF16), wo_ref[...], preferred_element_type=F32)
    x2_ref[...] = x2
    ms = jnp.mean(x2 * x2, axis=-1, keepdims=True)
    h2 = x2 * lax.rsqrt(ms + NORM_EPS) * gain_ref[...]
    h2_ref[...] = h2
    hi = h2.astype(BF16)
    lo = (h2 - hi.astype(F32)).astype(BF16)
    lg = (jnp.dot(hi, wrh_ref[...], preferred_element_type=F32)
          + jnp.dot(lo, wrh_ref[...], preferred_element_type=F32)
          + jnp.dot(hi, wrl_ref[...], preferred_element_type=F32))
    lg_ref[...] = lg + br_ref[...]


def _mix(o_nsa, o_gla, big, x2d, wpn, wpg, wo, gain, wr_hi, wr_lo, br):
    n, d = x2d.shape
    tm = 256
    const = lambda i: (0, 0)
    return pl.pallas_call(
        _mix_kernel,
        grid=(n // tm,),
        in_specs=[
            pl.BlockSpec((tm, NSA_Q), lambda i: (i, 0)),
            pl.BlockSpec((tm, GLA_V), lambda i: (i, 0)),
            pl.BlockSpec((tm, d), lambda i: (i, 0)),
            pl.BlockSpec((tm, d), lambda i: (i, 1)),
            pl.BlockSpec((tm, d), lambda i: (i, 0)),
            pl.BlockSpec((NSA_Q, d), const, pipeline_mode=pl.Buffered(1)),
            pl.BlockSpec((GLA_V, d), const, pipeline_mode=pl.Buffered(1)),
            pl.BlockSpec((d, d), const, pipeline_mode=pl.Buffered(1)),
            pl.BlockSpec((1, d), const),
            pl.BlockSpec((d, LANES), const),
            pl.BlockSpec((d, LANES), const),
            pl.BlockSpec((1, LANES), const),
        ],
        out_specs=[
            pl.BlockSpec((tm, d), lambda i: (i, 0)),
            pl.BlockSpec((tm, d), lambda i: (i, 0)),
            pl.BlockSpec((tm, LANES), lambda i: (i, 0)),
        ],
        out_shape=[
            jax.ShapeDtypeStruct((n, d), F32),
            jax.ShapeDtypeStruct((n, d), F32),
            jax.ShapeDtypeStruct((n, LANES), F32),
        ],
        compiler_params=_cparams(("parallel",)),
        name="mix",
    )(o_nsa, o_gla, big, big, x2d, wpn, wpg, wo, gain, wr_hi, wr_lo, br)


def _route_kernel(lg_ref, tri_ref, idx_ref, w_ref, cnt_ref, carry_scr, *, n_exp):
    i = pl.program_id(0)
    tr = lg_ref.shape[0]

    @pl.when(i == 0)
    def _():
        carry_scr[...] = jnp.zeros_like(carry_scr)

    lane = lax.broadcasted_iota(jnp.int32, (tr, LANES), 1)
    work = jnp.where(lane < n_exp, lg_ref[...], -3e38)
    onehots, vals, idxs = [], [], []
    for _ in range(TOP_K):
        mval = jnp.max(work, axis=-1, keepdims=True)
        idx = jnp.min(jnp.where(work == mval, lane, LANES), axis=-1, keepdims=True)
        oh = lane == idx
        work = jnp.where(oh, -3e38, work)
        onehots.append(oh)
        vals.append(mval)
        idxs.append(idx)
    exps = [jnp.exp(v - vals[0]) for v in vals]
    den = exps[0] + exps[1] + exps[2] + exps[3]
    onehot = jnp.zeros((tr, LANES), F32)
    for oh in onehots:
        onehot = onehot + jnp.where(oh, 1.0, 0.0)
    before = jnp.dot(tri_ref[...], onehot.astype(BF16), preferred_element_type=F32) + carry_scr[...]
    carry_scr[...] = carry_scr[...] + jnp.sum(onehot, axis=0, keepdims=True)
    idx_out = jnp.zeros((tr, LANES), jnp.int32)
    w_out = jnp.zeros((tr, LANES), F32)
    for k in range(TOP_K):
        rank_k = jnp.sum(jnp.where(onehots[k], before, 0.0), axis=-1, keepdims=True)
        idx_out = jnp.where(lane == k, idxs[k], idx_out)
        idx_out = jnp.where(lane == TOP_K + k, rank_k.astype(jnp.int32), idx_out)
        w_out = jnp.where(lane == k, exps[k] / den, w_out)
    idx_ref[...] = idx_out
    w_ref[...] = w_out
    cnt_ref[...] = carry_scr[...]


def _route(logits, tri, n_exp):
    n = logits.shape[0]
    tr = tri.shape[0]
    kern = functools.partial(_route_kernel, n_exp=n_exp)
    return pl.pallas_call(
        kern,
        grid=(n // tr,),
        in_specs=[pl.BlockSpec((tr, LANES), lambda i: (i, 0)),
                  pl.BlockSpec((tr, tr), lambda i: (0, 0))],
        out_specs=[pl.BlockSpec((tr, LANES), lambda i: (i, 0)),
                   pl.BlockSpec((tr, LANES), lambda i: (i, 0)),
                   pl.BlockSpec((1, LANES), lambda i: (0, 0))],
        out_shape=[jax.ShapeDtypeStruct((n, LANES), jnp.int32),
                   jax.ShapeDtypeStruct((n, LANES), F32),
                   jax.ShapeDtypeStruct((1, LANES), F32)],
        scratch_shapes=[pltpu.VMEM((1, LANES), F32)],
        compiler_params=_cparams(("arbitrary",)),
        name="route",
    )(logits, tri)


def _dispatch_kernel(dest_ref, padlo_ref, padhi_ref, h_hbm, xs_hbm, zbuf, sem, zsem, *, td, tb, n_exp):
    i = pl.program_id(0)
    n_blocks = xs_hbm.shape[0] // tb

    @pl.when(i == 0)
    def _():
        zbuf[...] = jnp.zeros_like(zbuf)

        def row_copy(s):
            return pltpu.make_async_copy(zbuf.at[pl.ds(0, 1)], xs_hbm.at[pl.ds(s, 1)], zsem)

        def per_expert(e, c):
            lo = padlo_ref[e]
            hi = padhi_ref[e]
            lax.fori_loop(lo, hi, lambda s, c2: (row_copy(s).start(), c2)[1], 0)
            lax.fori_loop(lo, hi, lambda s, c2: (row_copy(s).wait(), c2)[1], 0)
            return c

        lax.fori_loop(0, n_exp, per_expert, 0)

        def blk_copy(b):
            return pltpu.make_async_copy(zbuf, xs_hbm.at[pl.ds(b * tb, tb)], zsem)

        first_free = padhi_ref[n_exp - 1] // tb
        lax.fori_loop(first_free, n_blocks, lambda b, c: (blk_copy(b).start(), c)[1], 0)
        lax.fori_loop(first_free, n_blocks, lambda b, c: (blk_copy(b).wait(), c)[1], 0)

    def copy(r):
        t = i * td + r // TOP_K
        return pltpu.make_async_copy(h_hbm.at[pl.ds(t, 1)],
                                     xs_hbm.at[pl.ds(dest_ref[i * td * TOP_K + r], 1)], sem)

    def start(r, c):
        copy(r).start()
        return c

    def wait(r, c):
        copy(r).wait()
        return c

    lax.fori_loop(0, td * TOP_K, start, 0)
    lax.fori_loop(0, td * TOP_K, wait, 0)


def _dispatch(dest, pad_lo, pad_hi, h2, n_slots, n_exp, tb):
    n, d = h2.shape
    td = 256
    kern = functools.partial(_dispatch_kernel, td=td, tb=tb, n_exp=n_exp)
    return pl.pallas_call(
        kern,
        grid_spec=pltpu.PrefetchScalarGridSpec(
            num_scalar_prefetch=3,
            grid=(n // td,),
            in_specs=[pl.BlockSpec(memory_space=pl.ANY)],
            out_specs=pl.BlockSpec(memory_space=pl.ANY),
            scratch_shapes=[pltpu.VMEM((tb, d), F32), pltpu.SemaphoreType.DMA, pltpu.SemaphoreType.DMA],
        ),
        out_shape=jax.ShapeDtypeStruct((n_slots, d), F32),
        compiler_params=_cparams(("arbitrary",)),
        name="dispatch",
    )(dest, pad_lo, pad_hi, h2)


def _expert_changed(blk_e_ref, b):
    return (b == 0) | (blk_e_ref[b] != blk_e_ref[jnp.maximum(b - 1, 0)])


def _ffn_up_kernel(blk_e_ref, nvalid_ref, xs_ref, wg_ref, wu_ref, bg_ref, bu_ref, act_ref,
                   wg_scr, wu_scr):
    b = pl.program_id(1)
    valid = b < nvalid_ref[0]

    @pl.when(valid & _expert_changed(blk_e_ref, b))
    def _():
        wg_scr[...] = wg_ref[...].astype(BF16)
        wu_scr[...] = wu_ref[...].astype(BF16)

    @pl.when(valid)
    def _():
        x = xs_ref[...].astype(BF16)
        gate = jnp.dot(x, wg_scr[...], preferred_element_type=F32) + bg_ref[...]
        up = jnp.dot(x, wu_scr[...], preferred_element_type=F32) + bu_ref[...]
        gate = jnp.minimum(gate, SWIGLU_LIMIT)
        up = jnp.clip(up, -SWIGLU_LIMIT, SWIGLU_LIMIT)
        act_ref[...] = ((up + 1.0) * gate * jax.nn.sigmoid(gate * SWIGLU_ALPHA)).astype(BF16)

    @pl.when(jnp.logical_not(valid))
    def _():
        act_ref[...] = jnp.zeros_like(act_ref)


def _ffn_up(blk_e, nvalid, xs, w_gate_up, b_gate_up, tb):
    n_slots, d = xs.shape
    n_exp, _, f2 = w_gate_up.shape
    ff = f2 // 2
    tf = min(512, ff)
    nf = ff // tf
    nb = n_slots // tb

    def xmap(j, b, be, nv):
        return (jnp.minimum(b, nv[0] - 1), 0)

    def omap(j, b, be, nv):
        return (b, j)

    return pl.pallas_call(
        _ffn_up_kernel,
        grid_spec=pltpu.PrefetchScalarGridSpec(
            num_scalar_prefetch=2,
            grid=(nf, nb),
            in_specs=[
                pl.BlockSpec((tb, d), xmap),
                pl.BlockSpec((None, d, tf), lambda j, b, be, nv: (be[b], 0, j)),
                pl.BlockSpec((None, d, tf), lambda j, b, be, nv: (be[b], 0, nf + j)),
                pl.BlockSpec((None, 1, tf), lambda j, b, be, nv: (be[b], 0, j)),
                pl.BlockSpec((None, 1, tf), lambda j, b, be, nv: (be[b], 0, nf + j)),
            ],
            out_specs=pl.BlockSpec((tb, tf), omap),
            scratch_shapes=[pltpu.VMEM((d, tf), BF16), pltpu.VMEM((d, tf), BF16)],
        ),
        out_shape=jax.ShapeDtypeStruct((n_slots, ff), BF16),
        compiler_params=_cparams(("arbitrary", "arbitrary")),
        name="ffn_up",
    )(blk_e, nvalid, xs, w_gate_up, w_gate_up, b_gate_up, b_gate_up)


def _ffn_down_kernel(blk_e_ref, nvalid_ref, act_ref, wd_ref, bd_ref, out_ref, wd_scr):
    b = pl.program_id(1)
    valid = b < nvalid_ref[0]

    @pl.when(valid & _expert_changed(blk_e_ref, b))
    def _():
        wd_scr[...] = wd_ref[...].astype(BF16)

    @pl.when(valid)
    def _():
        out_ref[...] = jnp.dot(act_ref[...], wd_scr[...], preferred_element_type=F32) + bd_ref[...]

    @pl.when(jnp.logical_not(valid))
    def _():
        out_ref[...] = jnp.zeros_like(out_ref)


def _ffn_down(blk_e, nvalid, act, w_down, b_down, tb):
    n_slots, ff = act.shape
    n_exp, _, d = w_down.shape
    tn = min(1024, d)
    nn = d // tn
    nb = n_slots // tb

    def amap(j, b, be, nv):
        return (jnp.minimum(b, nv[0] - 1), 0)

    def omap(j, b, be, nv):
        return (b, j)

    return pl.pallas_call(
        _ffn_down_kernel,
        grid_spec=pltpu.PrefetchScalarGridSpec(
            num_scalar_prefetch=2,
            grid=(nn, nb),
            in_specs=[
                pl.BlockSpec((tb, ff), amap),
                pl.BlockSpec((None, ff, tn), lambda j, b, be, nv: (be[b], 0, j)),
                pl.BlockSpec((None, 1, tn), lambda j, b, be, nv: (be[b], 0, j)),
            ],
            out_specs=pl.BlockSpec((tb, tn), omap),
            scratch_shapes=[pltpu.VMEM((ff, tn), BF16)],
        ),
        out_shape=jax.ShapeDtypeStruct((n_slots, d), F32),
        compiler_params=_cparams(("arbitrary", "arbitrary")),
        name="ffn_down",
    )(blk_e, nvalid, act, w_down, b_down)


def _combine_kernel(dest_ref, outs_hbm, w_ref, x2_ref, gain_ref, y_ref, buf, sem, *, tc):
    i = pl.program_id(0)

    def copy(r):
        k = r % TOP_K
        t = r // TOP_K
        return pltpu.make_async_copy(outs_hbm.at[pl.ds(dest_ref[i * tc * TOP_K + r], 1)],
                                     buf.at[k, pl.ds(t, 1)], sem)

    def start(r, c):
        copy(r).start()
        return c

    def wait(r, c):
        copy(r).wait()
        return c

    lax.fori_loop(0, tc * TOP_K, start, 0)
    lax.fori_loop(0, tc * TOP_K, wait, 0)
    w = w_ref[...]
    y = x2_ref[...]
    for k in range(TOP_K):
        y = y + w[:, k:k + 1] * buf[k]
    ms = jnp.mean(y * y, axis=-1, keepdims=True)
    y_ref[...] = y * lax.rsqrt(ms + NORM_EPS) * gain_ref[...]


def _combine(dest, outs, w_top, x2, gain):
    n, d = x2.shape
    tc = 128
    kern = functools.partial(_combine_kernel, tc=tc)
    return pl.pallas_call(
        kern,
        grid_spec=pltpu.PrefetchScalarGridSpec(
            num_scalar_prefetch=1,
            grid=(n // tc,),
            in_specs=[
                pl.BlockSpec(memory_space=pl.ANY),
                pl.BlockSpec((tc, LANES), lambda i, dr: (i, 0)),
                pl.BlockSpec((tc, d), lambda i, dr: (i, 0)),
                pl.BlockSpec((1, d), lambda i, dr: (0, 0)),
            ],
            out_specs=pl.BlockSpec((tc, d), lambda i, dr: (i, 0)),
            scratch_shapes=[pltpu.VMEM((TOP_K, tc, d), F32), pltpu.SemaphoreType.DMA],
        ),
        out_shape=jax.ShapeDtypeStruct((n, d), F32),
        compiler_params=_cparams(("arbitrary",)),
        name="combine",
    )(dest, outs, w_top, x2, gain)


def _rope_tables(seq):
    half = DH // 2
    inv_freq = ROPE_THETA ** (-jnp.arange(half, dtype=F32) / half)
    ang = jnp.arange(seq, dtype=F32)[:, None] * inv_freq[None, :]
    cos = jnp.cos(ang)
    sin = jnp.sin(ang)
    return jnp.concatenate([cos, cos], axis=1), jnp.concatenate([-sin, sin], axis=1)


def _layer(x2d, batch, seq, norm_mix, w_in, cmp_pos_k, cmp_pos_v, w_cmp_k1, w_cmp_k2, w_cmp_v1,
           w_cmp_v2, w_gla_alpha, b_gla_alpha, gla_norm, w_proj_nsa, w_proj_gla, w_merge_gate,
           b_merge_gate, w_out, norm_moe, w_router, b_router, w_gate_up, b_gate_up, w_down, b_down,
           final_gain):
    n, d = x2d.shape
    n_exp = w_router.shape[1]
    d_merge = w_merge_gate.shape[1]
    main0 = d_merge

    o_g = NSA_Q + 6 * NSA_KV
    o_q = o_g + NSA_HEADS * 3
    o_a = o_q + 2 * GLA_QK + 2 * GLA_V
    w_main = jnp.concatenate([w_in[:, :o_g], w_in[:, o_q:o_a]], axis=1)
    w_all = jnp.concatenate([w_merge_gate, w_main], axis=1).astype(BF16)
    n_small = NSA_HEADS * 3 + GLA_RANK
    w_small = jnp.concatenate([w_in[:, o_g:o_q], w_in[:, o_a:],
                               jnp.zeros((d, LANES - n_small), F32)], axis=1).astype(BF16)
    bias_all = jnp.concatenate([b_merge_gate, jnp.zeros((MAIN_W,), F32)])[None, :]
    cos2, sin2 = _rope_tables(seq)

    big, small = _proj(x2d, norm_mix[None, :], w_all, w_small, bias_all, cos2, sin2, seq, d_merge)

    pos = jnp.stack([cmp_pos_k, cmp_pos_v])
    w1 = jnp.stack([w_cmp_k1, w_cmp_v1]).astype(BF16)
    w2 = jnp.stack([w_cmp_k2, w_cmp_v2]).astype(BF16)
    cmp_kv = _compress(big, pos, w1, w2, batch, seq, (main0 + NSA_Q) // DH)
    ncp = cmp_kv.shape[3]
    n_blk = seq // SEL_LEN
    cstart = jnp.arange(ncp)[:, None] * CMP_STRIDE
    blk = jnp.arange(LANES)[None, :]
    cover = ((cstart < (blk + 1) * SEL_LEN) & (cstart + CMP_LEN > blk * SEL_LEN)
             & (blk < n_blk) & (jnp.arange(ncp)[:, None] < seq // CMP_STRIDE - 1)).astype(F32)
    expand = (jnp.arange(LANES)[:, None] == (jnp.arange(seq)[None, :] // SEL_LEN)).astype(BF16)
    o_nsa = _nsa(big, cmp_kv, small, cos2, sin2, cover, expand, batch, seq, main0)

    tri_c = jnp.tril(jnp.ones((GLA_CHUNK, GLA_CHUNK), F32))
    a_col = NSA_HEADS * 3
    w_alpha = jnp.zeros((LANES, GLA_QK), F32).at[a_col:a_col + GLA_RANK].set(w_gla_alpha)
    o_gla = _gla(big, small, w_alpha, b_gla_alpha[None, :], gla_norm[None, :], tri_c,
                 batch, seq, main0)

    wr = jnp.concatenate([w_router, jnp.zeros((d, LANES - n_exp), F32)], axis=1)
    wr_hi = wr.astype(BF16)
    wr_lo = (wr - wr_hi.astype(F32)).astype(BF16)
    br = jnp.concatenate([b_router, jnp.zeros((LANES - n_exp,), F32)])[None, :]
    x2, h2, logits = _mix(o_nsa, o_gla, big, x2d, w_proj_nsa.astype(BF16), w_proj_gla.astype(BF16),
                          w_out.astype(BF16), norm_moe[None, :], wr_hi, wr_lo, br)

    tr = 256
    tri_r = (jnp.arange(tr)[:, None] > jnp.arange(tr)[None, :]).astype(BF16)
    ridx, w_top, counts = _route(logits, tri_r, n_exp)

    tb = 256
    nk = n * TOP_K
    n_blocks = -(-nk // tb) + n_exp
    n_slots = n_blocks * tb
    cnt = counts[0, :n_exp].astype(jnp.int32)
    padded = (cnt + tb - 1) // tb * tb
    pad_end = jnp.cumsum(padded)
    pad_start = pad_end - padded
    top_e = ridx[:, :TOP_K]
    dest = (pad_start[top_e] + ridx[:, TOP_K:2 * TOP_K]).reshape(nk)
    blk_first = jnp.arange(n_blocks, dtype=jnp.int32) * tb
    blk_e = jnp.minimum(jnp.sum((pad_end[None, :] <= blk_first[:, None]).astype(jnp.int32), axis=1),
                        n_exp - 1)
    nvalid = (pad_end[-1:] // tb).astype(jnp.int32)

    xs = _dispatch(dest, pad_start + cnt, pad_end, h2, n_slots, n_exp, tb)
    act = _ffn_up(blk_e, nvalid, xs, w_gate_up, b_gate_up[:, None, :], tb)
    outs = _ffn_down(blk_e, nvalid, act, w_down, b_down[:, None, :], tb)
    return _combine(dest, outs, w_top, x2, final_gain)


def kernel(x, norm_mix, w_in, cmp_pos_k, cmp_pos_v, w_cmp_k1, w_cmp_k2, w_cmp_v1, w_cmp_v2,
           w_gla_alpha, b_gla_alpha, gla_norm, w_proj_nsa, w_proj_gla, w_merge_gate, b_merge_gate,
           w_out, norm_moe, w_router, b_router, w_gate_up, b_gate_up, w_down, b_down, norm_final):
    batch, seq, d = x.shape
    depth = w_in.shape[0]
    assert depth == 1, "the final norm is fused into the (single) layer's combine stage"
    y = _layer(x.reshape(batch * seq, d), batch, seq, norm_mix[0], w_in[0], cmp_pos_k[0],
               cmp_pos_v[0], w_cmp_k1[0], w_cmp_k2[0], w_cmp_v1[0], w_cmp_v2[0], w_gla_alpha[0],
               b_gla_alpha[0], gla_norm[0], w_proj_nsa[0], w_proj_gla[0], w_merge_gate[0],
               b_merge_gate[0], w_out[0], norm_moe[0], w_router[0], b_router[0], w_gate_up[0],
               b_gate_up[0], w_down[0], b_down[0], norm_final[None, :])
    return y.reshape(batch, seq, d)
```

```python
import functools

import jax
import jax.numpy as jnp
from jax import lax
from jax.experimental import pallas as pl
from jax.experimental.pallas import tpu as pltpu

F32 = jnp.float32
BF16 = jnp.bfloat16

NORM_EPS = 1e-5
ROPE_THETA = 10000.0
NEG_INF = -1e30

NSA_HEADS = 8
NSA_GROUPS = 2
NSA_HPG = NSA_HEADS // NSA_GROUPS
DH = 128
CMP_LEN = 32
CMP_STRIDE = 16
SEL_LEN = 64
SEL_TOPK = 16
SEL_FORCE = 1e3
WINDOW = 512

GLA_HEADS = 4
GLA_DK = 128
GLA_DV = 256
GLA_RANK = 16
GLA_TAU = 16.0
GLA_CHUNK = 64

TOP_K = 4
SWIGLU_LIMIT = 7.0
SWIGLU_ALPHA = 1.702

LANES = 128
VMEM_LIMIT = 56 * 1024 * 1024

NSA_Q = NSA_HEADS * DH
NSA_KV = NSA_GROUPS * DH
GLA_QK = GLA_HEADS * GLA_DK
GLA_V = GLA_HEADS * GLA_DV
MAIN_W = NSA_Q + 6 * NSA_KV + 2 * GLA_QK + 2 * GLA_V


def _cparams(sem, vmem=VMEM_LIMIT):
    return pltpu.CompilerParams(dimension_semantics=sem, vmem_limit_bytes=vmem)


def _rope(x, cos, sin_signed):
    return x * cos + pltpu.roll(x, DH // 2, axis=1) * sin_signed


def _proj_kernel(x_ref, gain_ref, w_ref, wsmall_ref, bias_ref, cos_ref, sin_ref,
                 big_ref, small_ref, h_scr, *, n_merge_tiles, rope_tiles):
    j = pl.program_id(1)

    @pl.when(j == 0)
    def _():
        x = x_ref[...]
        ms = jnp.mean(x * x, axis=-1, keepdims=True)
        hb = (x * lax.rsqrt(ms + NORM_EPS) * gain_ref[...]).astype(BF16)
        h_scr[...] = hb
        small_ref[...] = jnp.dot(hb, wsmall_ref[...], preferred_element_type=F32)

    acc = jnp.dot(h_scr[...], w_ref[...], preferred_element_type=F32)
    is_merge = j < n_merge_tiles
    is_rope = (j == rope_tiles[0]) | (j == rope_tiles[1])

    @pl.when(is_merge)
    def _():
        big_ref[...] = jax.nn.sigmoid(acc + bias_ref[...]).astype(BF16)

    @pl.when(is_rope)
    def _():
        cos = cos_ref[...]
        sin = sin_ref[...]
        for g in range(NSA_GROUPS):
            sl = slice(g * DH, (g + 1) * DH)
            big_ref[:, sl] = _rope(acc[:, sl], cos, sin).astype(BF16)
        big_ref[:, NSA_KV:] = acc[:, NSA_KV:].astype(BF16)

    @pl.when(jnp.logical_not(is_merge | is_rope))
    def _():
        big_ref[...] = acc.astype(BF16)


def _proj(x2d, gain, w_all, w_small, bias_all, cos2, sin2, seq, d_merge):
    n, d = x2d.shape
    width = w_all.shape[1]
    tm = min(1024, seq)
    tn = 512
    n_merge_tiles = d_merge // tn
    rope_tiles = (n_merge_tiles + 3, n_merge_tiles + 4)
    nsb = seq // tm
    kern = functools.partial(_proj_kernel, n_merge_tiles=n_merge_tiles, rope_tiles=rope_tiles)
    return pl.pallas_call(
        kern,
        grid=(n // tm, width // tn),
        in_specs=[
            pl.BlockSpec((tm, d), lambda i, j: (i, 0)),
            pl.BlockSpec((1, d), lambda i, j: (0, 0)),
            pl.BlockSpec((d, tn), lambda i, j: (0, j)),
            pl.BlockSpec((d, LANES), lambda i, j: (0, 0)),
            pl.BlockSpec((1, tn), lambda i, j: (0, j)),
            pl.BlockSpec((tm, DH), lambda i, j: (i % nsb, 0)),
            pl.BlockSpec((tm, DH), lambda i, j: (i % nsb, 0)),
        ],
        out_specs=[
            pl.BlockSpec((tm, tn), lambda i, j: (i, j)),
            pl.BlockSpec((tm, LANES), lambda i, j: (i, 0)),
        ],
        out_shape=[
            jax.ShapeDtypeStruct((n, width), BF16),
            jax.ShapeDtypeStruct((n, LANES), F32),
        ],
        scratch_shapes=[pltpu.VMEM((tm, d), BF16)],
        compiler_params=_cparams(("parallel", "arbitrary")),
        name="proj",
    )(x2d, gain, w_all, w_small, bias_all, cos2, sin2)


def _compress_kernel(kv_ref, pos_ref, w1_ref, w2_ref, out_ref, scr, *, seq, ncp):
    nreal = seq // CMP_STRIDE
    scr[0:seq, :] = kv_ref[...].astype(F32)
    scr[seq:seq + CMP_LEN, :] = jnp.zeros((CMP_LEN, DH), F32)
    acc = jnp.zeros((nreal, w1_ref.shape[1]), F32)
    for l in range(CMP_LEN):
        a = scr[pl.ds(l, nreal, stride=CMP_STRIDE), :] + pos_ref[l:l + 1, :]
        acc = acc + jnp.dot(a.astype(BF16), w1_ref[l * DH:(l + 1) * DH, :],
                            preferred_element_type=F32)
    hid = jax.nn.gelu(acc)
    out = jnp.dot(hid.astype(BF16), w2_ref[...], preferred_element_type=F32)
    row = lax.broadcasted_iota(jnp.int32, out.shape, 0)
    out = jnp.where(row < nreal - 1, out, 0.0).astype(BF16)
    if ncp > nreal:
        out = jnp.concatenate([out, jnp.zeros((ncp - nreal, DH), BF16)], axis=0)
    out_ref[...] = out


def _compress(big, pos, w1, w2, batch, seq, col0):
    ncp = max(seq // CMP_STRIDE, LANES)
    kern = functools.partial(_compress_kernel, seq=seq, ncp=ncp)
    hid = w1.shape[2]
    return pl.pallas_call(
        kern,
        grid=(batch, NSA_GROUPS, 2),
        in_specs=[
            pl.BlockSpec((seq, DH), lambda b, g, t: (b, col0 + 2 * t + g)),
            pl.BlockSpec((None, CMP_LEN, DH), lambda b, g, t: (t, 0, 0)),
            pl.BlockSpec((None, CMP_LEN * DH, hid), lambda b, g, t: (t, 0, 0)),
            pl.BlockSpec((None, hid, DH), lambda b, g, t: (t, 0, 0)),
        ],
        out_specs=pl.BlockSpec((None, None, None, ncp, DH), lambda b, g, t: (b, g, t, 0, 0)),
        out_shape=jax.ShapeDtypeStruct((batch, NSA_GROUPS, 2, ncp, DH), BF16),
        scratch_shapes=[pltpu.VMEM((seq + CMP_LEN, DH), F32)],
        compiler_params=_cparams(("parallel", "parallel", "arbitrary")),
        name="compress",
    )(big, pos, w1, w2)


def _stack_heads(t):
    return jnp.concatenate([t[:, h * DH:(h + 1) * DH] for h in range(NSA_HPG)], axis=0)


def _nsa_kernel(q_ref, kc_ref, vc_ref, ks_ref, vs_ref, kw_ref, vw_ref, cos_ref, sin_ref,
                small_ref, cover_ref, expand_ref, o_ref, bias_scr, *, seq, tq, ck):
    i = pl.program_id(2)
    g = pl.program_id(1)
    q0 = i * tq
    scale = DH ** -0.5
    rows = NSA_HPG * tq
    ncp = kc_ref.shape[0]
    n_blk = seq // SEL_LEN
    wspan = tq + WINDOW

    q = q_ref[...]
    qs = _stack_heads(q)
    pos_q = q0 + lax.broadcasted_iota(jnp.int32, (tq, 1), 0)
    pos_rows = jnp.concatenate([pos_q] * NSA_HPG, axis=0)

    s = lax.dot_general(qs, kc_ref[...], (((1,), (1,)), ((), ())),
                        preferred_element_type=F32) * scale
    n_idx = lax.broadcasted_iota(jnp.int32, (1, ncp), 1)
    cmask = (n_idx * CMP_STRIDE + (CMP_LEN - 1)) <= pos_rows
    s = jnp.where(cmask, s, NEG_INF)
    m = jnp.max(s, axis=-1, keepdims=True)
    e = jnp.exp(s - m)
    p = jnp.where(cmask, e / jnp.sum(e, axis=-1, keepdims=True), 0.0)
    o_cmp = jnp.dot(p.astype(BF16), vc_ref[...], preferred_element_type=F32)

    psum = p[0:tq]
    for h in range(1, NSA_HPG):
        psum = psum + p[h * tq:(h + 1) * tq]
    imp = jnp.dot(psum, cover_ref[...], preferred_element_type=F32,
                  precision=lax.Precision.HIGHEST)
    blk = lax.broadcasted_iota(jnp.int32, (tq, LANES), 1)
    t_blk = pos_q // SEL_LEN
    forced = (blk == 0) | (blk == t_blk) | (blk == t_blk - 1)
    bonus = jnp.where(blk > t_blk, -SEL_FORCE, jnp.where(forced, SEL_FORCE, 0.0))
    val = jnp.where(blk < n_blk, imp + bonus, -3e38)
    rank = jnp.zeros((tq, LANES), F32)
    for c in range(n_blk):
        vc = val[:, c:c + 1]
        beats = (vc > val) | ((vc == val) & (blk > c))
        rank = rank + jnp.where(beats, 1.0, 0.0)
    n_sel = min(SEL_TOPK, n_blk)
    sel = jnp.where((rank < n_sel) & (blk < n_blk), 1.0, 0.0).astype(BF16)
    selfull = jnp.dot(sel, expand_ref[...], preferred_element_type=F32)
    kpos = lax.broadcasted_iota(jnp.int32, (1, seq), 1)
    bias = jnp.where((selfull > 0.5) & (kpos <= pos_q), 0.0, NEG_INF)
    for c in range(seq // ck):
        bias_scr[c] = bias[:, c * ck:(c + 1) * ck]

    cos = cos_ref[...]
    sin = sin_ref[...]
    qr = jnp.concatenate(
        [(_rope(q[:, h * DH:(h + 1) * DH].astype(F32), cos, sin) * scale).astype(BF16)
         for h in range(NSA_HPG)], axis=0)

    n_chunks = (q0 + tq + ck - 1) // ck

    def sel_body(c, carry):
        m_i, l_i, acc = carry
        k0 = pl.multiple_of(c * ck, ck)
        kblk = ks_ref[pl.ds(k0, ck), :]
        vblk = vs_ref[pl.ds(k0, ck), :]
        sc = lax.dot_general(qr, kblk, (((1,), (1,)), ((), ())), preferred_element_type=F32)
        b = bias_scr[c]
        sc = sc + jnp.concatenate([b] * NSA_HPG, axis=0)
        m_new = jnp.maximum(m_i, jnp.max(sc, axis=-1, keepdims=True))
        alpha = jnp.exp(m_i - m_new)
        pc = jnp.exp(sc - m_new)
        l_new = alpha * l_i + jnp.sum(pc, axis=-1, keepdims=True)
        acc = alpha * acc + jnp.dot(pc.astype(BF16), vblk, preferred_element_type=F32)
        return m_new, l_new, acc

    m0 = jnp.full((rows, 1), NEG_INF, F32)
    l0 = jnp.zeros((rows, 1), F32)
    a0 = jnp.zeros((rows, DH), F32)
    _, l_s, acc_s = lax.fori_loop(0, n_chunks, sel_body, (m0, l0, a0))
    o_sel = acc_s / l_s

    kstart = pl.multiple_of(jnp.maximum(q0 - WINDOW, 0), tq)
    kw = kw_ref[pl.ds(kstart, wspan), :]
    vw = vw_ref[pl.ds(kstart, wspan), :]
    sw = lax.dot_general(qr, kw, (((1,), (1,)), ((), ())), preferred_element_type=F32)
    kp = kstart + lax.broadcasted_iota(jnp.int32, (1, wspan), 1)
    wmask = (kp <= pos_rows) & (kp > pos_rows - WINDOW)
    sw = jnp.where(wmask, sw, NEG_INF)
    mw = jnp.max(sw, axis=-1, keepdims=True)
    ew = jnp.exp(sw - mw)
    pw = ew / jnp.sum(ew, axis=-1, keepdims=True)
    o_win = jnp.dot(pw.astype(BF16), vw, preferred_element_type=F32)

    gates = jax.nn.sigmoid(small_ref[...])
    outs = []
    for h in range(NSA_HPG):
        base = (g * NSA_HPG + h) * 3
        r = slice(h * tq, (h + 1) * tq)
        o_h = (_lane_col(gates, base, tq) * o_cmp[r]
               + _lane_col(gates, base + 1, tq) * o_sel[r]
               + _lane_col(gates, base + 2, tq) * o_win[r])
        outs.append(o_h)
    o_ref[...] = jnp.concatenate(outs, axis=1).astype(BF16)


def _lane_col(t, col, rows):
    lane = lax.broadcasted_iota(jnp.int32, (rows, LANES), 1)
    return jnp.sum(jnp.where(lane == col, t, 0.0), axis=-1, keepdims=True)


def _nsa(big, cmp_kv, small, cos2, sin2, cover, expand, batch, seq, main0):
    tq = 128
    ck = min(512, seq)
    nq = seq // tq
    ncp = cmp_kv.shape[3]
    c128 = main0 // DH
    kern = functools.partial(_nsa_kernel, seq=seq, tq=tq, ck=ck)
    kv_col = c128 + NSA_Q // DH

    def kvspec(which):
        return pl.BlockSpec((seq, DH), lambda b, g, i: (b, kv_col + 2 * which + g))

    return pl.pallas_call(
        kern,
        grid=(batch, NSA_GROUPS, nq),
        in_specs=[
            pl.BlockSpec((tq, NSA_HPG * DH), lambda b, g, i: (b * nq + i, main0 // (NSA_HPG * DH) + g)),
            pl.BlockSpec((None, None, None, ncp, DH), lambda b, g, i: (b, g, 0, 0, 0)),
            pl.BlockSpec((None, None, None, ncp, DH), lambda b, g, i: (b, g, 1, 0, 0)),
            kvspec(2), kvspec(3), kvspec(4), kvspec(5),
            pl.BlockSpec((tq, DH), lambda b, g, i: (i, 0)),
            pl.BlockSpec((tq, DH), lambda b, g, i: (i, 0)),
            pl.BlockSpec((tq, LANES), lambda b, g, i: (b * nq + i, 0)),
            pl.BlockSpec((ncp, LANES), lambda b, g, i: (0, 0)),
            pl.BlockSpec((LANES, seq), lambda b, g, i: (0, 0)),
        ],
        out_specs=pl.BlockSpec((tq, NSA_HPG * DH), lambda b, g, i: (b * nq + i, g)),
        out_shape=jax.ShapeDtypeStruct((batch * seq, NSA_Q), BF16),
        scratch_shapes=[pltpu.VMEM((seq // ck, tq, ck), F32)],
        compiler_params=_cparams(("parallel", "parallel", "arbitrary")),
        name="nsa",
    )(big, cmp_kv, cmp_kv, big, big, big, big, cos2, sin2, small, cover, expand)


def _gla_kernel(q_ref, k_ref, v_ref, r_ref, small_ref, wa_ref, ba_ref, gain_ref, tri_ref,
                o_ref, la_scr, st_scr, *, seq):
    C = GLA_CHUNK
    nc = seq // C
    z = jnp.dot(small_ref[...], wa_ref[...], preferred_element_type=F32,
                precision=lax.Precision.HIGHEST) + ba_ref[...]
    la_scr[...] = jax.nn.log_sigmoid(z) / GLA_TAU
    st_scr[...] = jnp.zeros_like(st_scr)
    ri = lax.broadcasted_iota(jnp.int32, (C, C), 0)
    ci = lax.broadcasted_iota(jnp.int32, (C, C), 1)
    causal = ri >= ci
    qscale = GLA_DK ** -0.5

    def body(c, carry):
        r0 = pl.multiple_of(c * C, C)
        la = la_scr[pl.ds(r0, C), :]
        cum = jnp.dot(tri_ref[...], la, preferred_element_type=F32,
                      precision=lax.Precision.HIGHEST)
        cum_last = cum[C - 1:C, :]
        qc = q_ref[pl.ds(r0, C), :].astype(F32) * qscale
        kc = k_ref[pl.ds(r0, C), :].astype(F32)
        vc = v_ref[pl.ds(r0, C), :]
        q_t = (qc * jnp.exp(cum)).astype(BF16)
        k_t = (kc * jnp.exp(-cum)).astype(BF16)
        attn = lax.dot_general(q_t, k_t, (((1,), (1,)), ((), ())), preferred_element_type=F32)
        attn = jnp.where(causal, attn, 0.0)
        o = jnp.dot(attn.astype(BF16), vc, preferred_element_type=F32)
        st = st_scr[...]
        o = o + lax.dot_general(q_t, st.astype(BF16), (((1,), (1,)), ((), ())),
                                preferred_element_type=F32)
        k_state = (kc * jnp.exp(cum_last - cum)).astype(BF16)
        d_st = lax.dot_general(vc, k_state, (((0,), (0,)), ((), ())),
                               preferred_element_type=F32)
        st_scr[...] = st * jnp.exp(cum_last) + d_st
        o = o * lax.rsqrt(jnp.mean(o * o, axis=-1, keepdims=True) + NORM_EPS)
        o = o * gain_ref[...]
        rr = r_ref[pl.ds(r0, C), :].astype(F32)
        o_ref[pl.ds(r0, C), :] = (o * (rr * jax.nn.sigmoid(rr))).astype(BF16)
        return carry

    lax.fori_loop(0, nc, body, 0)


def _gla(big, small, w_alpha, b_alpha, gain, tri, batch, seq, main0):
    qcol = (main0 + NSA_Q + 6 * NSA_KV) // GLA_DK
    kcol = qcol + GLA_QK // GLA_DK
    vcol = (main0 + NSA_Q + 6 * NSA_KV + 2 * GLA_QK) // GLA_DV
    rcol = vcol + GLA_V // GLA_DV
    kern = functools.partial(_gla_kernel, seq=seq)
    return pl.pallas_call(
        kern,
        grid=(batch, GLA_HEADS),
        in_specs=[
            pl.BlockSpec((seq, GLA_DK), lambda b, h: (b, qcol + h)),
            pl.BlockSpec((seq, GLA_DK), lambda b, h: (b, kcol + h)),
            pl.BlockSpec((seq, GLA_DV), lambda b, h: (b, vcol + h)),
            pl.BlockSpec((seq, GLA_DV), lambda b, h: (b, rcol + h)),
            pl.BlockSpec((seq, LANES), lambda b, h: (b, 0)),
            pl.BlockSpec((LANES, GLA_DK), lambda b, h: (0, h)),
            pl.BlockSpec((1, GLA_DK), lambda b, h: (0, h)),
            pl.BlockSpec((1, GLA_DV), lambda b, h: (0, h)),
            pl.BlockSpec((GLA_CHUNK, GLA_CHUNK), lambda b, h: (0, 0)),
        ],
        out_specs=pl.BlockSpec((seq, GLA_DV), lambda b, h: (b, h)),
        out_shape=jax.ShapeDtypeStruct((batch * seq, GLA_V), BF16),
        scratch_shapes=[pltpu.VMEM((seq, GLA_DK), F32), pltpu.VMEM((GLA_DV, GLA_DK), F32)],
        compiler_params=_cparams(("parallel", "parallel")),
        name="gla",
    )(big, big, big, big, small, w_alpha, b_alpha, gain, tri)


def _mix_kernel(on_ref, og_ref, ma_ref, mb_ref, x_ref, wpn_ref, wpg_ref, wo_ref, gain_ref,
                wrh_ref, wrl_ref, br_ref, x2_ref, h2_ref, lg_ref):
    a = jnp.dot(on_ref[...], wpn_ref[...], preferred_element_type=F32)
    b = jnp.dot(og_ref[...], wpg_ref[...], preferred_element_type=F32)
    mixed = ma_ref[...].astype(F32) * a + mb_ref[...].astype(F32) * b
    x2 = x_ref[...] + jnp.dot(mixed.astype(BF16), wo_ref[...], preferred_element_type=F32)
    x2_ref[...] = x2
    ms = jnp.mean(x2 * x2, axis=-1, keepdims=True)
    h2 = x2 * lax.rsqrt(ms + NORM_EPS) * gain_ref[...]
    h2_ref[...] = h2
    hi = h2.astype(BF16)
    lo = (h2 - hi.astype(F32)).astype(BF16)
    lg = (jnp.dot(hi, wrh_ref[...], preferred_element_type=F32)
          + jnp.dot(lo, wrh_ref[...], preferred_element_type=F32)
          + jnp.dot(hi, wrl_ref[...], preferred_element_type=F32))
    lg_ref[...] = lg + br_ref[...]


def _mix(o_nsa, o_gla, big, x2d, wpn, wpg, wo, gain, wr_hi, wr_lo, br):
    n, d = x2d.shape
    tm = 256
    const = lambda i: (0, 0)
    return pl.pallas_call(
        _mix_kernel,
        grid=(n // tm,),
        in_specs=[
            pl.BlockSpec((tm, NSA_Q), lambda i: (i, 0)),
            pl.BlockSpec((tm, GLA_V), lambda i: (i, 0)),
            pl.BlockSpec((tm, d), lambda i: (i, 0)),
            pl.BlockSpec((tm, d), lambda i: (i, 1)),
            pl.BlockSpec((tm, d), lambda i: (i, 0)),
            pl.BlockSpec((NSA_Q, d), const, pipeline_mode=pl.Buffered(1)),
            pl.BlockSpec((GLA_V, d), const, pipeline_mode=pl.Buffered(1)),
            pl.BlockSpec((d, d), const, pipeline_mode=pl.Buffered(1)),
            pl.BlockSpec((1, d), const),
            pl.BlockSpec((d, LANES), const),
            pl.BlockSpec((d, LANES), const),
            pl.BlockSpec((1, LANES), const),
        ],
        out_specs=[
            pl.BlockSpec((tm, d), lambda i: (i, 0)),
            pl.BlockSpec((tm, d), lambda i: (i, 0)),
            pl.BlockSpec((tm, LANES), lambda i: (i, 0)),
        ],
        out_shape=[
            jax.ShapeDtypeStruct((n, d), F32),
            jax.ShapeDtypeStruct((n, d), F32),
            jax.ShapeDtypeStruct((n, LANES), F32),
        ],
        compiler_params=_cparams(("parallel",)),
        name="mix",
    )(o_nsa, o_gla, big, big, x2d, wpn, wpg, wo, gain, wr_hi, wr_lo, br)


def _route_kernel(lg_ref, tri_ref, idx_ref, w_ref, cnt_ref, carry_scr, *, n_exp):
    i = pl.program_id(0)
    tr = lg_ref.shape[0]

    @pl.when(i == 0)
    def _():
        carry_scr[...] = jnp.zeros_like(carry_scr)

    lane = lax.broadcasted_iota(jnp.int32, (tr, LANES), 1)
    work = jnp.where(lane < n_exp, lg_ref[...], -3e38)
    onehots, vals, idxs = [], [], []
    for _ in range(TOP_K):
        mval = jnp.max(work, axis=-1, keepdims=True)
        idx = jnp.min(jnp.where(work == mval, lane, LANES), axis=-1, keepdims=True)
        oh = lane == idx
        work = jnp.where(oh, -3e38, work)
        onehots.append(oh)
        vals.append(mval)
        idxs.append(idx)
    exps = [jnp.exp(v - vals[0]) for v in vals]
    den = exps[0] + exps[1] + exps[2] + exps[3]
    onehot = jnp.zeros((tr, LANES), F32)
    for oh in onehots:
        onehot = onehot + jnp.where(oh, 1.0, 0.0)
    before = jnp.dot(tri_ref[...], onehot.astype(BF16), preferred_element_type=F32) + carry_scr[...]
    carry_scr[...] = carry_scr[...] + jnp.sum(onehot, axis=0, keepdims=True)
    idx_out = jnp.zeros((tr, LANES), jnp.int32)
    w_out = jnp.zeros((tr, LANES), F32)
    for k in range(TOP_K):
        rank_k = jnp.sum(jnp.where(onehots[k], before, 0.0), axis=-1, keepdims=True)
        idx_out = jnp.where(lane == k, idxs[k], idx_out)
        idx_out = jnp.where(lane == TOP_K + k, rank_k.astype(jnp.int32), idx_out)
        w_out = jnp.where(lane == k, exps[k] / den, w_out)
    idx_ref[...] = idx_out
    w_ref[...] = w_out
    cnt_ref[...] = carry_scr[...]


def _route(logits, tri, n_exp):
    n = logits.shape[0]
    tr = tri.shape[0]
    kern = functools.partial(_route_kernel, n_exp=n_exp)
    return pl.pallas_call(
        kern,
        grid=(n // tr,),
        in_specs=[pl.BlockSpec((tr, LANES), lambda i: (i, 0)),
                  pl.BlockSpec((tr, tr), lambda i: (0, 0))],
        out_specs=[pl.BlockSpec((tr, LANES), lambda i: (i, 0)),
                   pl.BlockSpec((tr, LANES), lambda i: (i, 0)),
                   pl.BlockSpec((1, LANES), lambda i: (0, 0))],
        out_shape=[jax.ShapeDtypeStruct((n, LANES), jnp.int32),
                   jax.ShapeDtypeStruct((n, LANES), F32),
                   jax.ShapeDtypeStruct((1, LANES), F32)],
        scratch_shapes=[pltpu.VMEM((1, LANES), F32)],
        compiler_params=_cparams(("arbitrary",)),
        name="route",
    )(logits, tri)


def _dispatch_kernel(dest_ref, padlo_ref, padhi_ref, h_ref, xs_hbm, zbuf, sem, zsem, *, td, tb, n_exp):
    i = pl.program_id(0)
    n_blocks = xs_hbm.shape[0] // tb

    @pl.when(i == 0)
    def _():
        zbuf[...] = jnp.zeros_like(zbuf)

        def row_copy(s):
            return pltpu.make_async_copy(zbuf.at[pl.ds(0, 1)], xs_hbm.at[pl.ds(s, 1)], zsem)

        def per_expert(e, c):
            lo = padlo_ref[e]
            hi = padhi_ref[e]
            lax.fori_loop(lo, hi, lambda s, c2: (row_copy(s).start(), c2)[1], 0)
            lax.fori_loop(lo, hi, lambda s, c2: (row_copy(s).wait(), c2)[1], 0)
            return c

        lax.fori_loop(0, n_exp, per_expert, 0)

        def blk_copy(b):
            return pltpu.make_async_copy(zbuf, xs_hbm.at[pl.ds(b * tb, tb)], zsem)

        first_free = padhi_ref[n_exp - 1] // tb
        lax.fori_loop(first_free, n_blocks, lambda b, c: (blk_copy(b).start(), c)[1], 0)
        lax.fori_loop(first_free, n_blocks, lambda b, c: (blk_copy(b).wait(), c)[1], 0)

    base = i * td * TOP_K

    def start(t, c):
        for k in range(TOP_K):
            pltpu.make_async_copy(h_ref.at[pl.ds(t, 1)],
                                  xs_hbm.at[pl.ds(dest_ref[base + t * TOP_K + k], 1)], sem).start()
        return c

    lax.fori_loop(0, td, start, 0, unroll=2)
    for k in range(TOP_K):
        pltpu.make_async_copy(h_ref, xs_hbm.at[pl.ds(0, td)], sem).wait()


def _dispatch(dest, pad_lo, pad_hi, h2, n_slots, n_exp, tb):
    n, d = h2.shape
    td = 256
    kern = functools.partial(_dispatch_kernel, td=td, tb=tb, n_exp=n_exp)
    return pl.pallas_call(
        kern,
        grid_spec=pltpu.PrefetchScalarGridSpec(
            num_scalar_prefetch=3,
            grid=(n // td,),
            in_specs=[pl.BlockSpec((td, d), lambda i, dr, lo, hi: (i, 0))],
            out_specs=pl.BlockSpec(memory_space=pl.ANY),
            scratch_shapes=[pltpu.VMEM((tb, d), F32), pltpu.SemaphoreType.DMA, pltpu.SemaphoreType.DMA],
        ),
        out_shape=jax.ShapeDtypeStruct((n_slots, d), F32),
        compiler_params=_cparams(("arbitrary",)),
        name="dispatch",
    )(dest, pad_lo, pad_hi, h2)


def _expert_changed(blk_e_ref, b):
    return (b == 0) | (blk_e_ref[b] != blk_e_ref[jnp.maximum(b - 1, 0)])


def _ffn_up_kernel(blk_e_ref, nvalid_ref, xs_ref, wg_ref, wu_ref, bg_ref, bu_ref, act_ref,
                   wg_scr, wu_scr):
    b = pl.program_id(1)
    valid = b < nvalid_ref[0]

    @pl.when(valid & _expert_changed(blk_e_ref, b))
    def _():
        wg_scr[...] = wg_ref[...].astype(BF16)
        wu_scr[...] = wu_ref[...].astype(BF16)

    @pl.when(valid)
    def _():
        x = xs_ref[...].astype(BF16)
        gate = jnp.dot(x, wg_scr[...], preferred_element_type=F32) + bg_ref[...]
        up = jnp.dot(x, wu_scr[...], preferred_element_type=F32) + bu_ref[...]
        gate = jnp.minimum(gate, SWIGLU_LIMIT)
        up = jnp.clip(up, -SWIGLU_LIMIT, SWIGLU_LIMIT)
        act_ref[...] = ((up + 1.0) * gate * jax.nn.sigmoid(gate * SWIGLU_ALPHA)).astype(BF16)

    @pl.when(jnp.logical_not(valid))
    def _():
        act_ref[...] = jnp.zeros_like(act_ref)


def _ffn_up(blk_e, nvalid, xs, w_gate_up, b_gate_up, tb):
    n_slots, d = xs.shape
    n_exp, _, f2 = w_gate_up.shape
    ff = f2 // 2
    tf = min(1024, ff)
    nf = ff // tf
    nb = n_slots // tb

    def xmap(j, b, be, nv):
        return (jnp.minimum(b, nv[0] - 1), 0)

    def omap(j, b, be, nv):
        return (b, j)

    return pl.pallas_call(
        _ffn_up_kernel,
        grid_spec=pltpu.PrefetchScalarGridSpec(
            num_scalar_prefetch=2,
            grid=(nf, nb),
            in_specs=[
                pl.BlockSpec((tb, d), xmap),
                pl.BlockSpec((None, d, tf), lambda j, b, be, nv: (be[b], 0, j)),
                pl.BlockSpec((None, d, tf), lambda j, b, be, nv: (be[b], 0, nf + j)),
                pl.BlockSpec((None, 1, tf), lambda j, b, be, nv: (be[b], 0, j)),
                pl.BlockSpec((None, 1, tf), lambda j, b, be, nv: (be[b], 0, nf + j)),
            ],
            out_specs=pl.BlockSpec((tb, tf), omap),
            scratch_shapes=[pltpu.VMEM((d, tf), BF16), pltpu.VMEM((d, tf), BF16)],
        ),
        out_shape=jax.ShapeDtypeStruct((n_slots, ff), BF16),
        compiler_params=_cparams(("arbitrary", "arbitrary")),
        name="ffn_up",
    )(blk_e, nvalid, xs, w_gate_up, w_gate_up, b_gate_up, b_gate_up)


def _ffn_down_kernel(blk_e_ref, nvalid_ref, act_ref, wd_ref, bd_ref, out_ref, wd_scr):
    b = pl.program_id(1)
    valid = b < nvalid_ref[0]

    @pl.when(valid & _expert_changed(blk_e_ref, b))
    def _():
        wd_scr[...] = wd_ref[...].astype(BF16)

    @pl.when(valid)
    def _():
        out_ref[...] = jnp.dot(act_ref[...], wd_scr[...], preferred_element_type=F32) + bd_ref[...]

    @pl.when(jnp.logical_not(valid))
    def _():
        out_ref[...] = jnp.zeros_like(out_ref)


def _ffn_down(blk_e, nvalid, act, w_down, b_down, tb):
    n_slots, ff = act.shape
    n_exp, _, d = w_down.shape
    tn = min(2048, d)
    nn = d // tn
    nb = n_slots // tb

    def amap(j, b, be, nv):
        return (jnp.minimum(b, nv[0] - 1), 0)

    def omap(j, b, be, nv):
        return (b, j)

    return pl.pallas_call(
        _ffn_down_kernel,
        grid_spec=pltpu.PrefetchScalarGridSpec(
            num_scalar_prefetch=2,
            grid=(nn, nb),
            in_specs=[
                pl.BlockSpec((tb, ff), amap),
                pl.BlockSpec((None, ff, tn), lambda j, b, be, nv: (be[b], 0, j)),
                pl.BlockSpec((None, 1, tn), lambda j, b, be, nv: (be[b], 0, j)),
            ],
            out_specs=pl.BlockSpec((tb, tn), omap),
            scratch_shapes=[pltpu.VMEM((ff, tn), BF16)],
        ),
        out_shape=jax.ShapeDtypeStruct((n_slots, d), F32),
        compiler_params=_cparams(("arbitrary", "arbitrary")),
        name="ffn_down",
    )(blk_e, nvalid, act, w_down, b_down)


def _combine_kernel(dest_ref, outs_hbm, w_ref, x2_ref, gain_ref, y_ref, buf, sem, *, tc):
    i = pl.program_id(0)
    slot = i % 2

    def gather(step, into):
        base = step * tc * TOP_K

        def start(t, c):
            for k in range(TOP_K):
                pltpu.make_async_copy(outs_hbm.at[pl.ds(dest_ref[base + t * TOP_K + k], 1)],
                                      buf.at[into, k, pl.ds(t, 1)], sem.at[into]).start()
            return c

        lax.fori_loop(0, tc, start, 0, unroll=2)

    @pl.when(i == 0)
    def _():
        gather(0, 0)

    @pl.when(i + 1 < pl.num_programs(0))
    def _():
        gather(i + 1, 1 - slot)

    for k in range(TOP_K):
        pltpu.make_async_copy(outs_hbm.at[pl.ds(0, tc)], buf.at[slot, k], sem.at[slot]).wait()
    w = w_ref[...]
    y = x2_ref[...]
    for k in range(TOP_K):
        y = y + w[:, k:k + 1] * buf[slot, k]
    ms = jnp.mean(y * y, axis=-1, keepdims=True)
    y_ref[...] = y * lax.rsqrt(ms + NORM_EPS) * gain_ref[...]


def _combine(dest, outs, w_top, x2, gain):
    n, d = x2.shape
    tc = 128
    kern = functools.partial(_combine_kernel, tc=tc)
    return pl.pallas_call(
        kern,
        grid_spec=pltpu.PrefetchScalarGridSpec(
            num_scalar_prefetch=1,
            grid=(n // tc,),
            in_specs=[
                pl.BlockSpec(memory_space=pl.ANY),
                pl.BlockSpec((tc, LANES), lambda i, dr: (i, 0)),
                pl.BlockSpec((tc, d), lambda i, dr: (i, 0)),
                pl.BlockSpec((1, d), lambda i, dr: (0, 0)),
            ],
            out_specs=pl.BlockSpec((tc, d), lambda i, dr: (i, 0)),
            scratch_shapes=[pltpu.VMEM((2, TOP_K, tc, d), F32), pltpu.SemaphoreType.DMA((2,))],
        ),
        out_shape=jax.ShapeDtypeStruct((n, d), F32),
        compiler_params=_cparams(("arbitrary",)),
        name="combine",
    )(dest, outs, w_top, x2, gain)


def _rope_tables(seq):
    half = DH // 2
    inv_freq = ROPE_THETA ** (-jnp.arange(half, dtype=F32) / half)
    ang = jnp.arange(seq, dtype=F32)[:, None] * inv_freq[None, :]
    cos = jnp.cos(ang)
    sin = jnp.sin(ang)
    return jnp.concatenate([cos, cos], axis=1), jnp.concatenate([-sin, sin], axis=1)


def _layer(x2d, batch, seq, norm_mix, w_in, cmp_pos_k, cmp_pos_v, w_cmp_k1, w_cmp_k2, w_cmp_v1,
           w_cmp_v2, w_gla_alpha, b_gla_alpha, gla_norm, w_proj_nsa, w_proj_gla, w_merge_gate,
           b_merge_gate, w_out, norm_moe, w_router, b_router, w_gate_up, b_gate_up, w_down, b_down,
           final_gain):
    n, d = x2d.shape
    n_exp = w_router.shape[1]
    d_merge = w_merge_gate.shape[1]
    main0 = d_merge

    o_g = NSA_Q + 6 * NSA_KV
    o_q = o_g + NSA_HEADS * 3
    o_a = o_q + 2 * GLA_QK + 2 * GLA_V
    w_main = jnp.concatenate([w_in[:, :o_g], w_in[:, o_q:o_a]], axis=1)
    w_all = jnp.concatenate([w_merge_gate, w_main], axis=1).astype(BF16)
    n_small = NSA_HEADS * 3 + GLA_RANK
    w_small = jnp.concatenate([w_in[:, o_g:o_q], w_in[:, o_a:],
                               jnp.zeros((d, LANES - n_small), F32)], axis=1).astype(BF16)
    bias_all = jnp.concatenate([b_merge_gate, jnp.zeros((MAIN_W,), F32)])[None, :]
    cos2, sin2 = _rope_tables(seq)

    big, small = _proj(x2d, norm_mix[None, :], w_all, w_small, bias_all, cos2, sin2, seq, d_merge)

    pos = jnp.stack([cmp_pos_k, cmp_pos_v])
    w1 = jnp.stack([w_cmp_k1, w_cmp_v1]).astype(BF16)
    w2 = jnp.stack([w_cmp_k2, w_cmp_v2]).astype(BF16)
    cmp_kv = _compress(big, pos, w1, w2, batch, seq, (main0 + NSA_Q) // DH)
    ncp = cmp_kv.shape[3]
    n_blk = seq // SEL_LEN
    cstart = jnp.arange(ncp)[:, None] * CMP_STRIDE
    blk = jnp.arange(LANES)[None, :]
    cover = ((cstart < (blk + 1) * SEL_LEN) & (cstart + CMP_LEN > blk * SEL_LEN)
             & (blk < n_blk) & (jnp.arange(ncp)[:, None] < seq // CMP_STRIDE - 1)).astype(F32)
    expand = (jnp.arange(LANES)[:, None] == (jnp.arange(seq)[None, :] // SEL_LEN)).astype(BF16)
    o_nsa = _nsa(big, cmp_kv, small, cos2, sin2, cover, expand, batch, seq, main0)

    tri_c = jnp.tril(jnp.ones((GLA_CHUNK, GLA_CHUNK), F32))
    a_col = NSA_HEADS * 3
    w_alpha = jnp.zeros((LANES, GLA_QK), F32).at[a_col:a_col + GLA_RANK].set(w_gla_alpha)
    o_gla = _gla(big, small, w_alpha, b_gla_alpha[None, :], gla_norm[None, :], tri_c,
                 batch, seq, main0)

    wr = jnp.concatenate([w_router, jnp.zeros((d, LANES - n_exp), F32)], axis=1)
    wr_hi = wr.astype(BF16)
    wr_lo = (wr - wr_hi.astype(F32)).astype(BF16)
    br = jnp.concatenate([b_router, jnp.zeros((LANES - n_exp,), F32)])[None, :]
    x2, h2, logits = _mix(o_nsa, o_gla, big, x2d, w_proj_nsa.astype(BF16), w_proj_gla.astype(BF16),
                          w_out.astype(BF16), norm_moe[None, :], wr_hi, wr_lo, br)

    tr = 256
    tri_r = (jnp.arange(tr)[:, None] > jnp.arange(tr)[None, :]).astype(BF16)
    ridx, w_top, counts = _route(logits, tri_r, n_exp)

    tb = 256
    nk = n * TOP_K
    n_blocks = -(-nk // tb) + n_exp
    n_slots = n_blocks * tb
    cnt = counts[0, :n_exp].astype(jnp.int32)
    padded = (cnt + tb - 1) // tb * tb
    pad_end = jnp.cumsum(padded)
    pad_start = pad_end - padded
    top_e = ridx[:, :TOP_K]
    dest = (pad_start[top_e] + ridx[:, TOP_K:2 * TOP_K]).reshape(nk)
    blk_first = jnp.arange(n_blocks, dtype=jnp.int32) * tb
    blk_e = jnp.minimum(jnp.sum((pad_end[None, :] <= blk_first[:, None]).astype(jnp.int32), axis=1),
                        n_exp - 1)
    nvalid = (pad_end[-1:] // tb).astype(jnp.int32)

    xs = _dispatch(dest, pad_start + cnt, pad_end, h2, n_slots, n_exp, tb)
    act = _ffn_up(blk_e, nvalid, xs, w_gate_up, b_gate_up[:, None, :], tb)
    outs = _ffn_down(blk_e, nvalid, act, w_down, b_down[:, None, :], tb)
    return _combine(dest, outs, w_top, x2, final_gain)


def kernel(x, norm_mix, w_in, cmp_pos_k, cmp_pos_v, w_cmp_k1, w_cmp_k2, w_cmp_v1, w_cmp_v2,
           w_gla_alpha, b_gla_alpha, gla_norm, w_proj_nsa, w_proj_gla, w_merge_gate, b_merge_gate,
           w_out, norm_moe, w_router, b_router, w_gate_up, b_gate_up, w_down, b_down, norm_final):
    batch, seq, d = x.shape
    depth = w_in.shape[0]
    assert depth == 1, "the final norm is fused into the (single) layer's combine stage"
    y = _layer(x.reshape(batch * seq, d), batch, seq, norm_mix[0], w_in[0], cmp_pos_k[0],
               cmp_pos_v[0], w_cmp_k1[0], w_cmp_k2[0], w_cmp_v1[0], w_cmp_v2[0], w_gla_alpha[0],
               b_gla_alpha[0], gla_norm[0], w_proj_nsa[0], w_proj_gla[0], w_merge_gate[0],
               b_merge_gate[0], w_out[0], norm_moe[0], w_router[0], b_router[0], w_gate_up[0],
               b_gate_up[0], w_down[0], b_down[0], norm_final[None, :])
    return y.reshape(batch, seq, d)
```

```python
import functools

import jax
import jax.numpy as jnp
from jax import lax
from jax.experimental import pallas as pl
from jax.experimental.pallas import tpu as pltpu

F32 = jnp.float32
BF16 = jnp.bfloat16

NORM_EPS = 1e-5
ROPE_THETA = 10000.0
NEG_INF = -1e30

NSA_HEADS = 8
NSA_GROUPS = 2
NSA_HPG = NSA_HEADS // NSA_GROUPS
DH = 128
CMP_LEN = 32
CMP_STRIDE = 16
SEL_LEN = 64
SEL_TOPK = 16
SEL_FORCE = 1e3
SEL_MASK = 2.0 ** 100
WINDOW = 512

GLA_HEADS = 4
GLA_DK = 128
GLA_DV = 256
GLA_RANK = 16
GLA_TAU = 16.0
GLA_CHUNK = 64

TOP_K = 4
SWIGLU_LIMIT = 7.0
SWIGLU_ALPHA = 1.702

LANES = 128
VMEM_LIMIT = 56 * 1024 * 1024

NSA_Q = NSA_HEADS * DH
NSA_KV = NSA_GROUPS * DH
GLA_QK = GLA_HEADS * GLA_DK
GLA_V = GLA_HEADS * GLA_DV
MAIN_W = NSA_Q + 6 * NSA_KV + 2 * GLA_QK + 2 * GLA_V
SMALL_W = (NSA_GROUPS + 1) * LANES


def _cparams(sem, vmem=VMEM_LIMIT):
    return pltpu.CompilerParams(dimension_semantics=sem, vmem_limit_bytes=vmem)


def _rope(x, cos, sin_signed):
    return x * cos + pltpu.roll(x, DH // 2, axis=1) * sin_signed


def _proj_kernel(x_ref, gain_ref, w_ref, wsmall_ref, bias_ref, cos_ref, sin_ref,
                 big_ref, small_ref, h_scr, *, n_merge_tiles, rope_tiles):
    j = pl.program_id(1)

    @pl.when(j == 0)
    def _():
        x = x_ref[...]
        ms = jnp.mean(x * x, axis=-1, keepdims=True)
        hb = (x * lax.rsqrt(ms + NORM_EPS) * gain_ref[...]).astype(BF16)
        h_scr[...] = hb
        small_ref[...] = jnp.dot(hb, wsmall_ref[...], preferred_element_type=F32)

    acc = jnp.dot(h_scr[...], w_ref[...], preferred_element_type=F32)
    is_merge = j < n_merge_tiles
    is_rope = (j == rope_tiles[0]) | (j == rope_tiles[1])

    @pl.when(is_merge)
    def _():
        big_ref[...] = jax.nn.sigmoid(acc + bias_ref[...]).astype(BF16)

    @pl.when(is_rope)
    def _():
        cos = cos_ref[...]
        sin = sin_ref[...]
        for g in range(NSA_GROUPS):
            sl = slice(g * DH, (g + 1) * DH)
            big_ref[:, sl] = _rope(acc[:, sl], cos, sin).astype(BF16)
        big_ref[:, NSA_KV:] = acc[:, NSA_KV:].astype(BF16)

    @pl.when(jnp.logical_not(is_merge | is_rope))
    def _():
        big_ref[...] = acc.astype(BF16)


def _proj(x2d, gain, w_all, w_small, bias_all, cos2, sin2, seq, d_merge):
    n, d = x2d.shape
    width = w_all.shape[1]
    tm = min(1024, seq)
    tn = 512
    n_merge_tiles = d_merge // tn
    rope_tiles = (n_merge_tiles + 3, n_merge_tiles + 4)
    nsb = seq // tm
    kern = functools.partial(_proj_kernel, n_merge_tiles=n_merge_tiles, rope_tiles=rope_tiles)
    return pl.pallas_call(
        kern,
        grid=(n // tm, width // tn),
        in_specs=[
            pl.BlockSpec((tm, d), lambda i, j: (i, 0)),
            pl.BlockSpec((1, d), lambda i, j: (0, 0)),
            pl.BlockSpec((d, tn), lambda i, j: (0, j)),
            pl.BlockSpec((d, SMALL_W), lambda i, j: (0, 0)),
            pl.BlockSpec((1, tn), lambda i, j: (0, j)),
            pl.BlockSpec((tm, DH), lambda i, j: (i % nsb, 0)),
            pl.BlockSpec((tm, DH), lambda i, j: (i % nsb, 0)),
        ],
        out_specs=[
            pl.BlockSpec((tm, tn), lambda i, j: (i, j)),
            pl.BlockSpec((tm, SMALL_W), lambda i, j: (i, 0)),
        ],
        out_shape=[
            jax.ShapeDtypeStruct((n, width), BF16),
            jax.ShapeDtypeStruct((n, SMALL_W), F32),
        ],
        scratch_shapes=[pltpu.VMEM((tm, d), BF16)],
        compiler_params=_cparams(("parallel", "arbitrary")),
        name="proj",
    )(x2d, gain, w_all, w_small, bias_all, cos2, sin2)


def _compress_kernel(kv_ref, pos_ref, w1_ref, w2_ref, out_ref, scr, *, seq, ncp):
    nreal = seq // CMP_STRIDE
    scr[0:seq, :] = kv_ref[...].astype(F32)
    scr[seq:seq + CMP_LEN, :] = jnp.zeros((CMP_LEN, DH), F32)
    acc = jnp.zeros((nreal, w1_ref.shape[1]), F32)
    for l in range(CMP_LEN):
        a = scr[pl.ds(l, nreal, stride=CMP_STRIDE), :] + pos_ref[l:l + 1, :]
        acc = acc + jnp.dot(a.astype(BF16), w1_ref[l * DH:(l + 1) * DH, :],
                            preferred_element_type=F32)
    hid = jax.nn.gelu(acc)
    out = jnp.dot(hid.astype(BF16), w2_ref[...], preferred_element_type=F32)
    row = lax.broadcasted_iota(jnp.int32, out.shape, 0)
    out = jnp.where(row < nreal - 1, out, 0.0).astype(BF16)
    if ncp > nreal:
        out = jnp.concatenate([out, jnp.zeros((ncp - nreal, DH), BF16)], axis=0)
    out_ref[...] = out


def _compress(big, pos, w1, w2, batch, seq, col0):
    ncp = max(seq // CMP_STRIDE, LANES)
    kern = functools.partial(_compress_kernel, seq=seq, ncp=ncp)
    hid = w1.shape[2]
    return pl.pallas_call(
        kern,
        grid=(batch, NSA_GROUPS, 2),
        in_specs=[
            pl.BlockSpec((seq, DH), lambda b, g, t: (b, col0 + 2 * t + g)),
            pl.BlockSpec((None, CMP_LEN, DH), lambda b, g, t: (t, 0, 0)),
            pl.BlockSpec((None, CMP_LEN * DH, hid), lambda b, g, t: (t, 0, 0)),
            pl.BlockSpec((None, hid, DH), lambda b, g, t: (t, 0, 0)),
        ],
        out_specs=pl.BlockSpec((None, None, None, ncp, DH), lambda b, g, t: (b, g, t, 0, 0)),
        out_shape=jax.ShapeDtypeStruct((batch, NSA_GROUPS, 2, ncp, DH), BF16),
        scratch_shapes=[pltpu.VMEM((seq + CMP_LEN, DH), F32)],
        compiler_params=_cparams(("parallel", "parallel", "arbitrary")),
        name="compress",
    )(big, pos, w1, w2)


def _stack_heads(t):
    return jnp.concatenate([t[:, h * DH:(h + 1) * DH] for h in range(NSA_HPG)], axis=0)


def _nsa_kernel(q_ref, kc_ref, vc_ref, ks_ref, vs_ref, kw_ref, vw_ref, cos_ref, sin_ref,
                gate_ref, cover_ref, et_ref, wbias_ref, o_ref, kext_scr, vsext_scr, vwext_scr,
                *, seq, tq, ck):
    i = pl.program_id(2)
    q0 = i * tq
    scale = DH ** -0.5
    rows = NSA_HPG * tq
    ncp = kc_ref.shape[0]
    n_blk = seq // SEL_LEN
    n_sel = min(SEL_TOPK, n_blk)
    wspan = tq + WINDOW
    nt = (((1,), (1,)), ((), ()))

    @pl.when(i == 0)
    def _():
        ones = jnp.ones((seq, DH), BF16)
        kext_scr[:, :DH] = ks_ref[...]
        kext_scr[:, DH:] = et_ref[...]
        vsext_scr[:, :DH] = vs_ref[...]
        vsext_scr[:, DH:] = ones
        vwext_scr[:, :DH] = vw_ref[...]
        vwext_scr[:, DH:] = ones

    q = q_ref[...]
    qs = _stack_heads(q)
    pos_q = q0 + lax.broadcasted_iota(jnp.int32, (tq, 1), 0)
    pos_rows = jnp.concatenate([pos_q] * NSA_HPG, axis=0)

    s = lax.dot_general(qs, kc_ref[...], nt, preferred_element_type=F32) * scale
    n_idx = lax.broadcasted_iota(jnp.int32, (1, ncp), 1)
    cmask = (n_idx * CMP_STRIDE + (CMP_LEN - 1)) <= pos_rows
    s = jnp.where(cmask, s, NEG_INF)
    m = jnp.max(s, axis=-1, keepdims=True)
    e = jnp.exp(s - m)
    p = jnp.where(cmask, e / jnp.sum(e, axis=-1, keepdims=True), 0.0)
    o_cmp = jnp.dot(p.astype(BF16), vc_ref[...], preferred_element_type=F32)

    psum = p[0:tq]
    for h in range(1, NSA_HPG):
        psum = psum + p[h * tq:(h + 1) * tq]
    imp = jnp.dot(psum, cover_ref[...], preferred_element_type=F32,
                  precision=lax.Precision.HIGHEST)
    blk = lax.broadcasted_iota(jnp.int32, (tq, LANES), 1)
    t_blk = pos_q // SEL_LEN
    forced = (blk == 0) | (blk == t_blk) | (blk == t_blk - 1)
    bonus = jnp.where(blk > t_blk, -SEL_FORCE, jnp.where(forced, SEL_FORCE, 0.0))
    val_t = (imp + bonus).T[:n_blk]
    blk_t = lax.broadcasted_iota(jnp.int32, (n_blk, tq), 0)
    rank = jnp.zeros((n_blk, tq), F32)
    for c in range(n_blk):
        vc = val_t[c:c + 1, :]
        beats = (vc > val_t) | ((vc == val_t) & (blk_t > c))
        rank = rank + jnp.where(beats, 1.0, 0.0)
    pen_t = jnp.where(rank < n_sel, 0.0, -SEL_MASK)
    pen_t = jnp.concatenate([pen_t, jnp.zeros((LANES - n_blk, tq), F32)], axis=0)
    pen = pen_t.T.astype(BF16)

    cos = cos_ref[...]
    sin = sin_ref[...]
    qr = jnp.concatenate(
        [(_rope(q[:, h * DH:(h + 1) * DH].astype(F32), cos, sin) * scale).astype(BF16)
         for h in range(NSA_HPG)], axis=0)
    q_ext = jnp.concatenate([qr, jnp.concatenate([pen] * NSA_HPG, axis=0)], axis=1)

    n_chunks = (q0 + tq + ck - 1) // ck

    q_heads = [q_ext[h * tq:(h + 1) * tq] for h in range(NSA_HPG)]

    def sel_chunk(c, carry, causal):
        k0 = pl.multiple_of(c * ck, ck)
        kblk = kext_scr[pl.ds(k0, ck), :]
        vblk = vsext_scr[pl.ds(k0, ck), :]
        scs = [lax.dot_general(q_heads[h], kblk, nt, preferred_element_type=F32)
               for h in range(NSA_HPG)]
        out = []
        for h in range(NSA_HPG):
            m_i, acc = carry[h]
            sc = scs[h]
            if causal:
                kp = k0 + lax.broadcasted_iota(jnp.int32, (1, ck), 1)
                sc = jnp.where(kp <= pos_q, sc, NEG_INF)
            m_new = jnp.maximum(m_i, jnp.max(sc, axis=-1, keepdims=True))
            alpha = jnp.exp(m_i - m_new)
            pc = jnp.exp(sc - m_new)
            acc = alpha * acc + jnp.dot(pc.astype(BF16), vblk, preferred_element_type=F32)
            out.append((m_new, acc))
        return tuple(out)

    carry = tuple((jnp.full((tq, 1), NEG_INF, F32), jnp.zeros((tq, 2 * DH), F32))
                  for _ in range(NSA_HPG))
    carry = lax.fori_loop(0, n_chunks - 1, lambda c, cr: sel_chunk(c, cr, False), carry)
    carry = sel_chunk(n_chunks - 1, carry, True)

    kstart = pl.multiple_of(jnp.maximum(q0 - WINDOW, 0), tq)
    kwin = kw_ref[pl.ds(kstart, wspan), :]
    vwin = vwext_scr[pl.ds(kstart, wspan), :]
    wbias = wbias_ref[...]

    gates = jax.nn.sigmoid(gate_ref[...])
    sws = [lax.dot_general(qr[h * tq:(h + 1) * tq], kwin, nt, preferred_element_type=F32)
           for h in range(NSA_HPG)]
    for h in range(NSA_HPG):
        acc_s = carry[h][1]
        o_sel = acc_s[:, :DH] / acc_s[:, DH:]
        sw = sws[h] + wbias
        ew = jnp.exp(sw - jnp.max(sw, axis=-1, keepdims=True))
        acc_w = jnp.dot(ew.astype(BF16), vwin, preferred_element_type=F32)
        o_win = acc_w[:, :DH] / acc_w[:, DH:]
        c0 = 3 * h
        o_h = (gates[:, c0:c0 + 1] * o_cmp[h * tq:(h + 1) * tq] + gates[:, c0 + 1:c0 + 2] * o_sel
               + gates[:, c0 + 2:c0 + 3] * o_win)
        o_ref[:, h * DH:(h + 1) * DH] = o_h.astype(BF16)


def _nsa(big, cmp_kv, small, cos2, sin2, cover, et, batch, seq, main0):
    tq = 128
    wbias = _window_bias(tq)
    ck = min(512, seq)
    nq = seq // tq
    ncp = cmp_kv.shape[3]
    wspan = tq + WINDOW
    n_wb = wbias.shape[0]
    kern = functools.partial(_nsa_kernel, seq=seq, tq=tq, ck=ck)
    kv_col = (main0 + NSA_Q) // DH

    def kvspec(which):
        return pl.BlockSpec((seq, DH), lambda b, g, i: (b, kv_col + 2 * which + g))

    return pl.pallas_call(
        kern,
        grid=(batch, NSA_GROUPS, nq),
        in_specs=[
            pl.BlockSpec((tq, NSA_HPG * DH), lambda b, g, i: (b * nq + i, main0 // (NSA_HPG * DH) + g)),
            pl.BlockSpec((None, None, None, ncp, DH), lambda b, g, i: (b, g, 0, 0, 0)),
            pl.BlockSpec((None, None, None, ncp, DH), lambda b, g, i: (b, g, 1, 0, 0)),
            kvspec(2), kvspec(3), kvspec(4), kvspec(5),
            pl.BlockSpec((tq, DH), lambda b, g, i: (i, 0)),
            pl.BlockSpec((tq, DH), lambda b, g, i: (i, 0)),
            pl.BlockSpec((tq, LANES), lambda b, g, i: (b * nq + i, g)),
            pl.BlockSpec((ncp, LANES), lambda b, g, i: (0, 0)),
            pl.BlockSpec((seq, LANES), lambda b, g, i: (0, 0)),
            pl.BlockSpec((None, tq, wspan), lambda b, g, i: (jnp.minimum(i, n_wb - 1), 0, 0)),
        ],
        out_specs=pl.BlockSpec((tq, NSA_HPG * DH), lambda b, g, i: (b * nq + i, g)),
        out_shape=jax.ShapeDtypeStruct((batch * seq, NSA_Q), BF16),
        scratch_shapes=[pltpu.VMEM((seq, 2 * DH), BF16), pltpu.VMEM((seq, 2 * DH), BF16),
                        pltpu.VMEM((seq, 2 * DH), BF16)],
        compiler_params=_cparams(("parallel", "parallel", "arbitrary")),
        name="nsa",
    )(big, cmp_kv, cmp_kv, big, big, big, big, cos2, sin2, small, cover, et, wbias)


def _gla_kernel(q_ref, k_ref, v_ref, r_ref, small_ref, wa_ref, ba_ref, gain_ref, tri_ref,
                o_ref, la_scr, st_scr, *, seq):
    C = GLA_CHUNK
    nc = seq // C
    z = jnp.dot(small_ref[...], wa_ref[...], preferred_element_type=F32,
                precision=lax.Precision.HIGHEST) + ba_ref[...]
    la_scr[...] = jax.nn.log_sigmoid(z) / GLA_TAU
    st_scr[...] = jnp.zeros_like(st_scr)
    ri = lax.broadcasted_iota(jnp.int32, (C, C), 0)
    ci = lax.broadcasted_iota(jnp.int32, (C, C), 1)
    causal = ri >= ci
    qscale = GLA_DK ** -0.5

    def body(c, carry):
        r0 = pl.multiple_of(c * C, C)
        la = la_scr[pl.ds(r0, C), :]
        cum = jnp.dot(tri_ref[...], la, preferred_element_type=F32,
                      precision=lax.Precision.HIGHEST)
        cum_last = cum[C - 1:C, :]
        qc = q_ref[pl.ds(r0, C), :].astype(F32) * qscale
        kc = k_ref[pl.ds(r0, C), :].astype(F32)
        vc = v_ref[pl.ds(r0, C), :]
        q_t = (qc * jnp.exp(cum)).astype(BF16)
        k_t = (kc * jnp.exp(-cum)).astype(BF16)
        attn = lax.dot_general(q_t, k_t, (((1,), (1,)), ((), ())), preferred_element_type=F32)
        attn = jnp.where(causal, attn, 0.0)
        o = jnp.dot(attn.astype(BF16), vc, preferred_element_type=F32)
        st = st_scr[...]
        o = o + lax.dot_general(q_t, st.astype(BF16), (((1,), (1,)), ((), ())),
                                preferred_element_type=F32)
        k_state = (kc * jnp.exp(cum_last - cum)).astype(BF16)
        d_st = lax.dot_general(vc, k_state, (((0,), (0,)), ((), ())),
                               preferred_element_type=F32)
        st_scr[...] = st * jnp.exp(cum_last) + d_st
        o = o * lax.rsqrt(jnp.mean(o * o, axis=-1, keepdims=True) + NORM_EPS)
        o = o * gain_ref[...]
        rr = r_ref[pl.ds(r0, C), :].astype(F32)
        o_ref[pl.ds(r0, C), :] = (o * (rr * jax.nn.sigmoid(rr))).astype(BF16)
        return carry

    lax.fori_loop(0, nc, body, 0, unroll=4)


def _gla(big, small, w_alpha, b_alpha, gain, tri, batch, seq, main0):
    qcol = (main0 + NSA_Q + 6 * NSA_KV) // GLA_DK
    kcol = qcol + GLA_QK // GLA_DK
    vcol = (main0 + NSA_Q + 6 * NSA_KV + 2 * GLA_QK) // GLA_DV
    rcol = vcol + GLA_V // GLA_DV
    kern = functools.partial(_gla_kernel, seq=seq)
    return pl.pallas_call(
        kern,
        grid=(batch, GLA_HEADS),
        in_specs=[
            pl.BlockSpec((seq, GLA_DK), lambda b, h: (b, qcol + h)),
            pl.BlockSpec((seq, GLA_DK), lambda b, h: (b, kcol + h)),
            pl.BlockSpec((seq, GLA_DV), lambda b, h: (b, vcol + h)),
            pl.BlockSpec((seq, GLA_DV), lambda b, h: (b, rcol + h)),
            pl.BlockSpec((seq, LANES), lambda b, h: (b, NSA_GROUPS)),
            pl.BlockSpec((LANES, GLA_DK), lambda b, h: (0, h)),
            pl.BlockSpec((1, GLA_DK), lambda b, h: (0, h)),
            pl.BlockSpec((1, GLA_DV), lambda b, h: (0, h)),
            pl.BlockSpec((GLA_CHUNK, GLA_CHUNK), lambda b, h: (0, 0)),
        ],
        out_specs=pl.BlockSpec((seq, GLA_DV), lambda b, h: (b, h)),
        out_shape=jax.ShapeDtypeStruct((batch * seq, GLA_V), BF16),
        scratch_shapes=[pltpu.VMEM((seq, GLA_DK), F32), pltpu.VMEM((GLA_DV, GLA_DK), F32)],
        compiler_params=_cparams(("parallel", "parallel")),
        name="gla",
    )(big, big, big, big, small, w_alpha, b_alpha, gain, tri)


def _mix_kernel(on_ref, og_ref, ma_ref, mb_ref, x_ref, wpn_ref, wpg_ref, wo_ref, gain_ref,
                wrh_ref, wrl_ref, br_ref, x2_ref, h2_ref, lg_ref):
    a = jnp.dot(on_ref[...], wpn_ref[...], preferred_element_type=F32)
    b = jnp.dot(og_ref[...], wpg_ref[...], preferred_element_type=F32)
    mixed = ma_ref[...].astype(F32) * a + mb_ref[...].astype(F32) * b
    x2 = x_ref[...] + jnp.dot(mixed.astype(BF16), wo_ref[...], preferred_element_type=F32)
    x2_ref[...] = x2
    ms = jnp.mean(x2 * x2, axis=-1, keepdims=True)
    h2 = x2 * lax.rsqrt(ms + NORM_EPS) * gain_ref[...]
    h2_ref[...] = h2
    hi = h2.astype(BF16)
    lo = (h2 - hi.astype(F32)).astype(BF16)
    lg = (jnp.dot(hi, wrh_ref[...], preferred_element_type=F32)
          + jnp.dot(lo, wrh_ref[...], preferred_element_type=F32)
          + jnp.dot(hi, wrl_ref[...], preferred_element_type=F32))
    lg_ref[...] = lg + br_ref[...]


def _mix(o_nsa, o_gla, big, x2d, wpn, wpg, wo, gain, wr_hi, wr_lo, br):
    n, d = x2d.shape
    tm = 256
    const = lambda i: (0, 0)
    return pl.pallas_call(
        _mix_kernel,
        grid=(n // tm,),
        in_specs=[
            pl.BlockSpec((tm, NSA_Q), lambda i: (i, 0)),
            pl.BlockSpec((tm, GLA_V), lambda i: (i, 0)),
            pl.BlockSpec((tm, d), lambda i: (i, 0)),
            pl.BlockSpec((tm, d), lambda i: (i, 1)),
            pl.BlockSpec((tm, d), lambda i: (i, 0)),
            pl.BlockSpec((NSA_Q, d), const, pipeline_mode=pl.Buffered(1)),
            pl.BlockSpec((GLA_V, d), const, pipeline_mode=pl.Buffered(1)),
            pl.BlockSpec((d, d), const, pipeline_mode=pl.Buffered(1)),
            pl.BlockSpec((1, d), const),
            pl.BlockSpec((d, LANES), const),
            pl.BlockSpec((d, LANES), const),
            pl.BlockSpec((1, LANES), const),
        ],
        out_specs=[
            pl.BlockSpec((tm, d), lambda i: (i, 0)),
            pl.BlockSpec((tm, d), lambda i: (i, 0)),
            pl.BlockSpec((tm, LANES), lambda i: (i, 0)),
        ],
        out_shape=[
            jax.ShapeDtypeStruct((n, d), F32),
            jax.ShapeDtypeStruct((n, d), F32),
            jax.ShapeDtypeStruct((n, LANES), F32),
        ],
        compiler_params=_cparams(("parallel",)),
        name="mix",
    )(o_nsa, o_gla, big, big, x2d, wpn, wpg, wo, gain, wr_hi, wr_lo, br)


def _route_kernel(lg_ref, tri_ref, idx_ref, w_ref, cnt_ref, carry_scr, *, n_exp):
    i = pl.program_id(0)
    tr = lg_ref.shape[0]

    @pl.when(i == 0)
    def _():
        carry_scr[...] = jnp.zeros_like(carry_scr)

    lane = lax.broadcasted_iota(jnp.int32, (tr, LANES), 1)
    work = jnp.where(lane < n_exp, lg_ref[...], -3e38)
    onehots, vals, idxs = [], [], []
    for _ in range(TOP_K):
        mval = jnp.max(work, axis=-1, keepdims=True)
        idx = jnp.min(jnp.where(work == mval, lane, LANES), axis=-1, keepdims=True)
        oh = lane == idx
        work = jnp.where(oh, -3e38, work)
        onehots.append(oh)
        vals.append(mval)
        idxs.append(idx)
    exps = [jnp.exp(v - vals[0]) for v in vals]
    den = exps[0] + exps[1] + exps[2] + exps[3]
    onehot = jnp.zeros((tr, LANES), F32)
    for oh in onehots:
        onehot = onehot + jnp.where(oh, 1.0, 0.0)
    before = jnp.dot(tri_ref[...], onehot.astype(BF16), preferred_element_type=F32) + carry_scr[...]
    carry_scr[...] = carry_scr[...] + jnp.sum(onehot, axis=0, keepdims=True)
    idx_out = jnp.zeros((tr, LANES), jnp.int32)
    w_out = jnp.zeros((tr, LANES), F32)
    for k in range(TOP_K):
        rank_k = jnp.sum(jnp.where(onehots[k], before, 0.0), axis=-1, keepdims=True)
        idx_out = jnp.where(lane == k, idxs[k], idx_out)
        idx_out = jnp.where(lane == TOP_K + k, rank_k.astype(jnp.int32), idx_out)
        w_out = jnp.where(lane == k, exps[k] / den, w_out)
    idx_ref[...] = idx_out
    w_ref[...] = w_out
    cnt_ref[...] = carry_scr[...]


def _route(logits, tri, n_exp):
    n = logits.shape[0]
    tr = tri.shape[0]
    kern = functools.partial(_route_kernel, n_exp=n_exp)
    return pl.pallas_call(
        kern,
        grid=(n // tr,),
        in_specs=[pl.BlockSpec((tr, LANES), lambda i: (i, 0)),
                  pl.BlockSpec((tr, tr), lambda i: (0, 0))],
        out_specs=[pl.BlockSpec((tr, LANES), lambda i: (i, 0)),
                   pl.BlockSpec((tr, LANES), lambda i: (i, 0)),
                   pl.BlockSpec((1, LANES), lambda i: (0, 0))],
        out_shape=[jax.ShapeDtypeStruct((n, LANES), jnp.int32),
                   jax.ShapeDtypeStruct((n, LANES), F32),
                   jax.ShapeDtypeStruct((1, LANES), F32)],
        scratch_shapes=[pltpu.VMEM((1, LANES), F32)],
        compiler_params=_cparams(("arbitrary",)),
        name="route",
    )(logits, tri)


def _dispatch_kernel(dest_ref, padlo_ref, padhi_ref, h_ref, xs_hbm, zbuf, sem, zsem, *, td, tb, n_exp):
    i = pl.program_id(0)
    n_blocks = xs_hbm.shape[0] // tb

    @pl.when(i == 0)
    def _():
        zbuf[...] = jnp.zeros_like(zbuf)

        def row_copy(s):
            return pltpu.make_async_copy(zbuf.at[pl.ds(0, 1)], xs_hbm.at[pl.ds(s, 1)], zsem)

        def per_expert(e, c):
            lo = padlo_ref[e]
            hi = padhi_ref[e]
            lax.fori_loop(lo, hi, lambda s, c2: (row_copy(s).start(), c2)[1], 0)
            lax.fori_loop(lo, hi, lambda s, c2: (row_copy(s).wait(), c2)[1], 0)
            return c

        lax.fori_loop(0, n_exp, per_expert, 0)

        def blk_copy(b):
            return pltpu.make_async_copy(zbuf, xs_hbm.at[pl.ds(b * tb, tb)], zsem)

        first_free = padhi_ref[n_exp - 1] // tb
        lax.fori_loop(first_free, n_blocks, lambda b, c: (blk_copy(b).start(), c)[1], 0)
        lax.fori_loop(first_free, n_blocks, lambda b, c: (blk_copy(b).wait(), c)[1], 0)

    base = i * td * TOP_K

    def start(t, c):
        for k in range(TOP_K):
            pltpu.make_async_copy(h_ref.at[pl.ds(t, 1)],
                                  xs_hbm.at[pl.ds(dest_ref[base + t * TOP_K + k], 1)], sem).start()
        return c

    lax.fori_loop(0, td, start, 0, unroll=2)
    for k in range(TOP_K):
        pltpu.make_async_copy(h_ref, xs_hbm.at[pl.ds(0, td)], sem).wait()


def _dispatch(dest, pad_lo, pad_hi, h2, n_slots, n_exp, tb):
    n, d = h2.shape
    td = 256
    kern = functools.partial(_dispatch_kernel, td=td, tb=tb, n_exp=n_exp)
    return pl.pallas_call(
        kern,
        grid_spec=pltpu.PrefetchScalarGridSpec(
            num_scalar_prefetch=3,
            grid=(n // td,),
            in_specs=[pl.BlockSpec((td, d), lambda i, dr, lo, hi: (i, 0))],
            out_specs=pl.BlockSpec(memory_space=pl.ANY),
            scratch_shapes=[pltpu.VMEM((tb, d), F32), pltpu.SemaphoreType.DMA, pltpu.SemaphoreType.DMA],
        ),
        out_shape=jax.ShapeDtypeStruct((n_slots, d), F32),
        compiler_params=_cparams(("arbitrary",)),
        name="dispatch",
    )(dest, pad_lo, pad_hi, h2)


def _expert_changed(blk_e_ref, b):
    return (b == 0) | (blk_e_ref[b] != blk_e_ref[jnp.maximum(b - 1, 0)])


def _stream_expert_weights(blk_e_ref, nxt_ref, nvalid_ref, copies, cast):
    j = pl.program_id(0)
    b = pl.program_id(1)
    nvalid = nvalid_ref[0]

    @pl.when((j == 0) & (b == 0))
    def _():
        for c in copies(blk_e_ref[0], 0):
            c.start()

    @pl.when((b < nvalid) & _expert_changed(blk_e_ref, b))
    def _():
        for c in copies(blk_e_ref[b], j):
            c.wait()
        cast()
        nxt = nxt_ref[b]
        more = nxt < nvalid

        @pl.when(more)
        def _():
            for c in copies(blk_e_ref[jnp.minimum(nxt, nvalid - 1)], j):
                c.start()

        @pl.when(jnp.logical_not(more) & (j + 1 < pl.num_programs(0)))
        def _():
            for c in copies(blk_e_ref[0], j + 1):
                c.start()


def _ffn_up_kernel(blk_e_ref, nxt_ref, nvalid_ref, xs_ref, w_hbm, bg_ref, bu_ref, act_ref,
                   wf32, wbf, sem, *, tf, ff):
    b = pl.program_id(1)
    valid = b < nvalid_ref[0]

    def copies(e, j):
        return [pltpu.make_async_copy(w_hbm.at[e, :, pl.ds(pl.multiple_of(h * ff + j * tf, tf), tf)],
                                      wf32.at[h], sem.at[h]) for h in range(2)]

    def cast():
        wbf[...] = wf32[...].astype(BF16)

    _stream_expert_weights(blk_e_ref, nxt_ref, nvalid_ref, copies, cast)

    @pl.when(valid)
    def _():
        x = xs_ref[...].astype(BF16)
        gate = jnp.dot(x, wbf[0], preferred_element_type=F32) + bg_ref[...]
        up = jnp.dot(x, wbf[1], preferred_element_type=F32) + bu_ref[...]
        gate = jnp.minimum(gate, SWIGLU_LIMIT)
        up = jnp.clip(up, -SWIGLU_LIMIT, SWIGLU_LIMIT)
        act_ref[...] = ((up + 1.0) * gate * jax.nn.sigmoid(gate * SWIGLU_ALPHA)).astype(BF16)

    @pl.when(jnp.logical_not(valid))
    def _():
        act_ref[...] = jnp.zeros_like(act_ref)


def _ffn_up(blk_e, nxt, nvalid, xs, w_gate_up, b_gate_up, tb):
    n_slots, d = xs.shape
    n_exp, _, f2 = w_gate_up.shape
    ff = f2 // 2
    tf = min(1024, ff)
    nf = ff // tf
    nb = n_slots // tb

    def xmap(j, b, be, nx, nv):
        return (jnp.minimum(b, nv[0] - 1), 0)

    kern = functools.partial(_ffn_up_kernel, tf=tf, ff=ff)
    return pl.pallas_call(
        kern,
        grid_spec=pltpu.PrefetchScalarGridSpec(
            num_scalar_prefetch=3,
            grid=(nf, nb),
            in_specs=[
                pl.BlockSpec((tb, d), xmap),
                pl.BlockSpec(memory_space=pl.ANY),
                pl.BlockSpec((None, 1, tf), lambda j, b, be, nx, nv: (be[b], 0, j)),
                pl.BlockSpec((None, 1, tf), lambda j, b, be, nx, nv: (be[b], 0, nf + j)),
            ],
            out_specs=pl.BlockSpec((tb, tf), lambda j, b, be, nx, nv: (b, j)),
            scratch_shapes=[pltpu.VMEM((2, d, tf), F32), pltpu.VMEM((2, d, tf), BF16),
                            pltpu.SemaphoreType.DMA((2,))],
        ),
        out_shape=jax.ShapeDtypeStruct((n_slots, ff), BF16),
        compiler_params=_cparams(("arbitrary", "arbitrary")),
        name="ffn_up",
    )(blk_e, nxt, nvalid, xs, w_gate_up, b_gate_up, b_gate_up)


def _ffn_down_kernel(blk_e_ref, nxt_ref, nvalid_ref, act_ref, w_hbm, bd_ref, out_ref,
                     wf32, wbf, sem, *, tn):
    b = pl.program_id(1)
    valid = b < nvalid_ref[0]

    def copies(e, j):
        return [pltpu.make_async_copy(w_hbm.at[e, :, pl.ds(pl.multiple_of(j * tn, tn), tn)],
                                      wf32, sem.at[0])]

    def cast():
        wbf[...] = wf32[...].astype(BF16)

    _stream_expert_weights(blk_e_ref, nxt_ref, nvalid_ref, copies, cast)

    @pl.when(valid)
    def _():
        out_ref[...] = jnp.dot(act_ref[...], wbf[...], preferred_element_type=F32) + bd_ref[...]

    @pl.when(jnp.logical_not(valid))
    def _():
        out_ref[...] = jnp.zeros_like(out_ref)


def _ffn_down(blk_e, nxt, nvalid, act, w_down, b_down, tb):
    n_slots, ff = act.shape
    n_exp, _, d = w_down.shape
    tn = min(2048, d)
    nn = d // tn
    nb = n_slots // tb

    def amap(j, b, be, nx, nv):
        return (jnp.minimum(b, nv[0] - 1), 0)

    kern = functools.partial(_ffn_down_kernel, tn=tn)
    return pl.pallas_call(
        kern,
        grid_spec=pltpu.PrefetchScalarGridSpec(
            num_scalar_prefetch=3,
            grid=(nn, nb),
            in_specs=[
                pl.BlockSpec((tb, ff), amap),
                pl.BlockSpec(memory_space=pl.ANY),
                pl.BlockSpec((None, 1, tn), lambda j, b, be, nx, nv: (be[b], 0, j)),
            ],
            out_specs=pl.BlockSpec((tb, tn), lambda j, b, be, nx, nv: (b, j)),
            scratch_shapes=[pltpu.VMEM((ff, tn), F32), pltpu.VMEM((ff, tn), BF16),
                            pltpu.SemaphoreType.DMA((1,))],
        ),
        out_shape=jax.ShapeDtypeStruct((n_slots, d), F32),
        compiler_params=_cparams(("arbitrary", "arbitrary")),
        name="ffn_down",
    )(blk_e, nxt, nvalid, act, w_down, b_down)


def _combine_kernel(dest_ref, outs_hbm, w_ref, x2_ref, gain_ref, y_ref, buf, sem, *, tc):
    i = pl.program_id(0)
    slot = i % 2

    def gather(step, into):
        base = step * tc * TOP_K

        def start(t, c):
            for k in range(TOP_K):
                pltpu.make_async_copy(outs_hbm.at[pl.ds(dest_ref[base + t * TOP_K + k], 1)],
                                      buf.at[into, k, pl.ds(t, 1)], sem.at[into]).start()
            return c

        lax.fori_loop(0, tc, start, 0, unroll=2)

    @pl.when(i == 0)
    def _():
        gather(0, 0)

    @pl.when(i + 1 < pl.num_programs(0))
    def _():
        gather(i + 1, 1 - slot)

    for k in range(TOP_K):
        pltpu.make_async_copy(outs_hbm.at[pl.ds(0, tc)], buf.at[slot, k], sem.at[slot]).wait()
    w = w_ref[...]
    y = x2_ref[...]
    for k in range(TOP_K):
        y = y + w[:, k:k + 1] * buf[slot, k]
    ms = jnp.mean(y * y, axis=-1, keepdims=True)
    y_ref[...] = y * lax.rsqrt(ms + NORM_EPS) * gain_ref[...]


def _combine(dest, outs, w_top, x2, gain):
    n, d = x2.shape
    tc = 128
    kern = functools.partial(_combine_kernel, tc=tc)
    return pl.pallas_call(
        kern,
        grid_spec=pltpu.PrefetchScalarGridSpec(
            num_scalar_prefetch=1,
            grid=(n // tc,),
            in_specs=[
                pl.BlockSpec(memory_space=pl.ANY),
                pl.BlockSpec((tc, LANES), lambda i, dr: (i, 0)),
                pl.BlockSpec((tc, d), lambda i, dr: (i, 0)),
                pl.BlockSpec((1, d), lambda i, dr: (0, 0)),
            ],
            out_specs=pl.BlockSpec((tc, d), lambda i, dr: (i, 0)),
            scratch_shapes=[pltpu.VMEM((2, TOP_K, tc, d), F32), pltpu.SemaphoreType.DMA((2,))],
        ),
        out_shape=jax.ShapeDtypeStruct((n, d), F32),
        compiler_params=_cparams(("arbitrary",)),
        name="combine",
    )(dest, outs, w_top, x2, gain)


def _window_bias(tq):
    nv = WINDOW // tq
    r = jnp.arange(tq)[None, :, None]
    c = jnp.arange(tq + WINDOW)[None, None, :]
    v = jnp.arange(nv + 1)[:, None, None]
    q0 = v * tq
    kstart = jnp.maximum(q0 - WINDOW, 0)
    rel = (kstart + c) - (q0 + r)
    return jnp.where((rel <= 0) & (rel > -WINDOW), 0.0, NEG_INF).astype(F32)


def _rope_tables(seq):
    half = DH // 2
    inv_freq = ROPE_THETA ** (-jnp.arange(half, dtype=F32) / half)
    ang = jnp.arange(seq, dtype=F32)[:, None] * inv_freq[None, :]
    cos = jnp.cos(ang)
    sin = jnp.sin(ang)
    return jnp.concatenate([cos, cos], axis=1), jnp.concatenate([-sin, sin], axis=1)


def _layer(x2d, batch, seq, norm_mix, w_in, cmp_pos_k, cmp_pos_v, w_cmp_k1, w_cmp_k2, w_cmp_v1,
           w_cmp_v2, w_gla_alpha, b_gla_alpha, gla_norm, w_proj_nsa, w_proj_gla, w_merge_gate,
           b_merge_gate, w_out, norm_moe, w_router, b_router, w_gate_up, b_gate_up, w_down, b_down,
           final_gain):
    n, d = x2d.shape
    n_exp = w_router.shape[1]
    d_merge = w_merge_gate.shape[1]
    main0 = d_merge

    o_g = NSA_Q + 6 * NSA_KV
    o_q = o_g + NSA_HEADS * 3
    o_a = o_q + 2 * GLA_QK + 2 * GLA_V
    w_main = jnp.concatenate([w_in[:, :o_g], w_in[:, o_q:o_a]], axis=1)
    w_all = jnp.concatenate([w_merge_gate, w_main], axis=1).astype(BF16)
    n_gl = NSA_HPG * 3
    small_parts = []
    for g in range(NSA_GROUPS):
        small_parts += [w_in[:, o_g + g * n_gl:o_g + (g + 1) * n_gl], jnp.zeros((d, LANES - n_gl), F32)]
    small_parts += [w_in[:, o_a:], jnp.zeros((d, LANES - GLA_RANK), F32)]
    w_small = jnp.concatenate(small_parts, axis=1).astype(BF16)
    bias_all = jnp.concatenate([b_merge_gate, jnp.zeros((MAIN_W,), F32)])[None, :]
    cos2, sin2 = _rope_tables(seq)

    big, small = _proj(x2d, norm_mix[None, :], w_all, w_small, bias_all, cos2, sin2, seq, d_merge)

    pos = jnp.stack([cmp_pos_k, cmp_pos_v])
    w1 = jnp.stack([w_cmp_k1, w_cmp_v1]).astype(BF16)
    w2 = jnp.stack([w_cmp_k2, w_cmp_v2]).astype(BF16)
    cmp_kv = _compress(big, pos, w1, w2, batch, seq, (main0 + NSA_Q) // DH)
    ncp = cmp_kv.shape[3]
    n_blk = seq // SEL_LEN
    cstart = jnp.arange(ncp)[:, None] * CMP_STRIDE
    blk = jnp.arange(LANES)[None, :]
    cover = ((cstart < (blk + 1) * SEL_LEN) & (cstart + CMP_LEN > blk * SEL_LEN)
             & (blk < n_blk) & (jnp.arange(ncp)[:, None] < seq // CMP_STRIDE - 1)).astype(F32)
    et = ((jnp.arange(seq)[:, None] // SEL_LEN) == jnp.arange(LANES)[None, :]).astype(BF16)
    o_nsa = _nsa(big, cmp_kv, small, cos2, sin2, cover, et, batch, seq, main0)

    tri_c = jnp.tril(jnp.ones((GLA_CHUNK, GLA_CHUNK), F32))
    w_alpha = jnp.concatenate([w_gla_alpha, jnp.zeros((LANES - GLA_RANK, GLA_QK), F32)], axis=0)
    o_gla = _gla(big, small, w_alpha, b_gla_alpha[None, :], gla_norm[None, :], tri_c,
                 batch, seq, main0)

    wr = jnp.concatenate([w_router, jnp.zeros((d, LANES - n_exp), F32)], axis=1)
    wr_hi = wr.astype(BF16)
    wr_lo = (wr - wr_hi.astype(F32)).astype(BF16)
    br = jnp.concatenate([b_router, jnp.zeros((LANES - n_exp,), F32)])[None, :]
    x2, h2, logits = _mix(o_nsa, o_gla, big, x2d, w_proj_nsa.astype(BF16), w_proj_gla.astype(BF16),
                          w_out.astype(BF16), norm_moe[None, :], wr_hi, wr_lo, br)

    tr = 256
    tri_r = (jnp.arange(tr)[:, None] > jnp.arange(tr)[None, :]).astype(BF16)
    ridx, w_top, counts = _route(logits, tri_r, n_exp)

    tb = 256
    nk = n * TOP_K
    n_blocks = -(-nk // tb) + n_exp
    n_slots = n_blocks * tb
    cnt = counts[0, :n_exp].astype(jnp.int32)
    padded = (cnt + tb - 1) // tb * tb
    pad_end = jnp.cumsum(padded)
    pad_start = pad_end - padded
    top_e = ridx[:, :TOP_K]
    dest = (pad_start[top_e] + ridx[:, TOP_K:2 * TOP_K]).reshape(nk)
    blk_first = jnp.arange(n_blocks, dtype=jnp.int32) * tb
    blk_e = jnp.minimum(jnp.sum((pad_end[None, :] <= blk_first[:, None]).astype(jnp.int32), axis=1),
                        n_exp - 1)
    nvalid = (pad_end[-1:] // tb).astype(jnp.int32)

    xs = _dispatch(dest, pad_start + cnt, pad_end, h2, n_slots, n_exp, tb)
    nxt = (pad_end[blk_e] // tb).astype(jnp.int32)
    act = _ffn_up(blk_e, nxt, nvalid, xs, w_gate_up, b_gate_up[:, None, :], tb)
    outs = _ffn_down(blk_e, nxt, nvalid, act, w_down, b_down[:, None, :], tb)
    return _combine(dest, outs, w_top, x2, final_gain)


def kernel(x, norm_mix, w_in, cmp_pos_k, cmp_pos_v, w_cmp_k1, w_cmp_k2, w_cmp_v1, w_cmp_v2,
           w_gla_alpha, b_gla_alpha, gla_norm, w_proj_nsa, w_proj_gla, w_merge_gate, b_merge_gate,
           w_out, norm_moe, w_router, b_router, w_gate_up, b_gate_up, w_down, b_down, norm_final):
    batch, seq, d = x.shape
    depth = w_in.shape[0]
    assert depth == 1, "the final norm is fused into the (single) layer's combine stage"
    y = _layer(x.reshape(batch * seq, d), batch, seq, norm_mix[0], w_in[0], cmp_pos_k[0],
               cmp_pos_v[0], w_cmp_k1[0], w_cmp_k2[0], w_cmp_v1[0], w_cmp_v2[0], w_gla_alpha[0],
               b_gla_alpha[0], gla_norm[0], w_proj_nsa[0], w_proj_gla[0], w_merge_gate[0],
               b_merge_gate[0], w_out[0], norm_moe[0], w_router[0], b_router[0], w_gate_up[0],
               b_gate_up[0], w_down[0], b_down[0], norm_final[None, :])
    return y.reshape(batch, seq, d)
```

```python
import functools

import jax
import jax.numpy as jnp
import numpy as np
from jax import lax
from jax.experimental import pallas as pl
from jax.experimental.pallas import tpu as pltpu

F32 = jnp.float32
BF16 = jnp.bfloat16

NORM_EPS = 1e-5
ROPE_THETA = 10000.0
NEG_INF = -1e30

NSA_HEADS = 8
NSA_GROUPS = 2
NSA_HPG = NSA_HEADS // NSA_GROUPS
DH = 128
CMP_LEN = 32
CMP_STRIDE = 16
SEL_LEN = 64
SEL_TOPK = 16
SEL_FORCE = 1e3
SEL_MASK = 2.0 ** 100
WINDOW = 512

GLA_HEADS = 4
GLA_DK = 128
GLA_DV = 256
GLA_RANK = 16
GLA_TAU = 16.0
GLA_CHUNK = 64

TOP_K = 4
SWIGLU_LIMIT = 7.0
SWIGLU_ALPHA = 1.702

LANES = 128
VMEM_LIMIT = 56 * 1024 * 1024

NSA_Q = NSA_HEADS * DH
NSA_KV = NSA_GROUPS * DH
GLA_QK = GLA_HEADS * GLA_DK
GLA_V = GLA_HEADS * GLA_DV
MAIN_W = NSA_Q + 6 * NSA_KV + 2 * GLA_QK + 2 * GLA_V
SMALL_W = (NSA_GROUPS + 1) * LANES


def _cparams(sem, vmem=VMEM_LIMIT):
    return pltpu.CompilerParams(dimension_semantics=sem, vmem_limit_bytes=vmem)


def _rope(x, cos, sin_signed):
    return x * cos + pltpu.roll(x, DH // 2, axis=1) * sin_signed


def _proj_kernel(x_ref, gain_ref, w_ref, wsmall_ref, bias_ref, cos_ref, sin_ref,
                 big_ref, small_ref, h_scr, *, n_merge_tiles, rope_tiles):
    j = pl.program_id(1)

    @pl.when(j == 0)
    def _():
        x = x_ref[...]
        ms = jnp.mean(x * x, axis=-1, keepdims=True)
        hb = (x * lax.rsqrt(ms + NORM_EPS) * gain_ref[...]).astype(BF16)
        h_scr[...] = hb
        small_ref[...] = jnp.dot(hb, wsmall_ref[...], preferred_element_type=F32)

    acc = jnp.dot(h_scr[...], w_ref[...], preferred_element_type=F32)
    is_merge = j < n_merge_tiles
    is_rope = (j == rope_tiles[0]) | (j == rope_tiles[1])

    @pl.when(is_merge)
    def _():
        big_ref[...] = jax.nn.sigmoid(acc + bias_ref[...]).astype(BF16)

    @pl.when(is_rope)
    def _():
        cos = cos_ref[...]
        sin = sin_ref[...]
        for g in range(NSA_GROUPS):
            sl = slice(g * DH, (g + 1) * DH)
            big_ref[:, sl] = _rope(acc[:, sl], cos, sin).astype(BF16)
        big_ref[:, NSA_KV:] = acc[:, NSA_KV:].astype(BF16)

    @pl.when(jnp.logical_not(is_merge | is_rope))
    def _():
        big_ref[...] = acc.astype(BF16)


def _proj(x2d, gain, w_all, w_small, bias_all, cos2, sin2, seq, d_merge):
    n, d = x2d.shape
    width = w_all.shape[1]
    tm = min(1024, seq)
    tn = 512
    n_merge_tiles = d_merge // tn
    rope_tiles = (n_merge_tiles + 3, n_merge_tiles + 4)
    nsb = seq // tm
    kern = functools.partial(_proj_kernel, n_merge_tiles=n_merge_tiles, rope_tiles=rope_tiles)
    return pl.pallas_call(
        kern,
        grid=(n // tm, width // tn),
        in_specs=[
            pl.BlockSpec((tm, d), lambda i, j: (i, 0)),
            pl.BlockSpec((1, d), lambda i, j: (0, 0)),
            pl.BlockSpec((d, tn), lambda i, j: (0, j)),
            pl.BlockSpec((d, SMALL_W), lambda i, j: (0, 0)),
            pl.BlockSpec((1, tn), lambda i, j: (0, j)),
            pl.BlockSpec((tm, DH), lambda i, j: (i % nsb, 0)),
            pl.BlockSpec((tm, DH), lambda i, j: (i % nsb, 0)),
        ],
        out_specs=[
            pl.BlockSpec((tm, tn), lambda i, j: (i, j)),
            pl.BlockSpec((tm, SMALL_W), lambda i, j: (i, 0)),
        ],
        out_shape=[
            jax.ShapeDtypeStruct((n, width), BF16),
            jax.ShapeDtypeStruct((n, SMALL_W), F32),
        ],
        scratch_shapes=[pltpu.VMEM((tm, d), BF16)],
        compiler_params=_cparams(("parallel", "arbitrary")),
        name="proj",
    )(x2d, gain, w_all, w_small, bias_all, cos2, sin2)


def _compress_kernel(kv_ref, pos_ref, w1_ref, w2_ref, out_ref, scr, *, seq, ncp):
    nreal = seq // CMP_STRIDE
    scr[0:seq, :] = kv_ref[...].astype(F32)
    scr[seq:seq + CMP_LEN, :] = jnp.zeros((CMP_LEN, DH), F32)
    acc = jnp.zeros((nreal, w1_ref.shape[1]), F32)
    for l in range(CMP_LEN):
        a = scr[pl.ds(l, nreal, stride=CMP_STRIDE), :] + pos_ref[l:l + 1, :]
        acc = acc + jnp.dot(a.astype(BF16), w1_ref[l * DH:(l + 1) * DH, :],
                            preferred_element_type=F32)
    hid = jax.nn.gelu(acc)
    out = jnp.dot(hid.astype(BF16), w2_ref[...], preferred_element_type=F32)
    row = lax.broadcasted_iota(jnp.int32, out.shape, 0)
    out = jnp.where(row < nreal - 1, out, 0.0).astype(BF16)
    if ncp > nreal:
        out = jnp.concatenate([out, jnp.zeros((ncp - nreal, DH), BF16)], axis=0)
    out_ref[...] = out


def _compress(big, pos, w1, w2, batch, seq, col0):
    ncp = max(seq // CMP_STRIDE, LANES)
    kern = functools.partial(_compress_kernel, seq=seq, ncp=ncp)
    hid = w1.shape[2]
    return pl.pallas_call(
        kern,
        grid=(batch, NSA_GROUPS, 2),
        in_specs=[
            pl.BlockSpec((seq, DH), lambda b, g, t: (b, col0 + 2 * t + g)),
            pl.BlockSpec((None, CMP_LEN, DH), lambda b, g, t: (t, 0, 0)),
            pl.BlockSpec((None, CMP_LEN * DH, hid), lambda b, g, t: (t, 0, 0)),
            pl.BlockSpec((None, hid, DH), lambda b, g, t: (t, 0, 0)),
        ],
        out_specs=pl.BlockSpec((None, None, None, ncp, DH), lambda b, g, t: (b, g, t, 0, 0)),
        out_shape=jax.ShapeDtypeStruct((batch, NSA_GROUPS, 2, ncp, DH), BF16),
        scratch_shapes=[pltpu.VMEM((seq + CMP_LEN, DH), F32)],
        compiler_params=_cparams(("parallel", "parallel", "arbitrary")),
        name="compress",
    )(big, pos, w1, w2)


def _stack_heads(t):
    return jnp.concatenate([t[:, h * DH:(h + 1) * DH] for h in range(NSA_HPG)], axis=0)


def _nsa_kernel(q_ref, kc_ref, vc_ref, ks_ref, vs_ref, kw_ref, vw_ref, cos_ref, sin_ref,
                gate_ref, cover_ref, et_ref, wbias_ref, o_ref, kext_scr, vsext_scr, vwext_scr,
                *, seq, tq, ck):
    i = pl.program_id(2)
    q0 = i * tq
    scale = DH ** -0.5
    rows = NSA_HPG * tq
    ncp = kc_ref.shape[0]
    n_blk = seq // SEL_LEN
    n_sel = min(SEL_TOPK, n_blk)
    wspan = tq + WINDOW
    nt = (((1,), (1,)), ((), ()))

    @pl.when(i == 0)
    def _():
        ones = jnp.ones((seq, DH), BF16)
        kext_scr[:, :DH] = ks_ref[...]
        kext_scr[:, DH:] = et_ref[...]
        vsext_scr[:, :DH] = vs_ref[...]
        vsext_scr[:, DH:] = ones
        vwext_scr[:, :DH] = vw_ref[...]
        vwext_scr[:, DH:] = ones

    q = q_ref[...]
    qs = _stack_heads(q)
    pos_q = q0 + lax.broadcasted_iota(jnp.int32, (tq, 1), 0)
    pos_rows = jnp.concatenate([pos_q] * NSA_HPG, axis=0)

    s = lax.dot_general(qs, kc_ref[...], nt, preferred_element_type=F32) * scale
    n_idx = lax.broadcasted_iota(jnp.int32, (1, ncp), 1)
    cmask = (n_idx * CMP_STRIDE + (CMP_LEN - 1)) <= pos_rows
    s = jnp.where(cmask, s, NEG_INF)
    m = jnp.max(s, axis=-1, keepdims=True)
    e = jnp.exp(s - m)
    p = jnp.where(cmask, e / jnp.sum(e, axis=-1, keepdims=True), 0.0)
    o_cmp = jnp.dot(p.astype(BF16), vc_ref[...], preferred_element_type=F32)

    psum = p[0:tq]
    for h in range(1, NSA_HPG):
        psum = psum + p[h * tq:(h + 1) * tq]
    imp = jnp.dot(psum, cover_ref[...], preferred_element_type=F32,
                  precision=lax.Precision.HIGHEST)
    blk = lax.broadcasted_iota(jnp.int32, (tq, LANES), 1)
    t_blk = pos_q // SEL_LEN
    forced = (blk == 0) | (blk == t_blk) | (blk == t_blk - 1)
    bonus = jnp.where(blk > t_blk, -SEL_FORCE, jnp.where(forced, SEL_FORCE, 0.0))
    val_t = (imp + bonus).T[:n_blk]
    blk_t = lax.broadcasted_iota(jnp.int32, (n_blk, tq), 0)
    rank = jnp.zeros((n_blk, tq), F32)
    for c in range(n_blk):
        vc = val_t[c:c + 1, :]
        beats = (vc > val_t) | ((vc == val_t) & (blk_t > c))
        rank = rank + jnp.where(beats, 1.0, 0.0)
    pen_t = jnp.where(rank < n_sel, 0.0, -SEL_MASK)
    pen_t = jnp.concatenate([pen_t, jnp.zeros((LANES - n_blk, tq), F32)], axis=0)
    pen = pen_t.T.astype(BF16)

    cos = cos_ref[...]
    sin = sin_ref[...]
    qr = jnp.concatenate(
        [(_rope(q[:, h * DH:(h + 1) * DH].astype(F32), cos, sin) * scale).astype(BF16)
         for h in range(NSA_HPG)], axis=0)
    q_ext = jnp.concatenate([qr, jnp.concatenate([pen] * NSA_HPG, axis=0)], axis=1)

    n_chunks = (q0 + tq + ck - 1) // ck

    q_heads = [q_ext[h * tq:(h + 1) * tq] for h in range(NSA_HPG)]

    def sel_chunk(c, carry, causal):
        k0 = pl.multiple_of(c * ck, ck)
        kblk = kext_scr[pl.ds(k0, ck), :]
        vblk = vsext_scr[pl.ds(k0, ck), :]
        scs = [lax.dot_general(q_heads[h], kblk, nt, preferred_element_type=F32)
               for h in range(NSA_HPG)]
        out = []
        for h in range(NSA_HPG):
            m_i, acc = carry[h]
            sc = scs[h]
            if causal:
                kp = k0 + lax.broadcasted_iota(jnp.int32, (1, ck), 1)
                sc = jnp.where(kp <= pos_q, sc, NEG_INF)
            m_new = jnp.maximum(m_i, jnp.max(sc, axis=-1, keepdims=True))
            alpha = jnp.exp(m_i - m_new)
            pc = jnp.exp(sc - m_new)
            acc = alpha * acc + jnp.dot(pc.astype(BF16), vblk, preferred_element_type=F32)
            out.append((m_new, acc))
        return tuple(out)

    carry = tuple((jnp.full((tq, 1), NEG_INF, F32), jnp.zeros((tq, 2 * DH), F32))
                  for _ in range(NSA_HPG))
    carry = lax.fori_loop(0, n_chunks - 1, lambda c, cr: sel_chunk(c, cr, False), carry)
    carry = sel_chunk(n_chunks - 1, carry, True)

    kstart = pl.multiple_of(jnp.maximum(q0 - WINDOW, 0), tq)
    kwin = kw_ref[pl.ds(kstart, wspan), :]
    vwin = vwext_scr[pl.ds(kstart, wspan), :]
    wbias = wbias_ref[...]

    gates = jax.nn.sigmoid(gate_ref[...])
    sws = [lax.dot_general(qr[h * tq:(h + 1) * tq], kwin, nt, preferred_element_type=F32)
           for h in range(NSA_HPG)]
    for h in range(NSA_HPG):
        acc_s = carry[h][1]
        o_sel = acc_s[:, :DH] / acc_s[:, DH:]
        sw = sws[h] + wbias
        ew = jnp.exp(sw - jnp.max(sw, axis=-1, keepdims=True))
        acc_w = jnp.dot(ew.astype(BF16), vwin, preferred_element_type=F32)
        o_win = acc_w[:, :DH] / acc_w[:, DH:]
        c0 = 3 * h
        o_h = (gates[:, c0:c0 + 1] * o_cmp[h * tq:(h + 1) * tq] + gates[:, c0 + 1:c0 + 2] * o_sel
               + gates[:, c0 + 2:c0 + 3] * o_win)
        o_ref[:, h * DH:(h + 1) * DH] = o_h.astype(BF16)


def _nsa(big, cmp_kv, small, cos2, sin2, cover, et, batch, seq, main0):
    tq = 128
    wbias = _window_bias(tq)
    ck = min(512, seq)
    nq = seq // tq
    ncp = cmp_kv.shape[3]
    wspan = tq + WINDOW
    n_wb = wbias.shape[0]
    kern = functools.partial(_nsa_kernel, seq=seq, tq=tq, ck=ck)
    kv_col = (main0 + NSA_Q) // DH

    def kvspec(which):
        return pl.BlockSpec((seq, DH), lambda b, g, i: (b, kv_col + 2 * which + g))

    return pl.pallas_call(
        kern,
        grid=(batch, NSA_GROUPS, nq),
        in_specs=[
            pl.BlockSpec((tq, NSA_HPG * DH), lambda b, g, i: (b * nq + i, main0 // (NSA_HPG * DH) + g)),
            pl.BlockSpec((None, None, None, ncp, DH), lambda b, g, i: (b, g, 0, 0, 0)),
            pl.BlockSpec((None, None, None, ncp, DH), lambda b, g, i: (b, g, 1, 0, 0)),
            kvspec(2), kvspec(3), kvspec(4), kvspec(5),
            pl.BlockSpec((tq, DH), lambda b, g, i: (i, 0)),
            pl.BlockSpec((tq, DH), lambda b, g, i: (i, 0)),
            pl.BlockSpec((tq, LANES), lambda b, g, i: (b * nq + i, g)),
            pl.BlockSpec((ncp, LANES), lambda b, g, i: (0, 0)),
            pl.BlockSpec((seq, LANES), lambda b, g, i: (0, 0)),
            pl.BlockSpec((None, tq, wspan), lambda b, g, i: (jnp.minimum(i, n_wb - 1), 0, 0)),
        ],
        out_specs=pl.BlockSpec((tq, NSA_HPG * DH), lambda b, g, i: (b * nq + i, g)),
        out_shape=jax.ShapeDtypeStruct((batch * seq, NSA_Q), BF16),
        scratch_shapes=[pltpu.VMEM((seq, 2 * DH), BF16), pltpu.VMEM((seq, 2 * DH), BF16),
                        pltpu.VMEM((seq, 2 * DH), BF16)],
        compiler_params=_cparams(("parallel", "parallel", "arbitrary")),
        name="nsa",
    )(big, cmp_kv, cmp_kv, big, big, big, big, cos2, sin2, small, cover, et, wbias)


def _gla_kernel(q_ref, k_ref, v_ref, r_ref, small_ref, wa_ref, ba_ref, gain_ref, tri_ref,
                o_ref, la_scr, st_scr, *, seq):
    C = GLA_CHUNK
    nc = seq // C
    z = jnp.dot(small_ref[...], wa_ref[...], preferred_element_type=F32,
                precision=lax.Precision.HIGHEST) + ba_ref[...]
    la_scr[...] = jax.nn.log_sigmoid(z) / GLA_TAU
    st_scr[...] = jnp.zeros_like(st_scr)
    ri = lax.broadcasted_iota(jnp.int32, (C, C), 0)
    ci = lax.broadcasted_iota(jnp.int32, (C, C), 1)
    causal = ri >= ci
    qscale = GLA_DK ** -0.5

    def body(c, carry):
        r0 = pl.multiple_of(c * C, C)
        la = la_scr[pl.ds(r0, C), :]
        cum = jnp.dot(tri_ref[...], la, preferred_element_type=F32,
                      precision=lax.Precision.HIGHEST)
        cum_last = cum[C - 1:C, :]
        qc = q_ref[pl.ds(r0, C), :].astype(F32) * qscale
        kc = k_ref[pl.ds(r0, C), :].astype(F32)
        vc = v_ref[pl.ds(r0, C), :]
        q_t = (qc * jnp.exp(cum)).astype(BF16)
        k_t = (kc * jnp.exp(-cum)).astype(BF16)
        attn = lax.dot_general(q_t, k_t, (((1,), (1,)), ((), ())), preferred_element_type=F32)
        attn = jnp.where(causal, attn, 0.0)
        o = jnp.dot(attn.astype(BF16), vc, preferred_element_type=F32)
        st = st_scr[...]
        o = o + lax.dot_general(q_t, st.astype(BF16), (((1,), (1,)), ((), ())),
                                preferred_element_type=F32)
        k_state = (kc * jnp.exp(cum_last - cum)).astype(BF16)
        d_st = lax.dot_general(vc, k_state, (((0,), (0,)), ((), ())),
                               preferred_element_type=F32)
        st_scr[...] = st * jnp.exp(cum_last) + d_st
        o = o * lax.rsqrt(jnp.mean(o * o, axis=-1, keepdims=True) + NORM_EPS)
        o = o * gain_ref[...]
        rr = r_ref[pl.ds(r0, C), :].astype(F32)
        o_ref[pl.ds(r0, C), :] = (o * (rr * jax.nn.sigmoid(rr))).astype(BF16)
        return carry

    lax.fori_loop(0, nc, body, 0, unroll=8)


def _gla(big, small, w_alpha, b_alpha, gain, tri, batch, seq, main0):
    qcol = (main0 + NSA_Q + 6 * NSA_KV) // GLA_DK
    kcol = qcol + GLA_QK // GLA_DK
    vcol = (main0 + NSA_Q + 6 * NSA_KV + 2 * GLA_QK) // GLA_DV
    rcol = vcol + GLA_V // GLA_DV
    kern = functools.partial(_gla_kernel, seq=seq)
    return pl.pallas_call(
        kern,
        grid=(batch, GLA_HEADS),
        in_specs=[
            pl.BlockSpec((seq, GLA_DK), lambda b, h: (b, qcol + h)),
            pl.BlockSpec((seq, GLA_DK), lambda b, h: (b, kcol + h)),
            pl.BlockSpec((seq, GLA_DV), lambda b, h: (b, vcol + h)),
            pl.BlockSpec((seq, GLA_DV), lambda b, h: (b, rcol + h)),
            pl.BlockSpec((seq, LANES), lambda b, h: (b, NSA_GROUPS)),
            pl.BlockSpec((LANES, GLA_DK), lambda b, h: (0, h)),
            pl.BlockSpec((1, GLA_DK), lambda b, h: (0, h)),
            pl.BlockSpec((1, GLA_DV), lambda b, h: (0, h)),
            pl.BlockSpec((GLA_CHUNK, GLA_CHUNK), lambda b, h: (0, 0)),
        ],
        out_specs=pl.BlockSpec((seq, GLA_DV), lambda b, h: (b, h)),
        out_shape=jax.ShapeDtypeStruct((batch * seq, GLA_V), BF16),
        scratch_shapes=[pltpu.VMEM((seq, GLA_DK), F32), pltpu.VMEM((GLA_DV, GLA_DK), F32)],
        compiler_params=_cparams(("parallel", "parallel")),
        name="gla",
    )(big, big, big, big, small, w_alpha, b_alpha, gain, tri)


def _mix_kernel(on_ref, og_ref, ma_ref, mb_ref, x_ref, wpn_ref, wpg_ref, wo_ref, gain_ref,
                wrh_ref, wrl_ref, br_ref, x2_ref, h2_ref, lg_ref):
    a = jnp.dot(on_ref[...], wpn_ref[...], preferred_element_type=F32)
    b = jnp.dot(og_ref[...], wpg_ref[...], preferred_element_type=F32)
    mixed = ma_ref[...].astype(F32) * a + mb_ref[...].astype(F32) * b
    x2 = x_ref[...] + jnp.dot(mixed.astype(BF16), wo_ref[...], preferred_element_type=F32)
    x2_ref[...] = x2
    ms = jnp.mean(x2 * x2, axis=-1, keepdims=True)
    h2 = x2 * lax.rsqrt(ms + NORM_EPS) * gain_ref[...]
    h2_ref[...] = h2
    hi = h2.astype(BF16)
    lo = (h2 - hi.astype(F32)).astype(BF16)
    lg = (jnp.dot(hi, wrh_ref[...], preferred_element_type=F32)
          + jnp.dot(lo, wrh_ref[...], preferred_element_type=F32)
          + jnp.dot(hi, wrl_ref[...], preferred_element_type=F32))
    lg_ref[...] = lg + br_ref[...]


def _mix(o_nsa, o_gla, big, x2d, wpn, wpg, wo, gain, wr_hi, wr_lo, br):
    n, d = x2d.shape
    tm = 256
    const = lambda i: (0, 0)
    return pl.pallas_call(
        _mix_kernel,
        grid=(n // tm,),
        in_specs=[
            pl.BlockSpec((tm, NSA_Q), lambda i: (i, 0)),
            pl.BlockSpec((tm, GLA_V), lambda i: (i, 0)),
            pl.BlockSpec((tm, d), lambda i: (i, 0)),
            pl.BlockSpec((tm, d), lambda i: (i, 1)),
            pl.BlockSpec((tm, d), lambda i: (i, 0)),
            pl.BlockSpec((NSA_Q, d), const, pipeline_mode=pl.Buffered(1)),
            pl.BlockSpec((GLA_V, d), const, pipeline_mode=pl.Buffered(1)),
            pl.BlockSpec((d, d), const, pipeline_mode=pl.Buffered(1)),
            pl.BlockSpec((1, d), const),
            pl.BlockSpec((d, LANES), const),
            pl.BlockSpec((d, LANES), const),
            pl.BlockSpec((1, LANES), const),
        ],
        out_specs=[
            pl.BlockSpec((tm, d), lambda i: (i, 0)),
            pl.BlockSpec((tm, d), lambda i: (i, 0)),
            pl.BlockSpec((tm, LANES), lambda i: (i, 0)),
        ],
        out_shape=[
            jax.ShapeDtypeStruct((n, d), F32),
            jax.ShapeDtypeStruct((n, d), F32),
            jax.ShapeDtypeStruct((n, LANES), F32),
        ],
        compiler_params=_cparams(("parallel",)),
        name="mix",
    )(o_nsa, o_gla, big, big, x2d, wpn, wpg, wo, gain, wr_hi, wr_lo, br)


def _route_kernel(lg_ref, tri_ref, idx_ref, w_ref, cnt_ref, carry_scr, *, n_exp):
    i = pl.program_id(0)
    tr = lg_ref.shape[0]

    @pl.when(i == 0)
    def _():
        carry_scr[...] = jnp.zeros_like(carry_scr)

    lane = lax.broadcasted_iota(jnp.int32, (tr, LANES), 1)
    work = jnp.where(lane < n_exp, lg_ref[...], -3e38)
    onehots, vals, idxs = [], [], []
    for _ in range(TOP_K):
        mval = jnp.max(work, axis=-1, keepdims=True)
        idx = jnp.min(jnp.where(work == mval, lane, LANES), axis=-1, keepdims=True)
        oh = lane == idx
        work = jnp.where(oh, -3e38, work)
        onehots.append(oh)
        vals.append(mval)
        idxs.append(idx)
    exps = [jnp.exp(v - vals[0]) for v in vals]
    den = exps[0] + exps[1] + exps[2] + exps[3]
    onehot = jnp.zeros((tr, LANES), F32)
    for oh in onehots:
        onehot = onehot + jnp.where(oh, 1.0, 0.0)
    before = jnp.dot(tri_ref[...], onehot.astype(BF16), preferred_element_type=F32) + carry_scr[...]
    carry_scr[...] = carry_scr[...] + jnp.sum(onehot, axis=0, keepdims=True)
    idx_out = jnp.zeros((tr, LANES), jnp.int32)
    w_out = jnp.zeros((tr, LANES), F32)
    for k in range(TOP_K):
        rank_k = jnp.sum(jnp.where(onehots[k], before, 0.0), axis=-1, keepdims=True)
        idx_out = jnp.where(lane == k, idxs[k], idx_out)
        idx_out = jnp.where(lane == TOP_K + k, rank_k.astype(jnp.int32), idx_out)
        w_out = jnp.where(lane == k, exps[k] / den, w_out)
    idx_ref[...] = idx_out
    w_ref[...] = w_out
    cnt_ref[...] = carry_scr[...]


def _route(logits, tri, n_exp):
    n = logits.shape[0]
    tr = tri.shape[0]
    kern = functools.partial(_route_kernel, n_exp=n_exp)
    return pl.pallas_call(
        kern,
        grid=(n // tr,),
        in_specs=[pl.BlockSpec((tr, LANES), lambda i: (i, 0)),
                  pl.BlockSpec((tr, tr), lambda i: (0, 0))],
        out_specs=[pl.BlockSpec((tr, LANES), lambda i: (i, 0)),
                   pl.BlockSpec((tr, LANES), lambda i: (i, 0)),
                   pl.BlockSpec((1, LANES), lambda i: (0, 0))],
        out_shape=[jax.ShapeDtypeStruct((n, LANES), jnp.int32),
                   jax.ShapeDtypeStruct((n, LANES), F32),
                   jax.ShapeDtypeStruct((1, LANES), F32)],
        scratch_shapes=[pltpu.VMEM((1, LANES), F32)],
        compiler_params=_cparams(("arbitrary",)),
        name="route",
    )(logits, tri)


def _dispatch_kernel(dest_ref, padlo_ref, padhi_ref, h_ref, xs_hbm, zbuf, sem, zsem, *, td, tb, n_exp):
    i = pl.program_id(0)
    n_blocks = xs_hbm.shape[0] // tb

    @pl.when(i == 0)
    def _():
        zbuf[...] = jnp.zeros_like(zbuf)

        def row_copy(s):
            return pltpu.make_async_copy(zbuf.at[pl.ds(0, 1)], xs_hbm.at[pl.ds(s, 1)], zsem)

        def per_expert(e, c):
            lo = padlo_ref[e]
            hi = padhi_ref[e]
            lax.fori_loop(lo, hi, lambda s, c2: (row_copy(s).start(), c2)[1], 0)
            lax.fori_loop(lo, hi, lambda s, c2: (row_copy(s).wait(), c2)[1], 0)
            return c

        lax.fori_loop(0, n_exp, per_expert, 0)

        def blk_copy(b):
            return pltpu.make_async_copy(zbuf, xs_hbm.at[pl.ds(b * tb, tb)], zsem)

        first_free = padhi_ref[n_exp - 1] // tb
        lax.fori_loop(first_free, n_blocks, lambda b, c: (blk_copy(b).start(), c)[1], 0)
        lax.fori_loop(first_free, n_blocks, lambda b, c: (blk_copy(b).wait(), c)[1], 0)

    base = i * td * TOP_K

    def start(t, c):
        for k in range(TOP_K):
            pltpu.make_async_copy(h_ref.at[pl.ds(t, 1)],
                                  xs_hbm.at[pl.ds(dest_ref[base + t * TOP_K + k], 1)],
                                  sem).start(priority=k % 2)
        return c

    lax.fori_loop(0, td, start, 0, unroll=2)
    for k in range(TOP_K):
        pltpu.make_async_copy(h_ref, xs_hbm.at[pl.ds(0, td)], sem).wait()


def _dispatch(dest, pad_lo, pad_hi, h2, n_slots, n_exp, tb):
    n, d = h2.shape
    td = 256
    kern = functools.partial(_dispatch_kernel, td=td, tb=tb, n_exp=n_exp)
    return pl.pallas_call(
        kern,
        grid_spec=pltpu.PrefetchScalarGridSpec(
            num_scalar_prefetch=3,
            grid=(n // td,),
            in_specs=[pl.BlockSpec((td, d), lambda i, dr, lo, hi: (i, 0))],
            out_specs=pl.BlockSpec(memory_space=pl.ANY),
            scratch_shapes=[pltpu.VMEM((tb, d), F32), pltpu.SemaphoreType.DMA, pltpu.SemaphoreType.DMA],
        ),
        out_shape=jax.ShapeDtypeStruct((n_slots, d), F32),
        compiler_params=_cparams(("arbitrary",)),
        name="dispatch",
    )(dest, pad_lo, pad_hi, h2)


def _expert_changed(blk_e_ref, b):
    return (b == 0) | (blk_e_ref[b] != blk_e_ref[jnp.maximum(b - 1, 0)])


def _stream_expert_weights(blk_e_ref, nxt_ref, nvalid_ref, copies, cast):
    j = pl.program_id(0)
    b = pl.program_id(1)
    nvalid = nvalid_ref[0]

    @pl.when((j == 0) & (b == 0))
    def _():
        for c in copies(blk_e_ref[0], 0):
            c.start(priority=1)

    @pl.when((b < nvalid) & _expert_changed(blk_e_ref, b))
    def _():
        for c in copies(blk_e_ref[b], j):
            c.wait()
        cast()
        nxt = nxt_ref[b]
        more = nxt < nvalid

        @pl.when(more)
        def _():
            for c in copies(blk_e_ref[jnp.minimum(nxt, nvalid - 1)], j):
                c.start(priority=1)

        @pl.when(jnp.logical_not(more) & (j + 1 < pl.num_programs(0)))
        def _():
            for c in copies(blk_e_ref[0], j + 1):
                c.start(priority=1)


def _ffn_up_kernel(blk_e_ref, nxt_ref, nvalid_ref, xs_ref, w_hbm, bg_ref, bu_ref, act_ref,
                   wf32, wbf, sem, *, tf, ff):
    b = pl.program_id(1)
    valid = b < nvalid_ref[0]

    def copies(e, j):
        return [pltpu.make_async_copy(w_hbm.at[e, :, pl.ds(pl.multiple_of(h * ff + j * tf, tf), tf)],
                                      wf32.at[h], sem.at[h]) for h in range(2)]

    def cast():
        wbf[...] = wf32[...].astype(BF16)

    _stream_expert_weights(blk_e_ref, nxt_ref, nvalid_ref, copies, cast)

    @pl.when(valid)
    def _():
        x = xs_ref[...].astype(BF16)
        gate = jnp.dot(x, wbf[0], preferred_element_type=F32) + bg_ref[...]
        up = jnp.dot(x, wbf[1], preferred_element_type=F32) + bu_ref[...]
        gate = jnp.minimum(gate, SWIGLU_LIMIT)
        up = jnp.clip(up, -SWIGLU_LIMIT, SWIGLU_LIMIT)
        act_ref[...] = ((up + 1.0) * gate * jax.nn.sigmoid(gate * SWIGLU_ALPHA)).astype(BF16)

    @pl.when(jnp.logical_not(valid))
    def _():
        act_ref[...] = jnp.zeros_like(act_ref)


def _ffn_up(blk_e, nxt, nvalid, xs, w_gate_up, b_gate_up, tb):
    n_slots, d = xs.shape
    n_exp, _, f2 = w_gate_up.shape
    ff = f2 // 2
    tf = min(1024, ff)
    nf = ff // tf
    nb = n_slots // tb

    def xmap(j, b, be, nx, nv):
        return (jnp.minimum(b, nv[0] - 1), 0)

    kern = functools.partial(_ffn_up_kernel, tf=tf, ff=ff)
    return pl.pallas_call(
        kern,
        grid_spec=pltpu.PrefetchScalarGridSpec(
            num_scalar_prefetch=3,
            grid=(nf, nb),
            in_specs=[
                pl.BlockSpec((tb, d), xmap),
                pl.BlockSpec(memory_space=pl.ANY),
                pl.BlockSpec((None, 1, tf), lambda j, b, be, nx, nv: (be[b], 0, j)),
                pl.BlockSpec((None, 1, tf), lambda j, b, be, nx, nv: (be[b], 0, nf + j)),
            ],
            out_specs=pl.BlockSpec((tb, tf), lambda j, b, be, nx, nv: (b, j)),
            scratch_shapes=[pltpu.VMEM((2, d, tf), F32), pltpu.VMEM((2, d, tf), BF16),
                            pltpu.SemaphoreType.DMA((2,))],
        ),
        out_shape=jax.ShapeDtypeStruct((n_slots, ff), BF16),
        compiler_params=_cparams(("arbitrary", "arbitrary")),
        name="ffn_up",
    )(blk_e, nxt, nvalid, xs, w_gate_up, b_gate_up, b_gate_up)


def _ffn_down_kernel(blk_e_ref, nxt_ref, nvalid_ref, act_ref, w_hbm, bd_ref, out_ref,
                     wf32, wbf, sem, *, tn):
    b = pl.program_id(1)
    valid = b < nvalid_ref[0]

    def copies(e, j):
        return [pltpu.make_async_copy(w_hbm.at[e, :, pl.ds(pl.multiple_of(j * tn, tn), tn)],
                                      wf32, sem.at[0])]

    def cast():
        wbf[...] = wf32[...].astype(BF16)

    _stream_expert_weights(blk_e_ref, nxt_ref, nvalid_ref, copies, cast)

    @pl.when(valid)
    def _():
        out_ref[...] = jnp.dot(act_ref[...], wbf[...], preferred_element_type=F32) + bd_ref[...]

    @pl.when(jnp.logical_not(valid))
    def _():
        out_ref[...] = jnp.zeros_like(out_ref)


def _ffn_down(blk_e, nxt, nvalid, act, w_down, b_down, tb):
    n_slots, ff = act.shape
    n_exp, _, d = w_down.shape
    tn = min(2048, d)
    nn = d // tn
    nb = n_slots // tb

    def amap(j, b, be, nx, nv):
        return (jnp.minimum(b, nv[0] - 1), 0)

    kern = functools.partial(_ffn_down_kernel, tn=tn)
    return pl.pallas_call(
        kern,
        grid_spec=pltpu.PrefetchScalarGridSpec(
            num_scalar_prefetch=3,
            grid=(nn, nb),
            in_specs=[
                pl.BlockSpec((tb, ff), amap),
                pl.BlockSpec(memory_space=pl.ANY),
                pl.BlockSpec((None, 1, tn), lambda j, b, be, nx, nv: (be[b], 0, j)),
            ],
            out_specs=pl.BlockSpec((tb, tn), lambda j, b, be, nx, nv: (b, j)),
            scratch_shapes=[pltpu.VMEM((ff, tn), F32), pltpu.VMEM((ff, tn), BF16),
                            pltpu.SemaphoreType.DMA((1,))],
        ),
        out_shape=jax.ShapeDtypeStruct((n_slots, d), F32),
        compiler_params=_cparams(("arbitrary", "arbitrary")),
        name="ffn_down",
    )(blk_e, nxt, nvalid, act, w_down, b_down)


def _combine_kernel(dest_ref, outs_hbm, w_ref, x2_ref, gain_ref, y_ref, buf, sem, *, tc):
    i = pl.program_id(0)
    slot = i % 2

    def gather(step, into):
        base = step * tc * TOP_K

        def start(t, c):
            for k in range(TOP_K):
                pltpu.make_async_copy(outs_hbm.at[pl.ds(dest_ref[base + t * TOP_K + k], 1)],
                                      buf.at[into, k, pl.ds(t, 1)], sem.at[into]).start(priority=k % 2)
            return c

        lax.fori_loop(0, tc, start, 0, unroll=2)

    @pl.when(i == 0)
    def _():
        gather(0, 0)

    @pl.when(i + 1 < pl.num_programs(0))
    def _():
        gather(i + 1, 1 - slot)

    for k in range(TOP_K):
        pltpu.make_async_copy(outs_hbm.at[pl.ds(0, tc)], buf.at[slot, k], sem.at[slot]).wait()
    w = w_ref[...]
    y = x2_ref[...]
    for k in range(TOP_K):
        y = y + w[:, k:k + 1] * buf[slot, k]
    ms = jnp.mean(y * y, axis=-1, keepdims=True)
    y_ref[...] = y * lax.rsqrt(ms + NORM_EPS) * gain_ref[...]


def _combine(dest, outs, w_top, x2, gain):
    n, d = x2.shape
    tc = 128
    kern = functools.partial(_combine_kernel, tc=tc)
    return pl.pallas_call(
        kern,
        grid_spec=pltpu.PrefetchScalarGridSpec(
            num_scalar_prefetch=1,
            grid=(n // tc,),
            in_specs=[
                pl.BlockSpec(memory_space=pl.ANY),
                pl.BlockSpec((tc, LANES), lambda i, dr: (i, 0)),
                pl.BlockSpec((tc, d), lambda i, dr: (i, 0)),
                pl.BlockSpec((1, d), lambda i, dr: (0, 0)),
            ],
            out_specs=pl.BlockSpec((tc, d), lambda i, dr: (i, 0)),
            scratch_shapes=[pltpu.VMEM((2, TOP_K, tc, d), F32), pltpu.SemaphoreType.DMA((2,))],
        ),
        out_shape=jax.ShapeDtypeStruct((n, d), F32),
        compiler_params=_cparams(("arbitrary",)),
        name="combine",
    )(dest, outs, w_top, x2, gain)


def _window_bias(tq):
    nv = WINDOW // tq
    r = np.arange(tq)[None, :, None]
    c = np.arange(tq + WINDOW)[None, None, :]
    v = np.arange(nv + 1)[:, None, None]
    q0 = v * tq
    kstart = np.maximum(q0 - WINDOW, 0)
    rel = (kstart + c) - (q0 + r)
    return np.where((rel <= 0) & (rel > -WINDOW), 0.0, NEG_INF).astype(np.float32)


def _rope_tables(seq):
    half = DH // 2
    inv_freq = np.float32(ROPE_THETA) ** (-np.arange(half, dtype=np.float32) / np.float32(half))
    ang = np.arange(seq, dtype=np.float32)[:, None] * inv_freq[None, :].astype(np.float32)
    cos = np.cos(ang.astype(np.float64)).astype(np.float32)
    sin = np.sin(ang.astype(np.float64)).astype(np.float32)
    return np.concatenate([cos, cos], axis=1), np.concatenate([-sin, sin], axis=1)


def _layer(x2d, batch, seq, norm_mix, w_in, cmp_pos_k, cmp_pos_v, w_cmp_k1, w_cmp_k2, w_cmp_v1,
           w_cmp_v2, w_gla_alpha, b_gla_alpha, gla_norm, w_proj_nsa, w_proj_gla, w_merge_gate,
           b_merge_gate, w_out, norm_moe, w_router, b_router, w_gate_up, b_gate_up, w_down, b_down,
           final_gain):
    n, d = x2d.shape
    n_exp = w_router.shape[1]
    d_merge = w_merge_gate.shape[1]
    main0 = d_merge

    o_g = NSA_Q + 6 * NSA_KV
    o_q = o_g + NSA_HEADS * 3
    o_a = o_q + 2 * GLA_QK + 2 * GLA_V
    w_main = jnp.concatenate([w_in[:, :o_g], w_in[:, o_q:o_a]], axis=1)
    w_all = jnp.concatenate([w_merge_gate, w_main], axis=1).astype(BF16)
    n_gl = NSA_HPG * 3
    small_parts = []
    for g in range(NSA_GROUPS):
        small_parts += [w_in[:, o_g + g * n_gl:o_g + (g + 1) * n_gl], jnp.zeros((d, LANES - n_gl), F32)]
    small_parts += [w_in[:, o_a:], jnp.zeros((d, LANES - GLA_RANK), F32)]
    w_small = jnp.concatenate(small_parts, axis=1).astype(BF16)
    bias_all = jnp.concatenate([b_merge_gate, jnp.zeros((MAIN_W,), F32)])[None, :]
    cos2, sin2 = _rope_tables(seq)

    big, small = _proj(x2d, norm_mix[None, :], w_all, w_small, bias_all, cos2, sin2, seq, d_merge)

    pos = jnp.stack([cmp_pos_k, cmp_pos_v])
    w1 = jnp.stack([w_cmp_k1, w_cmp_v1]).astype(BF16)
    w2 = jnp.stack([w_cmp_k2, w_cmp_v2]).astype(BF16)
    cmp_kv = _compress(big, pos, w1, w2, batch, seq, (main0 + NSA_Q) // DH)
    ncp = cmp_kv.shape[3]
    n_blk = seq // SEL_LEN
    cstart = np.arange(ncp)[:, None] * CMP_STRIDE
    blk = np.arange(LANES)[None, :]
    cover = ((cstart < (blk + 1) * SEL_LEN) & (cstart + CMP_LEN > blk * SEL_LEN)
             & (blk < n_blk) & (np.arange(ncp)[:, None] < seq // CMP_STRIDE - 1)).astype(np.float32)
    et = jnp.asarray((np.arange(seq)[:, None] // SEL_LEN) == np.arange(LANES)[None, :], BF16)
    o_nsa = _nsa(big, cmp_kv, small, cos2, sin2, cover, et, batch, seq, main0)

    tri_c = np.tril(np.ones((GLA_CHUNK, GLA_CHUNK), np.float32))
    w_alpha = jnp.concatenate([w_gla_alpha, jnp.zeros((LANES - GLA_RANK, GLA_QK), F32)], axis=0)
    o_gla = _gla(big, small, w_alpha, b_gla_alpha[None, :], gla_norm[None, :], tri_c,
                 batch, seq, main0)

    wr = jnp.concatenate([w_router, jnp.zeros((d, LANES - n_exp), F32)], axis=1)
    wr_hi = wr.astype(BF16)
    wr_lo = (wr - wr_hi.astype(F32)).astype(BF16)
    br = jnp.concatenate([b_router, jnp.zeros((LANES - n_exp,), F32)])[None, :]
    x2, h2, logits = _mix(o_nsa, o_gla, big, x2d, w_proj_nsa.astype(BF16), w_proj_gla.astype(BF16),
                          w_out.astype(BF16), norm_moe[None, :], wr_hi, wr_lo, br)

    tr = 256
    tri_r = jnp.asarray(np.arange(tr)[:, None] > np.arange(tr)[None, :], BF16)
    ridx, w_top, counts = _route(logits, tri_r, n_exp)

    tb = 256
    nk = n * TOP_K
    n_blocks = -(-nk // tb) + n_exp
    n_slots = n_blocks * tb
    cnt = counts[0, :n_exp].astype(jnp.int32)
    padded = (cnt + tb - 1) // tb * tb
    pad_end = jnp.cumsum(padded)
    pad_start = pad_end - padded
    top_e = ridx[:, :TOP_K]
    dest = (pad_start[top_e] + ridx[:, TOP_K:2 * TOP_K]).reshape(nk)
    blk_first = jnp.arange(n_blocks, dtype=jnp.int32) * tb
    blk_e = jnp.minimum(jnp.sum((pad_end[None, :] <= blk_first[:, None]).astype(jnp.int32), axis=1),
                        n_exp - 1)
    nvalid = (pad_end[-1:] // tb).astype(jnp.int32)

    xs = _dispatch(dest, pad_start + cnt, pad_end, h2, n_slots, n_exp, tb)
    nxt = (pad_end[blk_e] // tb).astype(jnp.int32)
    act = _ffn_up(blk_e, nxt, nvalid, xs, w_gate_up, b_gate_up[:, None, :], tb)
    outs = _ffn_down(blk_e, nxt, nvalid, act, w_down, b_down[:, None, :], tb)
    return _combine(dest, outs, w_top, x2, final_gain)


def kernel(x, norm_mix, w_in, cmp_pos_k, cmp_pos_v, w_cmp_k1, w_cmp_k2, w_cmp_v1, w_cmp_v2,
           w_gla_alpha, b_gla_alpha, gla_norm, w_proj_nsa, w_proj_gla, w_merge_gate, b_merge_gate,
           w_out, norm_moe, w_router, b_router, w_gate_up, b_gate_up, w_down, b_down, norm_final):
    batch, seq, d = x.shape
    depth = w_in.shape[0]
    assert depth == 1, "the final norm is fused into the (single) layer's combine stage"
    y = _layer(x.reshape(batch * seq, d), batch, seq, norm_mix[0], w_in[0], cmp_pos_k[0],
               cmp_pos_v[0], w_cmp_k1[0], w_cmp_k2[0], w_cmp_v1[0], w_cmp_v2[0], w_gla_alpha[0],
               b_gla_alpha[0], gla_norm[0], w_proj_nsa[0], w_proj_gla[0], w_merge_gate[0],
               b_merge_gate[0], w_out[0], norm_moe[0], w_router[0], b_router[0], w_gate_up[0],
               b_gate_up[0], w_down[0], b_down[0], norm_final[None, :])
    return y.reshape(batch, seq, d)
```

```python
import functools

import jax
import jax.numpy as jnp
import numpy as np
from jax import lax
from jax.experimental import pallas as pl
from jax.experimental.pallas import tpu as pltpu

F32 = jnp.float32
BF16 = jnp.bfloat16

NORM_EPS = 1e-5
ROPE_THETA = 10000.0
NEG_INF = -1e30

NSA_HEADS = 8
NSA_GROUPS = 2
NSA_HPG = NSA_HEADS // NSA_GROUPS
DH = 128
CMP_LEN = 32
CMP_STRIDE = 16
SEL_LEN = 64
SEL_TOPK = 16
SEL_FORCE = 1e3
SEL_MASK = 2.0 ** 100
WINDOW = 512

GLA_HEADS = 4
GLA_DK = 128
GLA_DV = 256
GLA_RANK = 16
GLA_TAU = 16.0
GLA_CHUNK = 64
GLA_GROUP = 4

TOP_K = 4
SWIGLU_LIMIT = 7.0
SWIGLU_ALPHA = 1.702

LANES = 128
VMEM_LIMIT = 56 * 1024 * 1024

NSA_Q = NSA_HEADS * DH
NSA_KV = NSA_GROUPS * DH
GLA_QK = GLA_HEADS * GLA_DK
GLA_V = GLA_HEADS * GLA_DV
MAIN_W = NSA_Q + 6 * NSA_KV + 2 * GLA_QK + 2 * GLA_V
SMALL_W = (NSA_GROUPS + 1) * LANES


def _cparams(sem, vmem=VMEM_LIMIT):
    return pltpu.CompilerParams(dimension_semantics=sem, vmem_limit_bytes=vmem)


def _rope(x, cos, sin_signed):
    return x * cos + pltpu.roll(x, DH // 2, axis=1) * sin_signed


def _proj_kernel(x_ref, gain_ref, w_ref, wsmall_ref, bias_ref, cos_ref, sin_ref,
                 big_ref, small_ref, h_scr, *, n_merge_tiles, rope_tiles):
    j = pl.program_id(1)

    @pl.when(j == 0)
    def _():
        x = x_ref[...]
        ms = jnp.mean(x * x, axis=-1, keepdims=True)
        hb = (x * lax.rsqrt(ms + NORM_EPS) * gain_ref[...]).astype(BF16)
        h_scr[...] = hb
        small_ref[...] = jnp.dot(hb, wsmall_ref[...], preferred_element_type=F32)

    acc = jnp.dot(h_scr[...], w_ref[...], preferred_element_type=F32)
    is_merge = j < n_merge_tiles
    is_rope = (j == rope_tiles[0]) | (j == rope_tiles[1])

    @pl.when(is_merge)
    def _():
        big_ref[...] = (acc + bias_ref[...]).astype(BF16)

    @pl.when(is_rope)
    def _():
        cos = cos_ref[...]
        sin = sin_ref[...]
        for g in range(NSA_GROUPS):
            sl = slice(g * DH, (g + 1) * DH)
            big_ref[:, sl] = _rope(acc[:, sl], cos, sin).astype(BF16)
        big_ref[:, NSA_KV:] = acc[:, NSA_KV:].astype(BF16)

    @pl.when(jnp.logical_not(is_merge | is_rope))
    def _():
        big_ref[...] = acc.astype(BF16)


def _proj(x2d, gain, w_all, w_small, bias_all, cos2, sin2, seq, d_merge):
    n, d = x2d.shape
    width = w_all.shape[1]
    tm = min(1024, seq)
    tn = 512
    n_merge_tiles = d_merge // tn
    rope_tiles = (n_merge_tiles + 3, n_merge_tiles + 4)
    nsb = seq // tm
    kern = functools.partial(_proj_kernel, n_merge_tiles=n_merge_tiles, rope_tiles=rope_tiles)
    return pl.pallas_call(
        kern,
        grid=(n // tm, width // tn),
        in_specs=[
            pl.BlockSpec((tm, d), lambda i, j: (i, 0)),
            pl.BlockSpec((1, d), lambda i, j: (0, 0)),
            pl.BlockSpec((d, tn), lambda i, j: (0, j)),
            pl.BlockSpec((d, SMALL_W), lambda i, j: (0, 0)),
            pl.BlockSpec((1, tn), lambda i, j: (0, j)),
            pl.BlockSpec((tm, DH), lambda i, j: (i % nsb, 0)),
            pl.BlockSpec((tm, DH), lambda i, j: (i % nsb, 0)),
        ],
        out_specs=[
            pl.BlockSpec((tm, tn), lambda i, j: (i, j)),
            pl.BlockSpec((tm, SMALL_W), lambda i, j: (i, 0)),
        ],
        out_shape=[
            jax.ShapeDtypeStruct((n, width), BF16),
            jax.ShapeDtypeStruct((n, SMALL_W), F32),
        ],
        scratch_shapes=[pltpu.VMEM((tm, d), BF16)],
        compiler_params=_cparams(("parallel", "arbitrary")),
        name="proj",
    )(x2d, gain, w_all, w_small, bias_all, cos2, sin2)


def _compress_kernel(kv_ref, pos_ref, w1_ref, w2_ref, out_ref, scr, *, seq, ncp):
    nreal = seq // CMP_STRIDE
    scr[0:seq, :] = kv_ref[...].astype(F32)
    scr[seq:seq + CMP_LEN, :] = jnp.zeros((CMP_LEN, DH), F32)
    acc = jnp.zeros((nreal, w1_ref.shape[1]), F32)
    for l in range(CMP_LEN):
        a = scr[pl.ds(l, nreal, stride=CMP_STRIDE), :] + pos_ref[l:l + 1, :]
        acc = acc + jnp.dot(a.astype(BF16), w1_ref[l * DH:(l + 1) * DH, :],
                            preferred_element_type=F32)
    hid = jax.nn.gelu(acc)
    out = jnp.dot(hid.astype(BF16), w2_ref[...], preferred_element_type=F32)
    row = lax.broadcasted_iota(jnp.int32, out.shape, 0)
    out = jnp.where(row < nreal - 1, out, 0.0).astype(BF16)
    if ncp > nreal:
        out = jnp.concatenate([out, jnp.zeros((ncp - nreal, DH), BF16)], axis=0)
    out_ref[...] = out


def _compress(big, pos, w1, w2, batch, seq, col0):
    ncp = max(seq // CMP_STRIDE, LANES)
    kern = functools.partial(_compress_kernel, seq=seq, ncp=ncp)
    hid = w1.shape[2]
    return pl.pallas_call(
        kern,
        grid=(batch, NSA_GROUPS, 2),
        in_specs=[
            pl.BlockSpec((seq, DH), lambda b, g, t: (b, col0 + 2 * t + g)),
            pl.BlockSpec((None, CMP_LEN, DH), lambda b, g, t: (t, 0, 0)),
            pl.BlockSpec((None, CMP_LEN * DH, hid), lambda b, g, t: (t, 0, 0)),
            pl.BlockSpec((None, hid, DH), lambda b, g, t: (t, 0, 0)),
        ],
        out_specs=pl.BlockSpec((None, None, None, ncp, DH), lambda b, g, t: (b, g, t, 0, 0)),
        out_shape=jax.ShapeDtypeStruct((batch, NSA_GROUPS, 2, ncp, DH), BF16),
        scratch_shapes=[pltpu.VMEM((seq + CMP_LEN, DH), F32)],
        compiler_params=_cparams(("parallel", "parallel", "arbitrary")),
        name="compress",
    )(big, pos, w1, w2)


def _stack_heads(t):
    return jnp.concatenate([t[:, h * DH:(h + 1) * DH] for h in range(NSA_HPG)], axis=0)


def _nsa_kernel(q_ref, kc_ref, vc_ref, ks_ref, vs_ref, kw_ref, vw_ref, cos_ref, sin_ref,
                gate_ref, cover_ref, et_ref, wbias_ref, o_ref, kext_scr, vsext_scr, vwext_scr,
                *, seq, tq, ck):
    i = pl.program_id(2)
    q0 = i * tq
    scale = DH ** -0.5
    rows = NSA_HPG * tq
    ncp = kc_ref.shape[0]
    n_blk = seq // SEL_LEN
    n_sel = min(SEL_TOPK, n_blk)
    wspan = tq + WINDOW
    nt = (((1,), (1,)), ((), ()))

    @pl.when(i == 0)
    def _():
        ones = jnp.ones((seq, DH), BF16)
        kext_scr[:, :DH] = ks_ref[...]
        kext_scr[:, DH:] = et_ref[...]
        vsext_scr[:, :DH] = vs_ref[...]
        vsext_scr[:, DH:] = ones
        vwext_scr[:, :DH] = vw_ref[...]
        vwext_scr[:, DH:] = ones

    q = q_ref[...]
    qs = _stack_heads(q)
    pos_q = q0 + lax.broadcasted_iota(jnp.int32, (tq, 1), 0)
    pos_rows = jnp.concatenate([pos_q] * NSA_HPG, axis=0)

    cos = cos_ref[...]
    sin = sin_ref[...]
    qr = jnp.concatenate(
        [(_rope(q[:, h * DH:(h + 1) * DH].astype(F32), cos, sin) * scale).astype(BF16)
         for h in range(NSA_HPG)], axis=0)

    kstart = pl.multiple_of(jnp.maximum(q0 - WINDOW, 0), tq)
    kwin = kw_ref[pl.ds(kstart, wspan), :]
    vwin = vwext_scr[pl.ds(kstart, wspan), :]
    wbias = wbias_ref[...]
    sws = [lax.dot_general(qr[h * tq:(h + 1) * tq], kwin, nt, preferred_element_type=F32)
           for h in range(NSA_HPG)]
    o_wins = []
    for h in range(NSA_HPG):
        sw = sws[h] + wbias
        ew = jnp.exp(sw - jnp.max(sw, axis=-1, keepdims=True))
        acc_w = jnp.dot(ew.astype(BF16), vwin, preferred_element_type=F32)
        o_wins.append(acc_w[:, :DH] / acc_w[:, DH:])

    s = lax.dot_general(qs, kc_ref[...], nt, preferred_element_type=F32) * scale
    n_idx = lax.broadcasted_iota(jnp.int32, (1, ncp), 1)
    cmask = (n_idx * CMP_STRIDE + (CMP_LEN - 1)) <= pos_rows
    s = jnp.where(cmask, s, NEG_INF)
    m = jnp.max(s, axis=-1, keepdims=True)
    e = jnp.exp(s - m)
    p = jnp.where(cmask, e / jnp.sum(e, axis=-1, keepdims=True), 0.0)
    o_cmp = jnp.dot(p.astype(BF16), vc_ref[...], preferred_element_type=F32)

    psum = p[0:tq]
    for h in range(1, NSA_HPG):
        psum = psum + p[h * tq:(h + 1) * tq]
    imp = jnp.dot(psum, cover_ref[...], preferred_element_type=F32,
                  precision=lax.Precision.HIGHEST)
    blk = lax.broadcasted_iota(jnp.int32, (tq, LANES), 1)
    t_blk = pos_q // SEL_LEN
    forced = (blk == 0) | (blk == t_blk) | (blk == t_blk - 1)
    bonus = jnp.where(blk > t_blk, -SEL_FORCE, jnp.where(forced, SEL_FORCE, 0.0))
    val_t = (imp + bonus).T[:n_blk]
    blk_t = lax.broadcasted_iota(jnp.int32, (n_blk, tq), 0)
    terms = []
    for c in range(n_blk):
        vc = val_t[c:c + 1, :]
        beats = (vc > val_t) | ((vc == val_t) & (blk_t > c))
        terms.append(jnp.where(beats, 1.0, 0.0))
    while len(terms) > 1:
        terms = [a + b for a, b in zip(terms[0::2], terms[1::2])]
    rank = terms[0]
    pen_t = jnp.where(rank < n_sel, 0.0, -SEL_MASK)
    pen_t = jnp.concatenate([pen_t, jnp.zeros((LANES - n_blk, tq), F32)], axis=0)
    pen = pen_t.T.astype(BF16)

    q_ext = jnp.concatenate([qr, jnp.concatenate([pen] * NSA_HPG, axis=0)], axis=1)

    n_chunks = (q0 + tq + ck - 1) // ck

    q_heads = [q_ext[h * tq:(h + 1) * tq] for h in range(NSA_HPG)]

    def sel_chunk(c, carry, causal):
        k0 = pl.multiple_of(c * ck, ck)
        kblk = kext_scr[pl.ds(k0, ck), :]
        vblk = vsext_scr[pl.ds(k0, ck), :]
        scs = [lax.dot_general(q_heads[h], kblk, nt, preferred_element_type=F32)
               for h in range(NSA_HPG)]
        out = []
        for h in range(NSA_HPG):
            m_i, acc = carry[h]
            sc = scs[h]
            if causal:
                kp = k0 + lax.broadcasted_iota(jnp.int32, (1, ck), 1)
                sc = jnp.where(kp <= pos_q, sc, NEG_INF)
            m_new = jnp.maximum(m_i, jnp.max(sc, axis=-1, keepdims=True))
            alpha = jnp.exp(m_i - m_new)
            pc = jnp.exp(sc - m_new)
            acc = alpha * acc + jnp.dot(pc.astype(BF16), vblk, preferred_element_type=F32)
            out.append((m_new, acc))
        return tuple(out)

    carry = tuple((jnp.full((tq, 1), NEG_INF, F32), jnp.zeros((tq, 2 * DH), F32))
                  for _ in range(NSA_HPG))
    carry = lax.fori_loop(0, n_chunks - 1, lambda c, cr: sel_chunk(c, cr, False), carry)
    carry = sel_chunk(n_chunks - 1, carry, True)

    gates = jax.nn.sigmoid(gate_ref[...])
    for h in range(NSA_HPG):
        acc_s = carry[h][1]
        o_sel = acc_s[:, :DH] / acc_s[:, DH:]
        c0 = 3 * h
        o_h = (gates[:, c0:c0 + 1] * o_cmp[h * tq:(h + 1) * tq] + gates[:, c0 + 1:c0 + 2] * o_sel
               + gates[:, c0 + 2:c0 + 3] * o_wins[h])
        o_ref[:, h * DH:(h + 1) * DH] = o_h.astype(BF16)


def _nsa(big, cmp_kv, small, cos2, sin2, cover, et, batch, seq, main0):
    tq = 128
    wbias = _window_bias(tq)
    ck = min(512, seq)
    nq = seq // tq
    ncp = cmp_kv.shape[3]
    wspan = tq + WINDOW
    n_wb = wbias.shape[0]
    kern = functools.partial(_nsa_kernel, seq=seq, tq=tq, ck=ck)
    kv_col = (main0 + NSA_Q) // DH

    def kvspec(which):
        return pl.BlockSpec((seq, DH), lambda b, g, i: (b, kv_col + 2 * which + g))

    return pl.pallas_call(
        kern,
        grid=(batch, NSA_GROUPS, nq),
        in_specs=[
            pl.BlockSpec((tq, NSA_HPG * DH), lambda b, g, i: (b * nq + i, main0 // (NSA_HPG * DH) + g)),
            pl.BlockSpec((None, None, None, ncp, DH), lambda b, g, i: (b, g, 0, 0, 0)),
            pl.BlockSpec((None, None, None, ncp, DH), lambda b, g, i: (b, g, 1, 0, 0)),
            kvspec(2), kvspec(3), kvspec(4), kvspec(5),
            pl.BlockSpec((tq, DH), lambda b, g, i: (i, 0)),
            pl.BlockSpec((tq, DH), lambda b, g, i: (i, 0)),
            pl.BlockSpec((tq, LANES), lambda b, g, i: (b * nq + i, g)),
            pl.BlockSpec((ncp, LANES), lambda b, g, i: (0, 0)),
            pl.BlockSpec((seq, LANES), lambda b, g, i: (0, 0)),
            pl.BlockSpec((None, tq, wspan), lambda b, g, i: (jnp.minimum(i, n_wb - 1), 0, 0)),
        ],
        out_specs=pl.BlockSpec((tq, NSA_HPG * DH), lambda b, g, i: (b * nq + i, g)),
        out_shape=jax.ShapeDtypeStruct((batch * seq, NSA_Q), BF16),
        scratch_shapes=[pltpu.VMEM((seq, 2 * DH), BF16), pltpu.VMEM((seq, 2 * DH), BF16),
                        pltpu.VMEM((seq, 2 * DH), BF16)],
        compiler_params=_cparams(("parallel", "parallel", "arbitrary")),
        name="nsa",
    )(big, cmp_kv, cmp_kv, big, big, big, big, cos2, sin2, small, cover, et, wbias)


def _gla_kernel(q_ref, k_ref, v_ref, r_ref, small_ref, wa_ref, ba_ref, gain_ref, tri_ref,
                o_ref, qt_scr, kt_scr, ks_scr, dec_scr, *, seq):
    C = GLA_CHUNK
    nc = seq // C
    GR = GLA_GROUP * C
    nt = (((1,), (1,)), ((), ()))
    z = jnp.dot(small_ref[...], wa_ref[...], preferred_element_type=F32,
                precision=lax.Precision.HIGHEST) + ba_ref[...]
    la = jax.nn.log_sigmoid(z) / GLA_TAU

    la_r = jnp.concatenate([la[c * C:(c + 1) * C, :] for c in range(nc)], axis=1)
    cum_r = jnp.dot(tri_ref[...], la_r, preferred_element_type=F32,
                    precision=lax.Precision.HIGHEST)
    last_r = cum_r[C - 1:C, :]
    qf_r = jnp.exp(cum_r)
    kf_r = jnp.exp(-cum_r)
    sf_r = jnp.exp(last_r - cum_r)
    dec_r = jnp.exp(last_r)
    qscale = GLA_DK ** -0.5
    for c in range(nc):
        rows = slice(c * C, (c + 1) * C)
        cols = slice(c * GLA_DK, (c + 1) * GLA_DK)
        qc = q_ref[rows, :].astype(F32) * qscale
        kc = k_ref[rows, :].astype(F32)
        qt_scr[rows, :] = (qc * qf_r[:, cols]).astype(BF16)
        kt_scr[rows, :] = (kc * kf_r[:, cols]).astype(BF16)
        ks_scr[rows, :] = (kc * sf_r[:, cols]).astype(BF16)
        dec_scr[c:c + 1, :] = dec_r[:, cols]

    ri = lax.broadcasted_iota(jnp.int32, (GR, GR), 0)
    ci = lax.broadcasted_iota(jnp.int32, (GR, GR), 1)
    mask = (ri >= ci) & ((ri // C) == (ci // C))

    def body(g, st):
        r0 = pl.multiple_of(g * GR, GR)
        qt = qt_scr[pl.ds(r0, GR), :]
        kt = kt_scr[pl.ds(r0, GR), :]
        ks = ks_scr[pl.ds(r0, GR), :]
        v = v_ref[pl.ds(r0, GR), :]
        attn = lax.dot_general(qt, kt, nt, preferred_element_type=F32)
        attn = jnp.where(mask, attn, 0.0)
        o_intra = jnp.dot(attn.astype(BF16), v, preferred_element_type=F32)
        outs = []
        for cc in range(GLA_GROUP):
            sl = slice(cc * C, (cc + 1) * C)
            outs.append(o_intra[sl] + lax.dot_general(qt[sl], st.astype(BF16), nt,
                                                      preferred_element_type=F32))
            d_st = lax.dot_general(v[sl], ks[sl], (((0,), (0,)), ((), ())),
                                   preferred_element_type=F32)
            st = st * dec_scr[pl.ds(g * GLA_GROUP + cc, 1), :] + d_st
        o = jnp.concatenate(outs, axis=0)
        o = o * lax.rsqrt(jnp.mean(o * o, axis=-1, keepdims=True) + NORM_EPS)
        o = o * gain_ref[...]
        rr = r_ref[pl.ds(r0, GR), :].astype(F32)
        o_ref[pl.ds(r0, GR), :] = (o * (rr * jax.nn.sigmoid(rr))).astype(BF16)
        return st

    lax.fori_loop(0, seq // GR, body, jnp.zeros((GLA_DV, GLA_DK), F32), unroll=2)


def _gla(big, small, w_alpha, b_alpha, gain, tri, batch, seq, main0):
    qcol = (main0 + NSA_Q + 6 * NSA_KV) // GLA_DK
    kcol = qcol + GLA_QK // GLA_DK
    vcol = (main0 + NSA_Q + 6 * NSA_KV + 2 * GLA_QK) // GLA_DV
    rcol = vcol + GLA_V // GLA_DV
    kern = functools.partial(_gla_kernel, seq=seq)
    return pl.pallas_call(
        kern,
        grid=(batch, GLA_HEADS),
        in_specs=[
            pl.BlockSpec((seq, GLA_DK), lambda b, h: (b, qcol + h)),
            pl.BlockSpec((seq, GLA_DK), lambda b, h: (b, kcol + h)),
            pl.BlockSpec((seq, GLA_DV), lambda b, h: (b, vcol + h)),
            pl.BlockSpec((seq, GLA_DV), lambda b, h: (b, rcol + h)),
            pl.BlockSpec((seq, LANES), lambda b, h: (b, NSA_GROUPS)),
            pl.BlockSpec((LANES, GLA_DK), lambda b, h: (0, h)),
            pl.BlockSpec((1, GLA_DK), lambda b, h: (0, h)),
            pl.BlockSpec((1, GLA_DV), lambda b, h: (0, h)),
            pl.BlockSpec((GLA_CHUNK, GLA_CHUNK), lambda b, h: (0, 0)),
        ],
        out_specs=pl.BlockSpec((seq, GLA_DV), lambda b, h: (b, h)),
        out_shape=jax.ShapeDtypeStruct((batch * seq, GLA_V), BF16),
        scratch_shapes=[pltpu.VMEM((seq, GLA_DK), BF16), pltpu.VMEM((seq, GLA_DK), BF16),
                        pltpu.VMEM((seq, GLA_DK), BF16), pltpu.VMEM((seq // GLA_CHUNK, GLA_DK), F32)],
        compiler_params=_cparams(("parallel", "parallel")),
        name="gla",
    )(big, big, big, big, small, w_alpha, b_alpha, gain, tri)


def _mix_kernel(on_ref, og_ref, ma_ref, mb_ref, x_ref, wpn_ref, wpg_ref, wo_ref, gain_ref,
                wrh_ref, wrl_ref, br_ref, x2_ref, h2_ref, lg_ref):
    a = jnp.dot(on_ref[...], wpn_ref[...], preferred_element_type=F32)
    b = jnp.dot(og_ref[...], wpg_ref[...], preferred_element_type=F32)
    mixed = (jax.nn.sigmoid(ma_ref[...].astype(F32)) * a
             + jax.nn.sigmoid(mb_ref[...].astype(F32)) * b)
    x2 = x_ref[...] + jnp.dot(mixed.astype(BF16), wo_ref[...], preferred_element_type=F32)
    x2_ref[...] = x2
    ms = jnp.mean(x2 * x2, axis=-1, keepdims=True)
    h2 = x2 * lax.rsqrt(ms + NORM_EPS) * gain_ref[...]
    h2_ref[...] = h2
    hi = h2.astype(BF16)
    lo = (h2 - hi.astype(F32)).astype(BF16)
    lg = (jnp.dot(hi, wrh_ref[...], preferred_element_type=F32)
          + jnp.dot(lo, wrh_ref[...], preferred_element_type=F32)
          + jnp.dot(hi, wrl_ref[...], preferred_element_type=F32))
    lg_ref[...] = lg + br_ref[...]


def _mix(o_nsa, o_gla, big, x2d, wpn, wpg, wo, gain, wr_hi, wr_lo, br):
    n, d = x2d.shape
    tm = 256
    const = lambda i: (0, 0)
    return pl.pallas_call(
        _mix_kernel,
        grid=(n // tm,),
        in_specs=[
            pl.BlockSpec((tm, NSA_Q), lambda i: (i, 0)),
            pl.BlockSpec((tm, GLA_V), lambda i: (i, 0)),
            pl.BlockSpec((tm, d), lambda i: (i, 0)),
            pl.BlockSpec((tm, d), lambda i: (i, 1)),
            pl.BlockSpec((tm, d), lambda i: (i, 0)),
            pl.BlockSpec((NSA_Q, d), const, pipeline_mode=pl.Buffered(1)),
            pl.BlockSpec((GLA_V, d), const, pipeline_mode=pl.Buffered(1)),
            pl.BlockSpec((d, d), const, pipeline_mode=pl.Buffered(1)),
            pl.BlockSpec((1, d), const),
            pl.BlockSpec((d, LANES), const),
            pl.BlockSpec((d, LANES), const),
            pl.BlockSpec((1, LANES), const),
        ],
        out_specs=[
            pl.BlockSpec((tm, d), lambda i: (i, 0)),
            pl.BlockSpec((tm, d), lambda i: (i, 0)),
            pl.BlockSpec((tm, LANES), lambda i: (i, 0)),
        ],
        out_shape=[
            jax.ShapeDtypeStruct((n, d), F32),
            jax.ShapeDtypeStruct((n, d), F32),
            jax.ShapeDtypeStruct((n, LANES), F32),
        ],
        compiler_params=_cparams(("parallel",)),
        name="mix",
    )(o_nsa, o_gla, big, big, x2d, wpn, wpg, wo, gain, wr_hi, wr_lo, br)


def _route_kernel(lg_ref, tri_ref, idx_ref, w_ref, cnt_ref, carry_scr, *, n_exp):
    i = pl.program_id(0)
    tr = lg_ref.shape[0]

    @pl.when(i == 0)
    def _():
        carry_scr[...] = jnp.zeros_like(carry_scr)

    lane = lax.broadcasted_iota(jnp.int32, (tr, LANES), 1)
    work = jnp.where(lane < n_exp, lg_ref[...], -3e38)
    onehots, vals, idxs = [], [], []
    for _ in range(TOP_K):
        mval = jnp.max(work, axis=-1, keepdims=True)
        idx = jnp.min(jnp.where(work == mval, lane, LANES), axis=-1, keepdims=True)
        oh = lane == idx
        work = jnp.where(oh, -3e38, work)
        onehots.append(oh)
        vals.append(mval)
        idxs.append(idx)
    exps = [jnp.exp(v - vals[0]) for v in vals]
    den = exps[0] + exps[1] + exps[2] + exps[3]
    onehot = jnp.zeros((tr, LANES), F32)
    for oh in onehots:
        onehot = onehot + jnp.where(oh, 1.0, 0.0)
    before = jnp.dot(tri_ref[...], onehot.astype(BF16), preferred_element_type=F32) + carry_scr[...]
    carry_scr[...] = carry_scr[...] + jnp.sum(onehot, axis=0, keepdims=True)
    idx_out = jnp.zeros((tr, LANES), jnp.int32)
    w_out = jnp.zeros((tr, LANES), F32)
    for k in range(TOP_K):
        rank_k = jnp.sum(jnp.where(onehots[k], before, 0.0), axis=-1, keepdims=True)
        idx_out = jnp.where(lane == k, idxs[k], idx_out)
        idx_out = jnp.where(lane == TOP_K + k, rank_k.astype(jnp.int32), idx_out)
        w_out = jnp.where(lane == k, exps[k] / den, w_out)
    idx_ref[...] = idx_out
    w_ref[...] = w_out
    cnt_ref[...] = carry_scr[...]


def _route(logits, tri, n_exp):
    n = logits.shape[0]
    tr = tri.shape[0]
    kern = functools.partial(_route_kernel, n_exp=n_exp)
    return pl.pallas_call(
        kern,
        grid=(n // tr,),
        in_specs=[pl.BlockSpec((tr, LANES), lambda i: (i, 0)),
                  pl.BlockSpec((tr, tr), lambda i: (0, 0))],
        out_specs=[pl.BlockSpec((tr, LANES), lambda i: (i, 0)),
                   pl.BlockSpec((tr, LANES), lambda i: (i, 0)),
                   pl.BlockSpec((1, LANES), lambda i: (0, 0))],
        out_shape=[jax.ShapeDtypeStruct((n, LANES), jnp.int32),
                   jax.ShapeDtypeStruct((n, LANES), F32),
                   jax.ShapeDtypeStruct((1, LANES), F32)],
        scratch_shapes=[pltpu.VMEM((1, LANES), F32)],
        compiler_params=_cparams(("arbitrary",)),
        name="route",
    )(logits, tri)


def _dispatch_kernel(dest_ref, padlo_ref, padhi_ref, h_ref, xs_hbm, zbuf, sem, zsem, *, td, tb, n_exp):
    i = pl.program_id(0)
    n_blocks = xs_hbm.shape[0] // tb

    @pl.when(i == 0)
    def _():
        zbuf[...] = jnp.zeros_like(zbuf)

        def row_copy(s):
            return pltpu.make_async_copy(zbuf.at[pl.ds(0, 1)], xs_hbm.at[pl.ds(s, 1)], zsem)

        def per_expert(e, c):
            lo = padlo_ref[e]
            hi = padhi_ref[e]
            lax.fori_loop(lo, hi, lambda s, c2: (row_copy(s).start(), c2)[1], 0)
            lax.fori_loop(lo, hi, lambda s, c2: (row_copy(s).wait(), c2)[1], 0)
            return c

        lax.fori_loop(0, n_exp, per_expert, 0)

        def blk_copy(b):
            return pltpu.make_async_copy(zbuf, xs_hbm.at[pl.ds(b * tb, tb)], zsem)

        first_free = padhi_ref[n_exp - 1] // tb
        lax.fori_loop(first_free, n_blocks, lambda b, c: (blk_copy(b).start(), c)[1], 0)
        lax.fori_loop(first_free, n_blocks, lambda b, c: (blk_copy(b).wait(), c)[1], 0)

    base = i * td * TOP_K

    def start(t, c):
        for k in range(TOP_K):
            pltpu.make_async_copy(h_ref.at[pl.ds(t, 1)],
                                  xs_hbm.at[pl.ds(dest_ref[base + t * TOP_K + k], 1)],
                                  sem).start(priority=k % 2)
        return c

    lax.fori_loop(0, td, start, 0, unroll=2)
    for k in range(TOP_K):
        pltpu.make_async_copy(h_ref, xs_hbm.at[pl.ds(0, td)], sem).wait()


def _dispatch(dest, pad_lo, pad_hi, h2, n_slots, n_exp, tb):
    n, d = h2.shape
    td = 256
    kern = functools.partial(_dispatch_kernel, td=td, tb=tb, n_exp=n_exp)
    return pl.pallas_call(
        kern,
        grid_spec=pltpu.PrefetchScalarGridSpec(
            num_scalar_prefetch=3,
            grid=(n // td,),
            in_specs=[pl.BlockSpec((td, d), lambda i, dr, lo, hi: (i, 0))],
            out_specs=pl.BlockSpec(memory_space=pl.ANY),
            scratch_shapes=[pltpu.VMEM((tb, d), F32), pltpu.SemaphoreType.DMA, pltpu.SemaphoreType.DMA],
        ),
        out_shape=jax.ShapeDtypeStruct((n_slots, d), F32),
        compiler_params=_cparams(("arbitrary",)),
        name="dispatch",
    )(dest, pad_lo, pad_hi, h2)


def _expert_changed(blk_e_ref, b):
    return (b == 0) | (blk_e_ref[b] != blk_e_ref[jnp.maximum(b - 1, 0)])


def _stream_expert_weights(blk_e_ref, nxt_ref, nvalid_ref, copies, cast):
    j = pl.program_id(0)
    b = pl.program_id(1)
    nvalid = nvalid_ref[0]

    @pl.when((j == 0) & (b == 0))
    def _():
        for c in copies(blk_e_ref[0], 0):
            c.start(priority=1)

    @pl.when((b < nvalid) & _expert_changed(blk_e_ref, b))
    def _():
        for c in copies(blk_e_ref[b], j):
            c.wait()
        cast()
        nxt = nxt_ref[b]
        more = nxt < nvalid

        @pl.when(more)
        def _():
            for c in copies(blk_e_ref[jnp.minimum(nxt, nvalid - 1)], j):
                c.start(priority=1)

        @pl.when(jnp.logical_not(more) & (j + 1 < pl.num_programs(0)))
        def _():
            for c in copies(blk_e_ref[0], j + 1):
                c.start(priority=1)


def _ffn_up_kernel(blk_e_ref, nxt_ref, nvalid_ref, xs_ref, w_hbm, bg_ref, bu_ref, act_ref,
                   wf32, wbf, sem, *, tf, ff):
    b = pl.program_id(1)
    valid = b < nvalid_ref[0]

    def copies(e, j):
        return [pltpu.make_async_copy(w_hbm.at[e, :, pl.ds(pl.multiple_of(h * ff + j * tf, tf), tf)],
                                      wf32.at[h], sem.at[h]) for h in range(2)]

    def cast():
        wbf[...] = wf32[...].astype(BF16)

    _stream_expert_weights(blk_e_ref, nxt_ref, nvalid_ref, copies, cast)

    @pl.when(valid)
    def _():
        x = xs_ref[...].astype(BF16)
        gate = jnp.dot(x, wbf[0], preferred_element_type=F32) + bg_ref[...]
        up = jnp.dot(x, wbf[1], preferred_element_type=F32) + bu_ref[...]
        gate = jnp.minimum(gate, SWIGLU_LIMIT)
        up = jnp.clip(up, -SWIGLU_LIMIT, SWIGLU_LIMIT)
        act_ref[...] = ((up + 1.0) * gate * jax.nn.sigmoid(gate * SWIGLU_ALPHA)).astype(BF16)

    @pl.when(jnp.logical_not(valid))
    def _():
        act_ref[...] = jnp.zeros_like(act_ref)


def _ffn_up(blk_e, nxt, nvalid, xs, w_gate_up, b_gate_up, tb):
    n_slots, d = xs.shape
    n_exp, _, f2 = w_gate_up.shape
    ff = f2 // 2
    tf = min(1024, ff)
    nf = ff // tf
    nb = n_slots // tb

    def xmap(j, b, be, nx, nv):
        return (jnp.minimum(b, nv[0] - 1), 0)

    kern = functools.partial(_ffn_up_kernel, tf=tf, ff=ff)
    return pl.pallas_call(
        kern,
        grid_spec=pltpu.PrefetchScalarGridSpec(
            num_scalar_prefetch=3,
            grid=(nf, nb),
            in_specs=[
                pl.BlockSpec((tb, d), xmap),
                pl.BlockSpec(memory_space=pl.ANY),
                pl.BlockSpec((None, 1, tf), lambda j, b, be, nx, nv: (be[b], 0, j)),
                pl.BlockSpec((None, 1, tf), lambda j, b, be, nx, nv: (be[b], 0, nf + j)),
            ],
            out_specs=pl.BlockSpec((tb, tf), lambda j, b, be, nx, nv: (b, j)),
            scratch_shapes=[pltpu.VMEM((2, d, tf), F32), pltpu.VMEM((2, d, tf), BF16),
                            pltpu.SemaphoreType.DMA((2,))],
        ),
        out_shape=jax.ShapeDtypeStruct((n_slots, ff), BF16),
        compiler_params=_cparams(("arbitrary", "arbitrary")),
        name="ffn_up",
    )(blk_e, nxt, nvalid, xs, w_gate_up, b_gate_up, b_gate_up)


def _ffn_down_kernel(blk_e_ref, nxt_ref, nvalid_ref, act_ref, w_hbm, bd_ref, out_ref,
                     wf32, wbf, sem, *, tn):
    b = pl.program_id(1)
    valid = b < nvalid_ref[0]

    def copies(e, j):
        return [pltpu.make_async_copy(w_hbm.at[e, :, pl.ds(pl.multiple_of(j * tn, tn), tn)],
                                      wf32, sem.at[0])]

    def cast():
        wbf[...] = wf32[...].astype(BF16)

    _stream_expert_weights(blk_e_ref, nxt_ref, nvalid_ref, copies, cast)

    @pl.when(valid)
    def _():
        out_ref[...] = jnp.dot(act_ref[...], wbf[...], preferred_element_type=F32) + bd_ref[...]

    @pl.when(jnp.logical_not(valid))
    def _():
        out_ref[...] = jnp.zeros_like(out_ref)


def _ffn_down(blk_e, nxt, nvalid, act, w_down, b_down, tb):
    n_slots, ff = act.shape
    n_exp, _, d = w_down.shape
    tn = min(2048, d)
    nn = d // tn
    nb = n_slots // tb

    def amap(j, b, be, nx, nv):
        return (jnp.minimum(b, nv[0] - 1), 0)

    kern = functools.partial(_ffn_down_kernel, tn=tn)
    return pl.pallas_call(
        kern,
        grid_spec=pltpu.PrefetchScalarGridSpec(
            num_scalar_prefetch=3,
            grid=(nn, nb),
            in_specs=[
                pl.BlockSpec((tb, ff), amap),
                pl.BlockSpec(memory_space=pl.ANY),
                pl.BlockSpec((None, 1, tn), lambda j, b, be, nx, nv: (be[b], 0, j)),
            ],
            out_specs=pl.BlockSpec((tb, tn), lambda j, b, be, nx, nv: (b, j)),
            scratch_shapes=[pltpu.VMEM((ff, tn), F32), pltpu.VMEM((ff, tn), BF16),
                            pltpu.SemaphoreType.DMA((1,))],
        ),
        out_shape=jax.ShapeDtypeStruct((n_slots, d), F32),
        compiler_params=_cparams(("arbitrary", "arbitrary")),
        name="ffn_down",
    )(blk_e, nxt, nvalid, act, w_down, b_down)


def _combine_kernel(dest_ref, outs_hbm, w_ref, x2_ref, gain_ref, y_ref, buf, sem, *, tc):
    i = pl.program_id(0)
    slot = i % 2

    def gather(step, into):
        base = step * tc * TOP_K

        def start(t, c):
            for k in range(TOP_K):
                pltpu.make_async_copy(outs_hbm.at[pl.ds(dest_ref[base + t * TOP_K + k], 1)],
                                      buf.at[into, k, pl.ds(t, 1)], sem.at[into]).start(priority=k % 2)
            return c

        lax.fori_loop(0, tc, start, 0, unroll=2)

    @pl.when(i == 0)
    def _():
        gather(0, 0)

    @pl.when(i + 1 < pl.num_programs(0))
    def _():
        gather(i + 1, 1 - slot)

    for k in range(TOP_K):
        pltpu.make_async_copy(outs_hbm.at[pl.ds(0, tc)], buf.at[slot, k], sem.at[slot]).wait()
    w = w_ref[...]
    y = x2_ref[...]
    for k in range(TOP_K):
        y = y + w[:, k:k + 1] * buf[slot, k]
    ms = jnp.mean(y * y, axis=-1, keepdims=True)
    y_ref[...] = y * lax.rsqrt(ms + NORM_EPS) * gain_ref[...]


def _combine(dest, outs, w_top, x2, gain):
    n, d = x2.shape
    tc = 128
    kern = functools.partial(_combine_kernel, tc=tc)
    return pl.pallas_call(
        kern,
        grid_spec=pltpu.PrefetchScalarGridSpec(
            num_scalar_prefetch=1,
            grid=(n // tc,),
            in_specs=[
                pl.BlockSpec(memory_space=pl.ANY),
                pl.BlockSpec((tc, LANES), lambda i, dr: (i, 0)),
                pl.BlockSpec((tc, d), lambda i, dr: (i, 0)),
                pl.BlockSpec((1, d), lambda i, dr: (0, 0)),
            ],
            out_specs=pl.BlockSpec((tc, d), lambda i, dr: (i, 0)),
            scratch_shapes=[pltpu.VMEM((2, TOP_K, tc, d), F32), pltpu.SemaphoreType.DMA((2,))],
        ),
        out_shape=jax.ShapeDtypeStruct((n, d), F32),
        compiler_params=_cparams(("arbitrary",)),
        name="combine",
    )(dest, outs, w_top, x2, gain)


def _window_bias(tq):
    nv = WINDOW // tq
    r = np.arange(tq)[None, :, None]
    c = np.arange(tq + WINDOW)[None, None, :]
    v = np.arange(nv + 1)[:, None, None]
    q0 = v * tq
    kstart = np.maximum(q0 - WINDOW, 0)
    rel = (kstart + c) - (q0 + r)
    return np.where((rel <= 0) & (rel > -WINDOW), 0.0, NEG_INF).astype(np.float32)


def _rope_tables(seq):
    half = DH // 2
    inv_freq = np.float32(ROPE_THETA) ** (-np.arange(half, dtype=np.float32) / np.float32(half))
    ang = np.arange(seq, dtype=np.float32)[:, None] * inv_freq[None, :].astype(np.float32)
    cos = np.cos(ang.astype(np.float64)).astype(np.float32)
    sin = np.sin(ang.astype(np.float64)).astype(np.float32)
    return np.concatenate([cos, cos], axis=1), np.concatenate([-sin, sin], axis=1)


def _layer(x2d, batch, seq, norm_mix, w_in, cmp_pos_k, cmp_pos_v, w_cmp_k1, w_cmp_k2, w_cmp_v1,
           w_cmp_v2, w_gla_alpha, b_gla_alpha, gla_norm, w_proj_nsa, w_proj_gla, w_merge_gate,
           b_merge_gate, w_out, norm_moe, w_router, b_router, w_gate_up, b_gate_up, w_down, b_down,
           final_gain):
    n, d = x2d.shape
    n_exp = w_router.shape[1]
    d_merge = w_merge_gate.shape[1]
    main0 = d_merge

    o_g = NSA_Q + 6 * NSA_KV
    o_q = o_g + NSA_HEADS * 3
    o_a = o_q + 2 * GLA_QK + 2 * GLA_V
    w_main = jnp.concatenate([w_in[:, :o_g], w_in[:, o_q:o_a]], axis=1)
    w_all = jnp.concatenate([w_merge_gate, w_main], axis=1).astype(BF16)
    n_gl = NSA_HPG * 3
    small_parts = []
    for g in range(NSA_GROUPS):
        small_parts += [w_in[:, o_g + g * n_gl:o_g + (g + 1) * n_gl], jnp.zeros((d, LANES - n_gl), F32)]
    small_parts += [w_in[:, o_a:], jnp.zeros((d, LANES - GLA_RANK), F32)]
    w_small = jnp.concatenate(small_parts, axis=1).astype(BF16)
    bias_all = jnp.concatenate([b_merge_gate, jnp.zeros((MAIN_W,), F32)])[None, :]
    cos2, sin2 = _rope_tables(seq)

    big, small = _proj(x2d, norm_mix[None, :], w_all, w_small, bias_all, cos2, sin2, seq, d_merge)

    pos = jnp.stack([cmp_pos_k, cmp_pos_v])
    w1 = jnp.stack([w_cmp_k1, w_cmp_v1]).astype(BF16)
    w2 = jnp.stack([w_cmp_k2, w_cmp_v2]).astype(BF16)
    cmp_kv = _compress(big, pos, w1, w2, batch, seq, (main0 + NSA_Q) // DH)
    ncp = cmp_kv.shape[3]
    n_blk = seq // SEL_LEN
    cstart = np.arange(ncp)[:, None] * CMP_STRIDE
    blk = np.arange(LANES)[None, :]
    cover = ((cstart < (blk + 1) * SEL_LEN) & (cstart + CMP_LEN > blk * SEL_LEN)
             & (blk < n_blk) & (np.arange(ncp)[:, None] < seq // CMP_STRIDE - 1)).astype(np.float32)
    et = jnp.asarray((np.arange(seq)[:, None] // SEL_LEN) == np.arange(LANES)[None, :], BF16)
    o_nsa = _nsa(big, cmp_kv, small, cos2, sin2, cover, et, batch, seq, main0)

    tri_c = np.tril(np.ones((GLA_CHUNK, GLA_CHUNK), np.float32))
    w_alpha = jnp.concatenate([w_gla_alpha, jnp.zeros((LANES - GLA_RANK, GLA_QK), F32)], axis=0)
    o_gla = _gla(big, small, w_alpha, b_gla_alpha[None, :], gla_norm[None, :], tri_c,
                 batch, seq, main0)

    wr = jnp.concatenate([w_router, jnp.zeros((d, LANES - n_exp), F32)], axis=1)
    wr_hi = wr.astype(BF16)
    wr_lo = (wr - wr_hi.astype(F32)).astype(BF16)
    br = jnp.concatenate([b_router, jnp.zeros((LANES - n_exp,), F32)])[None, :]
    x2, h2, logits = _mix(o_nsa, o_gla, big, x2d, w_proj_nsa.astype(BF16), w_proj_gla.astype(BF16),
                          w_out.astype(BF16), norm_moe[None, :], wr_hi, wr_lo, br)

    tr = 256
    tri_r = jnp.asarray(np.arange(tr)[:, None] > np.arange(tr)[None, :], BF16)
    ridx, w_top, counts = _route(logits, tri_r, n_exp)

    tb = 256
    nk = n * TOP_K
    n_blocks = -(-nk // tb) + n_exp
    n_slots = n_blocks * tb
    cnt = counts[0, :n_exp].astype(jnp.int32)
    padded = (cnt + tb - 1) // tb * tb
    pad_end = jnp.cumsum(padded)
    pad_start = pad_end - padded
    top_e = ridx[:, :TOP_K]
    dest = (pad_start[top_e] + ridx[:, TOP_K:2 * TOP_K]).reshape(nk)
    blk_first = jnp.arange(n_blocks, dtype=jnp.int32) * tb
    blk_e = jnp.minimum(jnp.sum((pad_end[None, :] <= blk_first[:, None]).astype(jnp.int32), axis=1),
                        n_exp - 1)
    nvalid = (pad_end[-1:] // tb).astype(jnp.int32)

    xs = _dispatch(dest, pad_start + cnt, pad_end, h2, n_slots, n_exp, tb)
    nxt = (pad_end[blk_e] // tb).astype(jnp.int32)
    act = _ffn_up(blk_e, nxt, nvalid, xs, w_gate_up, b_gate_up[:, None, :], tb)
    outs = _ffn_down(blk_e, nxt, nvalid, act, w_down, b_down[:, None, :], tb)
    return _combine(dest, outs, w_top, x2, final_gain)


def kernel(x, norm_mix, w_in, cmp_pos_k, cmp_pos_v, w_cmp_k1, w_cmp_k2, w_cmp_v1, w_cmp_v2,
           w_gla_alpha, b_gla_alpha, gla_norm, w_proj_nsa, w_proj_gla, w_merge_gate, b_merge_gate,
           w_out, norm_moe, w_router, b_router, w_gate_up, b_gate_up, w_down, b_down, norm_final):
    batch, seq, d = x.shape
    depth = w_in.shape[0]
    assert depth == 1, "the final norm is fused into the (single) layer's combine stage"
    y = _layer(x.reshape(batch * seq, d), batch, seq, norm_mix[0], w_in[0], cmp_pos_k[0],
               cmp_pos_v[0], w_cmp_k1[0], w_cmp_k2[0], w_cmp_v1[0], w_cmp_v2[0], w_gla_alpha[0],
               b_gla_alpha[0], gla_norm[0], w_proj_nsa[0], w_proj_gla[0], w_merge_gate[0],
               b_merge_gate[0], w_out[0], norm_moe[0], w_router[0], b_router[0], w_gate_up[0],
               b_gate_up[0], w_down[0], b_down[0], norm_final[None, :])
    return y.reshape(batch, seq, d)
```

```python
import functools

import jax
import jax.numpy as jnp
import numpy as np
from jax import lax
from jax.experimental import pallas as pl
from jax.experimental.pallas import tpu as pltpu

F32 = jnp.float32
BF16 = jnp.bfloat16

NORM_EPS = 1e-5
ROPE_THETA = 10000.0
NEG_INF = -1e30

NSA_HEADS = 8
NSA_GROUPS = 2
NSA_HPG = NSA_HEADS // NSA_GROUPS
DH = 128
CMP_LEN = 32
CMP_STRIDE = 16
SEL_LEN = 64
SEL_TOPK = 16
SEL_FORCE = 1e3
SEL_MASK = 2.0 ** 100
WINDOW = 512

GLA_HEADS = 4
GLA_DK = 128
GLA_DV = 256
GLA_RANK = 16
GLA_TAU = 16.0
GLA_CHUNK = 64
GLA_GROUP = 4

TOP_K = 4
SWIGLU_LIMIT = 7.0
SWIGLU_ALPHA = 1.702

LANES = 128
VMEM_LIMIT = 56 * 1024 * 1024

NSA_Q = NSA_HEADS * DH
NSA_KV = NSA_GROUPS * DH
GLA_QK = GLA_HEADS * GLA_DK
GLA_V = GLA_HEADS * GLA_DV
MAIN_W = NSA_Q + 6 * NSA_KV + 2 * GLA_QK + 2 * GLA_V
SMALL_W = (NSA_GROUPS + 1) * LANES


def _cparams(sem, vmem=VMEM_LIMIT):
    return pltpu.CompilerParams(dimension_semantics=sem, vmem_limit_bytes=vmem)


def _rope(x, cos, sin_signed):
    return x * cos + pltpu.roll(x, DH // 2, axis=1) * sin_signed


def _proj_kernel(x_ref, gain_ref, w_ref, wsmall_ref, bias_ref, cos_ref, sin_ref,
                 big_ref, small_ref, h_scr, *, n_merge_tiles, rope_tiles):
    j = pl.program_id(1)

    @pl.when(j == 0)
    def _():
        x = x_ref[...]
        ms = jnp.mean(x * x, axis=-1, keepdims=True)
        hb = (x * lax.rsqrt(ms + NORM_EPS) * gain_ref[...]).astype(BF16)
        h_scr[...] = hb
        small_ref[...] = jnp.dot(hb, wsmall_ref[...], preferred_element_type=F32)

    acc = jnp.dot(h_scr[...], w_ref[...], preferred_element_type=F32)
    is_merge = j < n_merge_tiles
    is_rope = (j == rope_tiles[0]) | (j == rope_tiles[1])

    @pl.when(is_merge)
    def _():
        big_ref[...] = (acc + bias_ref[...]).astype(BF16)

    @pl.when(is_rope)
    def _():
        cos = cos_ref[...]
        sin = sin_ref[...]
        for g in range(NSA_GROUPS):
            sl = slice(g * DH, (g + 1) * DH)
            big_ref[:, sl] = _rope(acc[:, sl], cos, sin).astype(BF16)
        big_ref[:, NSA_KV:] = acc[:, NSA_KV:].astype(BF16)

    @pl.when(jnp.logical_not(is_merge | is_rope))
    def _():
        big_ref[...] = acc.astype(BF16)


def _proj(x2d, gain, w_all, w_small, bias_all, cos2, sin2, seq, d_merge):
    n, d = x2d.shape
    width = w_all.shape[1]
    tm = min(1024, seq)
    tn = 512
    n_merge_tiles = d_merge // tn
    rope_tiles = (n_merge_tiles + 3, n_merge_tiles + 4)
    nsb = seq // tm
    kern = functools.partial(_proj_kernel, n_merge_tiles=n_merge_tiles, rope_tiles=rope_tiles)
    return pl.pallas_call(
        kern,
        grid=(n // tm, width // tn),
        in_specs=[
            pl.BlockSpec((tm, d), lambda i, j: (i, 0)),
            pl.BlockSpec((1, d), lambda i, j: (0, 0)),
            pl.BlockSpec((d, tn), lambda i, j: (0, j)),
            pl.BlockSpec((d, SMALL_W), lambda i, j: (0, 0)),
            pl.BlockSpec((1, tn), lambda i, j: (0, j)),
            pl.BlockSpec((tm, DH), lambda i, j: (i % nsb, 0)),
            pl.BlockSpec((tm, DH), lambda i, j: (i % nsb, 0)),
        ],
        out_specs=[
            pl.BlockSpec((tm, tn), lambda i, j: (i, j)),
            pl.BlockSpec((tm, SMALL_W), lambda i, j: (i, 0)),
        ],
        out_shape=[
            jax.ShapeDtypeStruct((n, width), BF16),
            jax.ShapeDtypeStruct((n, SMALL_W), F32),
        ],
        scratch_shapes=[pltpu.VMEM((tm, d), BF16)],
        compiler_params=_cparams(("parallel", "arbitrary")),
        name="proj",
    )(x2d, gain, w_all, w_small, bias_all, cos2, sin2)


def _compress_kernel(kv_ref, pos_ref, w1_ref, w2_ref, out_ref, scr, *, seq, ncp):
    nreal = seq // CMP_STRIDE
    scr[0:seq, :] = kv_ref[...].astype(F32)
    scr[seq:seq + CMP_LEN, :] = jnp.zeros((CMP_LEN, DH), F32)
    acc = jnp.zeros((nreal, w1_ref.shape[1]), F32)
    for l in range(CMP_LEN):
        a = scr[pl.ds(l, nreal, stride=CMP_STRIDE), :] + pos_ref[l:l + 1, :]
        acc = acc + jnp.dot(a.astype(BF16), w1_ref[l * DH:(l + 1) * DH, :],
                            preferred_element_type=F32)
    hid = jax.nn.gelu(acc)
    out = jnp.dot(hid.astype(BF16), w2_ref[...], preferred_element_type=F32)
    row = lax.broadcasted_iota(jnp.int32, out.shape, 0)
    out = jnp.where(row < nreal - 1, out, 0.0).astype(BF16)
    if ncp > nreal:
        out = jnp.concatenate([out, jnp.zeros((ncp - nreal, DH), BF16)], axis=0)
    out_ref[...] = out


def _compress(big, pos, w1, w2, batch, seq, col0):
    ncp = max(seq // CMP_STRIDE, LANES)
    kern = functools.partial(_compress_kernel, seq=seq, ncp=ncp)
    hid = w1.shape[2]
    return pl.pallas_call(
        kern,
        grid=(batch, NSA_GROUPS, 2),
        in_specs=[
            pl.BlockSpec((seq, DH), lambda b, g, t: (b, col0 + 2 * t + g)),
            pl.BlockSpec((None, CMP_LEN, DH), lambda b, g, t: (t, 0, 0)),
            pl.BlockSpec((None, CMP_LEN * DH, hid), lambda b, g, t: (t, 0, 0)),
            pl.BlockSpec((None, hid, DH), lambda b, g, t: (t, 0, 0)),
        ],
        out_specs=pl.BlockSpec((None, None, None, ncp, DH), lambda b, g, t: (b, g, t, 0, 0)),
        out_shape=jax.ShapeDtypeStruct((batch, NSA_GROUPS, 2, ncp, DH), BF16),
        scratch_shapes=[pltpu.VMEM((seq + CMP_LEN, DH), F32)],
        compiler_params=_cparams(("parallel", "parallel", "arbitrary")),
        name="compress",
    )(big, pos, w1, w2)


def _stack_heads(t):
    return jnp.concatenate([t[:, h * DH:(h + 1) * DH] for h in range(NSA_HPG)], axis=0)


def _nsa_kernel(q_ref, kc_ref, vc_ref, ks_ref, vs_ref, kw_ref, vw_ref, cos_ref, sin_ref,
                gate_ref, cover_ref, et_ref, wbias_ref, o_ref, kext_scr, vsext_scr, vwext_scr,
                *, seq, tq, ck):
    i = pl.program_id(2)
    q0 = i * tq
    scale = DH ** -0.5
    rows = NSA_HPG * tq
    ncp = kc_ref.shape[0]
    n_blk = seq // SEL_LEN
    n_sel = min(SEL_TOPK, n_blk)
    wspan = tq + WINDOW
    nt = (((1,), (1,)), ((), ()))

    @pl.when(i == 0)
    def _():
        ones = jnp.ones((seq, DH), BF16)
        kext_scr[:, :DH] = ks_ref[...]
        kext_scr[:, DH:] = et_ref[...]
        vsext_scr[:, :DH] = vs_ref[...]
        vsext_scr[:, DH:] = ones
        vwext_scr[:, :DH] = vw_ref[...]
        vwext_scr[:, DH:] = ones

    q = q_ref[...]
    qs = _stack_heads(q)
    pos_q = q0 + lax.broadcasted_iota(jnp.int32, (tq, 1), 0)
    pos_rows = jnp.concatenate([pos_q] * NSA_HPG, axis=0)

    cos = cos_ref[...]
    sin = sin_ref[...]
    qr = jnp.concatenate(
        [(_rope(q[:, h * DH:(h + 1) * DH].astype(F32), cos, sin) * scale).astype(BF16)
         for h in range(NSA_HPG)], axis=0)

    s = lax.dot_general(qs, kc_ref[...], nt, preferred_element_type=F32) * scale
    n_idx = lax.broadcasted_iota(jnp.int32, (1, ncp), 1)
    cmask = (n_idx * CMP_STRIDE + (CMP_LEN - 1)) <= pos_rows
    s = jnp.where(cmask, s, NEG_INF)
    m = jnp.max(s, axis=-1, keepdims=True)
    e = jnp.exp(s - m)
    p = jnp.where(cmask, e / jnp.sum(e, axis=-1, keepdims=True), 0.0)
    o_cmp = jnp.dot(p.astype(BF16), vc_ref[...], preferred_element_type=F32)

    psum = p[0:tq]
    for h in range(1, NSA_HPG):
        psum = psum + p[h * tq:(h + 1) * tq]
    imp = jnp.dot(psum, cover_ref[...], preferred_element_type=F32,
                  precision=lax.Precision.HIGHEST)
    blk = lax.broadcasted_iota(jnp.int32, (tq, LANES), 1)
    t_blk = pos_q // SEL_LEN
    forced = (blk == 0) | (blk == t_blk) | (blk == t_blk - 1)
    bonus = jnp.where(blk > t_blk, -SEL_FORCE, jnp.where(forced, SEL_FORCE, 0.0))
    val_t = (imp + bonus).T[:n_blk]
    blk_t = lax.broadcasted_iota(jnp.int32, (n_blk, tq), 0)
    terms = []
    for c in range(n_blk):
        vc = val_t[c:c + 1, :]
        beats = (vc > val_t) | ((vc == val_t) & (blk_t > c))
        terms.append(jnp.where(beats, 1.0, 0.0))
    while len(terms) > 1:
        terms = [a + b for a, b in zip(terms[0::2], terms[1::2])]
    rank = terms[0]
    pen_t = jnp.where(rank < n_sel, 0.0, -SEL_MASK)
    pen_t = jnp.concatenate([pen_t, jnp.zeros((LANES - n_blk, tq), F32)], axis=0)
    pen = pen_t.T.astype(BF16)

    q_ext = jnp.concatenate([qr, jnp.concatenate([pen] * NSA_HPG, axis=0)], axis=1)

    n_chunks = (q0 + tq + ck - 1) // ck

    q_heads = [q_ext[h * tq:(h + 1) * tq] for h in range(NSA_HPG)]

    def sel_chunk(c, carry, causal):
        k0 = pl.multiple_of(c * ck, ck)
        kblk = kext_scr[pl.ds(k0, ck), :]
        vblk = vsext_scr[pl.ds(k0, ck), :]
        scs = [lax.dot_general(q_heads[h], kblk, nt, preferred_element_type=F32)
               for h in range(NSA_HPG)]
        out = []
        for h in range(NSA_HPG):
            m_i, acc = carry[h]
            sc = scs[h]
            if causal:
                kp = k0 + lax.broadcasted_iota(jnp.int32, (1, ck), 1)
                sc = jnp.where(kp <= pos_q, sc, NEG_INF)
            m_new = jnp.maximum(m_i, jnp.max(sc, axis=-1, keepdims=True))
            alpha = jnp.exp(m_i - m_new)
            pc = jnp.exp(sc - m_new)
            acc = alpha * acc + jnp.dot(pc.astype(BF16), vblk, preferred_element_type=F32)
            out.append((m_new, acc))
        return tuple(out)

    carry = tuple((jnp.full((tq, 1), NEG_INF, F32), jnp.zeros((tq, 2 * DH), F32))
                  for _ in range(NSA_HPG))
    carry = lax.fori_loop(0, n_chunks - 1, lambda c, cr: sel_chunk(c, cr, False), carry)
    carry = sel_chunk(n_chunks - 1, carry, True)

    kstart = pl.multiple_of(jnp.maximum(q0 - WINDOW, 0), tq)
    kwin = kw_ref[pl.ds(kstart, wspan), :]
    vwin = vwext_scr[pl.ds(kstart, wspan), :]
    wbias = wbias_ref[...]
    sws = [lax.dot_general(qr[h * tq:(h + 1) * tq], kwin, nt, preferred_element_type=F32)
           for h in range(NSA_HPG)]

    gates = jax.nn.sigmoid(gate_ref[...])
    for h in range(NSA_HPG):
        acc_s = carry[h][1]
        o_sel = acc_s[:, :DH] / acc_s[:, DH:]
        sw = sws[h] + wbias
        ew = jnp.exp(sw - jnp.max(sw, axis=-1, keepdims=True))
        acc_w = jnp.dot(ew.astype(BF16), vwin, preferred_element_type=F32)
        o_win = acc_w[:, :DH] / acc_w[:, DH:]
        c0 = 3 * h
        o_h = (gates[:, c0:c0 + 1] * o_cmp[h * tq:(h + 1) * tq] + gates[:, c0 + 1:c0 + 2] * o_sel
               + gates[:, c0 + 2:c0 + 3] * o_win)
        o_ref[:, h * DH:(h + 1) * DH] = o_h.astype(BF16)


def _nsa(big, cmp_kv, small, cos2, sin2, cover, et, batch, seq, main0):
    tq = 128
    wbias = _window_bias(tq)
    ck = min(512, seq)
    nq = seq // tq
    ncp = cmp_kv.shape[3]
    wspan = tq + WINDOW
    n_wb = wbias.shape[0]
    kern = functools.partial(_nsa_kernel, seq=seq, tq=tq, ck=ck)
    kv_col = (main0 + NSA_Q) // DH

    def kvspec(which):
        return pl.BlockSpec((seq, DH), lambda b, g, i: (b, kv_col + 2 * which + g))

    return pl.pallas_call(
        kern,
        grid=(batch, NSA_GROUPS, nq),
        in_specs=[
            pl.BlockSpec((tq, NSA_HPG * DH), lambda b, g, i: (b * nq + i, main0 // (NSA_HPG * DH) + g)),
            pl.BlockSpec((None, None, None, ncp, DH), lambda b, g, i: (b, g, 0, 0, 0)),
            pl.BlockSpec((None, None, None, ncp, DH), lambda b, g, i: (b, g, 1, 0, 0)),
            kvspec(2), kvspec(3), kvspec(4), kvspec(5),
            pl.BlockSpec((tq, DH), lambda b, g, i: (i, 0)),
            pl.BlockSpec((tq, DH), lambda b, g, i: (i, 0)),
            pl.BlockSpec((tq, LANES), lambda b, g, i: (b * nq + i, g)),
            pl.BlockSpec((ncp, LANES), lambda b, g, i: (0, 0)),
            pl.BlockSpec((seq, LANES), lambda b, g, i: (0, 0)),
            pl.BlockSpec((None, tq, wspan), lambda b, g, i: (jnp.minimum(i, n_wb - 1), 0, 0)),
        ],
        out_specs=pl.BlockSpec((tq, NSA_HPG * DH), lambda b, g, i: (b * nq + i, g)),
        out_shape=jax.ShapeDtypeStruct((batch * seq, NSA_Q), BF16),
        scratch_shapes=[pltpu.VMEM((seq, 2 * DH), BF16), pltpu.VMEM((seq, 2 * DH), BF16),
                        pltpu.VMEM((seq, 2 * DH), BF16)],
        compiler_params=_cparams(("parallel", "parallel", "arbitrary")),
        name="nsa",
    )(big, cmp_kv, cmp_kv, big, big, big, big, cos2, sin2, small, cover, et, wbias)


def _gla_kernel(q_ref, k_ref, v_ref, r_ref, small_ref, wa_ref, ba_ref, gain_ref, tri_ref,
                o_ref, qt_scr, kt_scr, ks_scr, dec_scr, *, seq):
    C = GLA_CHUNK
    nc = seq // C
    GR = GLA_GROUP * C
    nt = (((1,), (1,)), ((), ()))
    z = jnp.dot(small_ref[...], wa_ref[...], preferred_element_type=F32,
                precision=lax.Precision.HIGHEST) + ba_ref[...]
    la = jax.nn.log_sigmoid(z) / GLA_TAU

    la_r = jnp.concatenate([la[c * C:(c + 1) * C, :] for c in range(nc)], axis=1)
    cum_r = jnp.dot(tri_ref[...], la_r, preferred_element_type=F32,
                    precision=lax.Precision.HIGHEST)
    last_r = cum_r[C - 1:C, :]
    qf_r = jnp.exp(cum_r)
    kf_r = jnp.exp(-cum_r)
    sf_r = jnp.exp(last_r - cum_r)
    dec_r = jnp.exp(last_r)
    qscale = GLA_DK ** -0.5
    for c in range(nc):
        rows = slice(c * C, (c + 1) * C)
        cols = slice(c * GLA_DK, (c + 1) * GLA_DK)
        qc = q_ref[rows, :].astype(F32) * qscale
        kc = k_ref[rows, :].astype(F32)
        qt_scr[rows, :] = (qc * qf_r[:, cols]).astype(BF16)
        kt_scr[rows, :] = (kc * kf_r[:, cols]).astype(BF16)
        ks_scr[rows, :] = (kc * sf_r[:, cols]).astype(BF16)
        dec_scr[c:c + 1, :] = dec_r[:, cols]

    ri = lax.broadcasted_iota(jnp.int32, (GR, GR), 0)
    ci = lax.broadcasted_iota(jnp.int32, (GR, GR), 1)
    mask = (ri >= ci) & ((ri // C) == (ci // C))

    def body(g, st):
        r0 = pl.multiple_of(g * GR, GR)
        qt = qt_scr[pl.ds(r0, GR), :]
        kt = kt_scr[pl.ds(r0, GR), :]
        ks = ks_scr[pl.ds(r0, GR), :]
        v = v_ref[pl.ds(r0, GR), :]
        attn = lax.dot_general(qt, kt, nt, preferred_element_type=F32)
        attn = jnp.where(mask, attn, 0.0)
        o_intra = jnp.dot(attn.astype(BF16), v, preferred_element_type=F32)
        outs = []
        for cc in range(GLA_GROUP):
            sl = slice(cc * C, (cc + 1) * C)
            outs.append(o_intra[sl] + lax.dot_general(qt[sl], st.astype(BF16), nt,
                                                      preferred_element_type=F32))
            d_st = lax.dot_general(v[sl], ks[sl], (((0,), (0,)), ((), ())),
                                   preferred_element_type=F32)
            st = st * dec_scr[pl.ds(g * GLA_GROUP + cc, 1), :] + d_st
        o = jnp.concatenate(outs, axis=0)
        o = o * lax.rsqrt(jnp.mean(o * o, axis=-1, keepdims=True) + NORM_EPS)
        o = o * gain_ref[...]
        rr = r_ref[pl.ds(r0, GR), :].astype(F32)
        o_ref[pl.ds(r0, GR), :] = (o * (rr * jax.nn.sigmoid(rr))).astype(BF16)
        return st

    lax.fori_loop(0, seq // GR, body, jnp.zeros((GLA_DV, GLA_DK), F32), unroll=2)


def _gla(big, small, w_alpha, b_alpha, gain, tri, batch, seq, main0):
    qcol = (main0 + NSA_Q + 6 * NSA_KV) // GLA_DK
    kcol = qcol + GLA_QK // GLA_DK
    vcol = (main0 + NSA_Q + 6 * NSA_KV + 2 * GLA_QK) // GLA_DV
    rcol = vcol + GLA_V // GLA_DV
    kern = functools.partial(_gla_kernel, seq=seq)
    return pl.pallas_call(
        kern,
        grid=(batch, GLA_HEADS),
        in_specs=[
            pl.BlockSpec((seq, GLA_DK), lambda b, h: (b, qcol + h)),
            pl.BlockSpec((seq, GLA_DK), lambda b, h: (b, kcol + h)),
            pl.BlockSpec((seq, GLA_DV), lambda b, h: (b, vcol + h)),
            pl.BlockSpec((seq, GLA_DV), lambda b, h: (b, rcol + h)),
            pl.BlockSpec((seq, LANES), lambda b, h: (b, NSA_GROUPS)),
            pl.BlockSpec((LANES, GLA_DK), lambda b, h: (0, h)),
            pl.BlockSpec((1, GLA_DK), lambda b, h: (0, h)),
            pl.BlockSpec((1, GLA_DV), lambda b, h: (0, h)),
            pl.BlockSpec((GLA_CHUNK, GLA_CHUNK), lambda b, h: (0, 0)),
        ],
        out_specs=pl.BlockSpec((seq, GLA_DV), lambda b, h: (b, h)),
        out_shape=jax.ShapeDtypeStruct((batch * seq, GLA_V), BF16),
        scratch_shapes=[pltpu.VMEM((seq, GLA_DK), BF16), pltpu.VMEM((seq, GLA_DK), BF16),
                        pltpu.VMEM((seq, GLA_DK), BF16), pltpu.VMEM((seq // GLA_CHUNK, GLA_DK), F32)],
        compiler_params=_cparams(("parallel", "parallel")),
        name="gla",
    )(big, big, big, big, small, w_alpha, b_alpha, gain, tri)


def _mix_kernel(on_ref, og_ref, ma_ref, mb_ref, x_ref, wpn_ref, wpg_ref, wo_ref, gain_ref,
                wrh_ref, wrl_ref, br_ref, x2_ref, h2_ref, lg_ref):
    a = jnp.dot(on_ref[...], wpn_ref[...], preferred_element_type=F32)
    b = jnp.dot(og_ref[...], wpg_ref[...], preferred_element_type=F32)
    mixed = (jax.nn.sigmoid(ma_ref[...].astype(F32)) * a
             + jax.nn.sigmoid(mb_ref[...].astype(F32)) * b)
    x2 = x_ref[...] + jnp.dot(mixed.astype(BF16), wo_ref[...], preferred_element_type=F32)
    x2_ref[...] = x2
    ms = jnp.mean(x2 * x2, axis=-1, keepdims=True)
    h2 = x2 * lax.rsqrt(ms + NORM_EPS) * gain_ref[...]
    h2_ref[...] = h2
    hi = h2.astype(BF16)
    lo = (h2 - hi.astype(F32)).astype(BF16)
    lg = (jnp.dot(hi, wrh_ref[...], preferred_element_type=F32)
          + jnp.dot(lo, wrh_ref[...], preferred_element_type=F32)
          + jnp.dot(hi, wrl_ref[...], preferred_element_type=F32))
    lg_ref[...] = lg + br_ref[...]


def _mix(o_nsa, o_gla, big, x2d, wpn, wpg, wo, gain, wr_hi, wr_lo, br):
    n, d = x2d.shape
    tm = 256
    const = lambda i: (0, 0)
    return pl.pallas_call(
        _mix_kernel,
        grid=(n // tm,),
        in_specs=[
            pl.BlockSpec((tm, NSA_Q), lambda i: (i, 0)),
            pl.BlockSpec((tm, GLA_V), lambda i: (i, 0)),
            pl.BlockSpec((tm, d), lambda i: (i, 0)),
            pl.BlockSpec((tm, d), lambda i: (i, 1)),
            pl.BlockSpec((tm, d), lambda i: (i, 0)),
            pl.BlockSpec((NSA_Q, d), const, pipeline_mode=pl.Buffered(1)),
            pl.BlockSpec((GLA_V, d), const, pipeline_mode=pl.Buffered(1)),
            pl.BlockSpec((d, d), const, pipeline_mode=pl.Buffered(1)),
            pl.BlockSpec((1, d), const),
            pl.BlockSpec((d, LANES), const),
            pl.BlockSpec((d, LANES), const),
            pl.BlockSpec((1, LANES), const),
        ],
        out_specs=[
            pl.BlockSpec((tm, d), lambda i: (i, 0)),
            pl.BlockSpec((tm, d), lambda i: (i, 0)),
            pl.BlockSpec((tm, LANES), lambda i: (i, 0)),
        ],
        out_shape=[
            jax.ShapeDtypeStruct((n, d), F32),
            jax.ShapeDtypeStruct((n, d), F32),
            jax.ShapeDtypeStruct((n, LANES), F32),
        ],
        compiler_params=_cparams(("parallel",)),
        name="mix",
    )(o_nsa, o_gla, big, big, x2d, wpn, wpg, wo, gain, wr_hi, wr_lo, br)


def _route_kernel(lg_ref, tri_ref, idx_ref, w_ref, cnt_ref, carry_scr, *, n_exp):
    i = pl.program_id(0)
    tr = lg_ref.shape[0]

    @pl.when(i == 0)
    def _():
        carry_scr[...] = jnp.zeros_like(carry_scr)

    lane = lax.broadcasted_iota(jnp.int32, (tr, LANES), 1)
    work = jnp.where(lane < n_exp, lg_ref[...], -3e38)
    onehots, vals, idxs = [], [], []
    for _ in range(TOP_K):
        mval = jnp.max(work, axis=-1, keepdims=True)
        idx = jnp.min(jnp.where(work == mval, lane, LANES), axis=-1, keepdims=True)
        oh = lane == idx
        work = jnp.where(oh, -3e38, work)
        onehots.append(oh)
        vals.append(mval)
        idxs.append(idx)
    exps = [jnp.exp(v - vals[0]) for v in vals]
    den = exps[0] + exps[1] + exps[2] + exps[3]
    onehot = jnp.zeros((tr, LANES), F32)
    for oh in onehots:
        onehot = onehot + jnp.where(oh, 1.0, 0.0)
    before = jnp.dot(tri_ref[...], onehot.astype(BF16), preferred_element_type=F32) + carry_scr[...]
    carry_scr[...] = carry_scr[...] + jnp.sum(onehot, axis=0, keepdims=True)
    idx_out = jnp.zeros((tr, LANES), jnp.int32)
    w_out = jnp.zeros((tr, LANES), F32)
    for k in range(TOP_K):
        rank_k = jnp.sum(jnp.where(onehots[k], before, 0.0), axis=-1, keepdims=True)
        idx_out = jnp.where(lane == k, idxs[k], idx_out)
        idx_out = jnp.where(lane == TOP_K + k, rank_k.astype(jnp.int32), idx_out)
        w_out = jnp.where(lane == k, exps[k] / den, w_out)
    idx_ref[...] = idx_out
    w_ref[...] = w_out
    cnt_ref[...] = carry_scr[...]


def _route(logits, tri, n_exp):
    n = logits.shape[0]
    tr = tri.shape[0]
    kern = functools.partial(_route_kernel, n_exp=n_exp)
    return pl.pallas_call(
        kern,
        grid=(n // tr,),
        in_specs=[pl.BlockSpec((tr, LANES), lambda i: (i, 0)),
                  pl.BlockSpec((tr, tr), lambda i: (0, 0))],
        out_specs=[pl.BlockSpec((tr, LANES), lambda i: (i, 0)),
                   pl.BlockSpec((tr, LANES), lambda i: (i, 0)),
                   pl.BlockSpec((1, LANES), lambda i: (0, 0))],
        out_shape=[jax.ShapeDtypeStruct((n, LANES), jnp.int32),
                   jax.ShapeDtypeStruct((n, LANES), F32),
                   jax.ShapeDtypeStruct((1, LANES), F32)],
        scratch_shapes=[pltpu.VMEM((1, LANES), F32)],
        compiler_params=_cparams(("arbitrary",)),
        name="route",
    )(logits, tri)


def _dispatch_kernel(dest_ref, padlo_ref, padhi_ref, h_ref, xs_hbm, zbuf, sem, zsem, *, td, tb, n_exp):
    i = pl.program_id(0)
    n_blocks = xs_hbm.shape[0] // tb

    @pl.when(i == 0)
    def _():
        zbuf[...] = jnp.zeros_like(zbuf)

        def row_copy(s):
            return pltpu.make_async_copy(zbuf.at[pl.ds(0, 1)], xs_hbm.at[pl.ds(s, 1)], zsem)

        def per_expert(e, c):
            lo = padlo_ref[e]
            hi = padhi_ref[e]
            lax.fori_loop(lo, hi, lambda s, c2: (row_copy(s).start(), c2)[1], 0)
            lax.fori_loop(lo, hi, lambda s, c2: (row_copy(s).wait(), c2)[1], 0)
            return c

        lax.fori_loop(0, n_exp, per_expert, 0)

        def blk_copy(b):
            return pltpu.make_async_copy(zbuf, xs_hbm.at[pl.ds(b * tb, tb)], zsem)

        first_free = padhi_ref[n_exp - 1] // tb
        lax.fori_loop(first_free, n_blocks, lambda b, c: (blk_copy(b).start(), c)[1], 0)
        lax.fori_loop(first_free, n_blocks, lambda b, c: (blk_copy(b).wait(), c)[1], 0)

    base = i * td * TOP_K

    def start(t, c):
        for k in range(TOP_K):
            pltpu.make_async_copy(h_ref.at[pl.ds(t, 1)],
                                  xs_hbm.at[pl.ds(dest_ref[base + t * TOP_K + k], 1)],
                                  sem).start(priority=k % 2)
        return c

    lax.fori_loop(0, td, start, 0, unroll=2)
    for k in range(TOP_K):
        pltpu.make_async_copy(h_ref, xs_hbm.at[pl.ds(0, td)], sem).wait()


def _dispatch(dest, pad_lo, pad_hi, h2, n_slots, n_exp, tb):
    n, d = h2.shape
    td = 256
    kern = functools.partial(_dispatch_kernel, td=td, tb=tb, n_exp=n_exp)
    return pl.pallas_call(
        kern,
        grid_spec=pltpu.PrefetchScalarGridSpec(
            num_scalar_prefetch=3,
            grid=(n // td,),
            in_specs=[pl.BlockSpec((td, d), lambda i, dr, lo, hi: (i, 0))],
            out_specs=pl.BlockSpec(memory_space=pl.ANY),
            scratch_shapes=[pltpu.VMEM((tb, d), F32), pltpu.SemaphoreType.DMA, pltpu.SemaphoreType.DMA],
        ),
        out_shape=jax.ShapeDtypeStruct((n_slots, d), F32),
        compiler_params=_cparams(("arbitrary",)),
        name="dispatch",
    )(dest, pad_lo, pad_hi, h2)


def _expert_step(first_ref, count_ref, nxt_ref, w_copies, cast, in_copy, out_copy, compute,
                 zero_out, n_blocks):
    j = pl.program_id(0)
    e = pl.program_id(1)
    n_exp = pl.num_programs(1)
    fb = first_ref[e]
    nb = count_ref[e]
    first_e = nxt_ref[n_exp]

    @pl.when((j == 0) & (e == first_e))
    def _():
        for c in w_copies(e, 0):
            c.start(priority=1)

    @pl.when(nb > 0)
    def _():
        in_copy(fb, 0).start()
        for c in w_copies(e, j):
            c.wait()
        cast()
        ne = nxt_ref[e]

        @pl.when(ne < n_exp)
        def _():
            for c in w_copies(ne, j):
                c.start(priority=1)

        @pl.when((ne >= n_exp) & (j + 1 < pl.num_programs(0)))
        def _():
            for c in w_copies(first_e, j + 1):
                c.start(priority=1)

        def block(i, carry):
            slot = i % 2
            in_copy(fb + i, slot).wait()

            @pl.when(i + 1 < nb)
            def _():
                in_copy(fb + i + 1, 1 - slot).start()

            @pl.when(i >= 2)
            def _():
                out_copy(fb + i - 2, slot).wait()

            compute(slot)
            out_copy(fb + i, slot).start()
            return carry

        lax.fori_loop(0, nb, block, 0)

        @pl.when(nb >= 2)
        def _():
            out_copy(fb + nb - 2, nb % 2).wait()

        out_copy(fb + nb - 1, (nb - 1) % 2).wait()

    @pl.when(e == n_exp - 1)
    def _():
        used = first_ref[n_exp]
        zero_out()
        lax.fori_loop(used, n_blocks, lambda b, c: (out_copy(b, 0).start(), c)[1], 0)
        lax.fori_loop(used, n_blocks, lambda b, c: (out_copy(b, 0).wait(), c)[1], 0)


def _ffn_up_kernel(first_ref, count_ref, nxt_ref, xs_hbm, w_hbm, bg_ref, bu_ref, act_hbm,
                   wf32, wbf, xbuf, obuf, wsem, xsem, osem, *, tb, tf, ff):
    j = pl.program_id(0)

    def w_copies(e, jj):
        return [pltpu.make_async_copy(w_hbm.at[e, :, pl.ds(pl.multiple_of(h * ff + jj * tf, tf), tf)],
                                      wf32.at[h], wsem.at[h]) for h in range(2)]

    def cast():
        wbf[...] = wf32[...].astype(BF16)

    def in_copy(blk, slot):
        return pltpu.make_async_copy(xs_hbm.at[pl.ds(pl.multiple_of(blk * tb, tb), tb)],
                                     xbuf.at[slot], xsem.at[slot])

    def out_copy(blk, slot):
        return pltpu.make_async_copy(
            obuf.at[slot],
            act_hbm.at[pl.ds(pl.multiple_of(blk * tb, tb), tb), pl.ds(pl.multiple_of(j * tf, tf), tf)],
            osem.at[slot])

    def compute(slot):
        x = xbuf[slot].astype(BF16)
        gate = jnp.dot(x, wbf[0], preferred_element_type=F32) + bg_ref[...]
        up = jnp.dot(x, wbf[1], preferred_element_type=F32) + bu_ref[...]
        gate = jnp.minimum(gate, SWIGLU_LIMIT)
        up = jnp.clip(up, -SWIGLU_LIMIT, SWIGLU_LIMIT)
        obuf[slot] = ((up + 1.0) * gate * jax.nn.sigmoid(gate * SWIGLU_ALPHA)).astype(BF16)

    def zero_out():
        obuf[0] = jnp.zeros(obuf.shape[1:], BF16)

    _expert_step(first_ref, count_ref, nxt_ref, w_copies, cast, in_copy, out_copy, compute,
                 zero_out, act_hbm.shape[0] // tb)


def _ffn_up(first, count, nxt, xs, w_gate_up, b_gate_up, tb):
    n_slots, d = xs.shape
    n_exp, _, f2 = w_gate_up.shape
    ff = f2 // 2
    tf = min(1024, ff)
    nf = ff // tf
    kern = functools.partial(_ffn_up_kernel, tb=tb, tf=tf, ff=ff)
    return pl.pallas_call(
        kern,
        grid_spec=pltpu.PrefetchScalarGridSpec(
            num_scalar_prefetch=3,
            grid=(nf, n_exp),
            in_specs=[
                pl.BlockSpec(memory_space=pl.ANY),
                pl.BlockSpec(memory_space=pl.ANY),
                pl.BlockSpec((None, 1, tf), lambda j, e, fr, cn, nx: (e, 0, j)),
                pl.BlockSpec((None, 1, tf), lambda j, e, fr, cn, nx: (e, 0, nf + j)),
            ],
            out_specs=pl.BlockSpec(memory_space=pl.ANY),
            scratch_shapes=[pltpu.VMEM((2, d, tf), F32), pltpu.VMEM((2, d, tf), BF16),
                            pltpu.VMEM((2, tb, d), F32), pltpu.VMEM((2, tb, tf), BF16),
                            pltpu.SemaphoreType.DMA((2,)), pltpu.SemaphoreType.DMA((2,)),
                            pltpu.SemaphoreType.DMA((2,))],
        ),
        out_shape=jax.ShapeDtypeStruct((n_slots, ff), BF16),
        compiler_params=_cparams(("arbitrary", "arbitrary")),
        name="ffn_up",
    )(first, count, nxt, xs, w_gate_up, b_gate_up, b_gate_up)


def _ffn_down_kernel(first_ref, count_ref, nxt_ref, act_hbm, w_hbm, bd_ref, out_hbm,
                     wf32, wbf, abuf, obuf, wsem, asem, osem, *, tb, tn):
    j = pl.program_id(0)

    def w_copies(e, jj):
        return [pltpu.make_async_copy(w_hbm.at[e, :, pl.ds(pl.multiple_of(jj * tn, tn), tn)],
                                      wf32, wsem.at[0])]

    def cast():
        wbf[...] = wf32[...].astype(BF16)

    def in_copy(blk, slot):
        return pltpu.make_async_copy(act_hbm.at[pl.ds(pl.multiple_of(blk * tb, tb), tb)],
                                     abuf.at[slot], asem.at[slot])

    def out_copy(blk, slot):
        return pltpu.make_async_copy(
            obuf.at[slot],
            out_hbm.at[pl.ds(pl.multiple_of(blk * tb, tb), tb), pl.ds(pl.multiple_of(j * tn, tn), tn)],
            osem.at[slot])

    def compute(slot):
        obuf[slot] = jnp.dot(abuf[slot], wbf[...], preferred_element_type=F32) + bd_ref[...]

    def zero_out():
        obuf[0] = jnp.zeros(obuf.shape[1:], F32)

    _expert_step(first_ref, count_ref, nxt_ref, w_copies, cast, in_copy, out_copy, compute,
                 zero_out, out_hbm.shape[0] // tb)


def _ffn_down(first, count, nxt, act, w_down, b_down, tb):
    n_slots, ff = act.shape
    n_exp, _, d = w_down.shape
    tn = min(2048, d)
    nn = d // tn
    kern = functools.partial(_ffn_down_kernel, tb=tb, tn=tn)
    return pl.pallas_call(
        kern,
        grid_spec=pltpu.PrefetchScalarGridSpec(
            num_scalar_prefetch=3,
            grid=(nn, n_exp),
            in_specs=[
                pl.BlockSpec(memory_space=pl.ANY),
                pl.BlockSpec(memory_space=pl.ANY),
                pl.BlockSpec((None, 1, tn), lambda j, e, fr, cn, nx: (e, 0, j)),
            ],
            out_specs=pl.BlockSpec(memory_space=pl.ANY),
            scratch_shapes=[pltpu.VMEM((ff, tn), F32), pltpu.VMEM((ff, tn), BF16),
                            pltpu.VMEM((2, tb, ff), BF16), pltpu.VMEM((2, tb, tn), F32),
                            pltpu.SemaphoreType.DMA((1,)), pltpu.SemaphoreType.DMA((2,)),
                            pltpu.SemaphoreType.DMA((2,))],
        ),
        out_shape=jax.ShapeDtypeStruct((n_slots, d), F32),
        compiler_params=_cparams(("arbitrary", "arbitrary")),
        name="ffn_down",
    )(first, count, nxt, act, w_down, b_down)


def _combine_kernel(dest_ref, outs_hbm, w_ref, x2_ref, gain_ref, y_ref, buf, sem, *, tc):
    i = pl.program_id(0)
    slot = i % 2

    def gather(step, into):
        base = step * tc * TOP_K

        def start(t, c):
            for k in range(TOP_K):
                pltpu.make_async_copy(outs_hbm.at[pl.ds(dest_ref[base + t * TOP_K + k], 1)],
                                      buf.at[into, k, pl.ds(t, 1)], sem.at[into]).start(priority=k % 2)
            return c

        lax.fori_loop(0, tc, start, 0, unroll=2)

    @pl.when(i == 0)
    def _():
        gather(0, 0)

    @pl.when(i + 1 < pl.num_programs(0))
    def _():
        gather(i + 1, 1 - slot)

    for k in range(TOP_K):
        pltpu.make_async_copy(outs_hbm.at[pl.ds(0, tc)], buf.at[slot, k], sem.at[slot]).wait()
    w = w_ref[...]
    y = x2_ref[...]
    for k in range(TOP_K):
        y = y + w[:, k:k + 1] * buf[slot, k]
    ms = jnp.mean(y * y, axis=-1, keepdims=True)
    y_ref[...] = y * lax.rsqrt(ms + NORM_EPS) * gain_ref[...]


def _combine(dest, outs, w_top, x2, gain):
    n, d = x2.shape
    tc = 128
    kern = functools.partial(_combine_kernel, tc=tc)
    return pl.pallas_call(
        kern,
        grid_spec=pltpu.PrefetchScalarGridSpec(
            num_scalar_prefetch=1,
            grid=(n // tc,),
            in_specs=[
                pl.BlockSpec(memory_space=pl.ANY),
                pl.BlockSpec((tc, LANES), lambda i, dr: (i, 0)),
                pl.BlockSpec((tc, d), lambda i, dr: (i, 0)),
                pl.BlockSpec((1, d), lambda i, dr: (0, 0)),
            ],
            out_specs=pl.BlockSpec((tc, d), lambda i, dr: (i, 0)),
            scratch_shapes=[pltpu.VMEM((2, TOP_K, tc, d), F32), pltpu.SemaphoreType.DMA((2,))],
        ),
        out_shape=jax.ShapeDtypeStruct((n, d), F32),
        compiler_params=_cparams(("arbitrary",)),
        name="combine",
    )(dest, outs, w_top, x2, gain)


def _window_bias(tq):
    nv = WINDOW // tq
    r = np.arange(tq)[None, :, None]
    c = np.arange(tq + WINDOW)[None, None, :]
    v = np.arange(nv + 1)[:, None, None]
    q0 = v * tq
    kstart = np.maximum(q0 - WINDOW, 0)
    rel = (kstart + c) - (q0 + r)
    return np.where((rel <= 0) & (rel > -WINDOW), 0.0, NEG_INF).astype(np.float32)


def _rope_tables(seq):
    half = DH // 2
    inv_freq = np.float32(ROPE_THETA) ** (-np.arange(half, dtype=np.float32) / np.float32(half))
    ang = np.arange(seq, dtype=np.float32)[:, None] * inv_freq[None, :].astype(np.float32)
    cos = np.cos(ang.astype(np.float64)).astype(np.float32)
    sin = np.sin(ang.astype(np.float64)).astype(np.float32)
    return np.concatenate([cos, cos], axis=1), np.concatenate([-sin, sin], axis=1)


def _layer(x2d, batch, seq, norm_mix, w_in, cmp_pos_k, cmp_pos_v, w_cmp_k1, w_cmp_k2, w_cmp_v1,
           w_cmp_v2, w_gla_alpha, b_gla_alpha, gla_norm, w_proj_nsa, w_proj_gla, w_merge_gate,
           b_merge_gate, w_out, norm_moe, w_router, b_router, w_gate_up, b_gate_up, w_down, b_down,
           final_gain):
    n, d = x2d.shape
    n_exp = w_router.shape[1]
    d_merge = w_merge_gate.shape[1]
    main0 = d_merge

    o_g = NSA_Q + 6 * NSA_KV
    o_q = o_g + NSA_HEADS * 3
    o_a = o_q + 2 * GLA_QK + 2 * GLA_V
    w_main = jnp.concatenate([w_in[:, :o_g], w_in[:, o_q:o_a]], axis=1)
    w_all = jnp.concatenate([w_merge_gate, w_main], axis=1).astype(BF16)
    n_gl = NSA_HPG * 3
    small_parts = []
    for g in range(NSA_GROUPS):
        small_parts += [w_in[:, o_g + g * n_gl:o_g + (g + 1) * n_gl], jnp.zeros((d, LANES - n_gl), F32)]
    small_parts += [w_in[:, o_a:], jnp.zeros((d, LANES - GLA_RANK), F32)]
    w_small = jnp.concatenate(small_parts, axis=1).astype(BF16)
    bias_all = jnp.concatenate([b_merge_gate, jnp.zeros((MAIN_W,), F32)])[None, :]
    cos2, sin2 = _rope_tables(seq)

    big, small = _proj(x2d, norm_mix[None, :], w_all, w_small, bias_all, cos2, sin2, seq, d_merge)

    pos = jnp.stack([cmp_pos_k, cmp_pos_v])
    w1 = jnp.stack([w_cmp_k1, w_cmp_v1]).astype(BF16)
    w2 = jnp.stack([w_cmp_k2, w_cmp_v2]).astype(BF16)
    cmp_kv = _compress(big, pos, w1, w2, batch, seq, (main0 + NSA_Q) // DH)
    ncp = cmp_kv.shape[3]
    n_blk = seq // SEL_LEN
    cstart = np.arange(ncp)[:, None] * CMP_STRIDE
    blk = np.arange(LANES)[None, :]
    cover = ((cstart < (blk + 1) * SEL_LEN) & (cstart + CMP_LEN > blk * SEL_LEN)
             & (blk < n_blk) & (np.arange(ncp)[:, None] < seq // CMP_STRIDE - 1)).astype(np.float32)
    et = jnp.asarray((np.arange(seq)[:, None] // SEL_LEN) == np.arange(LANES)[None, :], BF16)
    o_nsa = _nsa(big, cmp_kv, small, cos2, sin2, cover, et, batch, seq, main0)

    tri_c = np.tril(np.ones((GLA_CHUNK, GLA_CHUNK), np.float32))
    w_alpha = jnp.concatenate([w_gla_alpha, jnp.zeros((LANES - GLA_RANK, GLA_QK), F32)], axis=0)
    o_gla = _gla(big, small, w_alpha, b_gla_alpha[None, :], gla_norm[None, :], tri_c,
                 batch, seq, main0)

    wr = jnp.concatenate([w_router, jnp.zeros((d, LANES - n_exp), F32)], axis=1)
    wr_hi = wr.astype(BF16)
    wr_lo = (wr - wr_hi.astype(F32)).astype(BF16)
    br = jnp.concatenate([b_router, jnp.zeros((LANES - n_exp,), F32)])[None, :]
    x2, h2, logits = _mix(o_nsa, o_gla, big, x2d, w_proj_nsa.astype(BF16), w_proj_gla.astype(BF16),
                          w_out.astype(BF16), norm_moe[None, :], wr_hi, wr_lo, br)

    tr = 256
    tri_r = jnp.asarray(np.arange(tr)[:, None] > np.arange(tr)[None, :], BF16)
    ridx, w_top, counts = _route(logits, tri_r, n_exp)

    tb = 256
    nk = n * TOP_K
    n_blocks = -(-nk // tb) + n_exp
    n_slots = n_blocks * tb
    cnt = counts[0, :n_exp].astype(jnp.int32)
    padded = (cnt + tb - 1) // tb * tb
    pad_end = jnp.cumsum(padded)
    pad_start = pad_end - padded
    top_e = ridx[:, :TOP_K]
    dest = (pad_start[top_e] + ridx[:, TOP_K:2 * TOP_K]).reshape(nk)
    first = jnp.concatenate([pad_start, pad_end[-1:]]) // tb
    count = padded // tb
    ids = jnp.arange(n_exp + 1, dtype=jnp.int32)
    cand = jnp.where(count > 0, ids[:n_exp], n_exp)
    later = ids[None, :n_exp] >= ids[:, None]
    after = jnp.concatenate([later[1:], later[:1]], axis=0)
    nxt = jnp.min(jnp.where(after, cand[None, :], n_exp), axis=1)

    xs = _dispatch(dest, pad_start + cnt, pad_end, h2, n_slots, n_exp, tb)
    act = _ffn_up(first, count, nxt, xs, w_gate_up, b_gate_up[:, None, :], tb)
    outs = _ffn_down(first, count, nxt, act, w_down, b_down[:, None, :], tb)
    return _combine(dest, outs, w_top, x2, final_gain)


def kernel(x, norm_mix, w_in, cmp_pos_k, cmp_pos_v, w_cmp_k1, w_cmp_k2, w_cmp_v1, w_cmp_v2,
           w_gla_alpha, b_gla_alpha, gla_norm, w_proj_nsa, w_proj_gla, w_merge_gate, b_merge_gate,
           w_out, norm_moe, w_router, b_router, w_gate_up, b_gate_up, w_down, b_down, norm_final):
    batch, seq, d = x.shape
    depth = w_in.shape[0]
    assert depth == 1, "the final norm is fused into the (single) layer's combine stage"
    y = _layer(x.reshape(batch * seq, d), batch, seq, norm_mix[0], w_in[0], cmp_pos_k[0],
               cmp_pos_v[0], w_cmp_k1[0], w_cmp_k2[0], w_cmp_v1[0], w_cmp_v2[0], w_gla_alpha[0],
               b_gla_alpha[0], gla_norm[0], w_proj_nsa[0], w_proj_gla[0], w_merge_gate[0],
               b_merge_gate[0], w_out[0], norm_moe[0], w_router[0], b_router[0], w_gate_up[0],
               b_gate_up[0], w_down[0], b_down[0], norm_final[None, :])
    return y.reshape(batch, seq, d)
```

```python
import functools

import jax
import jax.numpy as jnp
import numpy as np
from jax import lax
from jax.experimental import pallas as pl
from jax.experimental.pallas import tpu as pltpu

F32 = jnp.float32
BF16 = jnp.bfloat16

NORM_EPS = 1e-5
ROPE_THETA = 10000.0
NEG_INF = -1e30

NSA_HEADS = 8
NSA_GROUPS = 2
NSA_HPG = NSA_HEADS // NSA_GROUPS
DH = 128
CMP_LEN = 32
CMP_STRIDE = 16
SEL_LEN = 64
SEL_TOPK = 16
SEL_FORCE = 1e3
SEL_MASK = 2.0 ** 100
WINDOW = 512

GLA_HEADS = 4
GLA_DK = 128
GLA_DV = 256
GLA_RANK = 16
GLA_TAU = 16.0
GLA_CHUNK = 64
GLA_GROUP = 4

TOP_K = 4
SWIGLU_LIMIT = 7.0
SWIGLU_ALPHA = 1.702

LANES = 128
VMEM_LIMIT = 56 * 1024 * 1024

NSA_Q = NSA_HEADS * DH
NSA_KV = NSA_GROUPS * DH
GLA_QK = GLA_HEADS * GLA_DK
GLA_V = GLA_HEADS * GLA_DV
MAIN_W = NSA_Q + 6 * NSA_KV + 2 * GLA_QK + 2 * GLA_V
SMALL_W = (NSA_GROUPS + 1) * LANES


def _cparams(sem, vmem=VMEM_LIMIT):
    return pltpu.CompilerParams(dimension_semantics=sem, vmem_limit_bytes=vmem)


def _rope(x, cos, sin_signed):
    return x * cos + pltpu.roll(x, DH // 2, axis=1) * sin_signed


def _proj_kernel(x_ref, gain_ref, w_ref, wsmall_ref, bias_ref, cos_ref, sin_ref,
                 big_ref, small_ref, h_scr, *, n_merge_tiles, rope_tiles):
    j = pl.program_id(1)

    @pl.when(j == 0)
    def _():
        x = x_ref[...]
        ms = jnp.mean(x * x, axis=-1, keepdims=True)
        hb = (x * lax.rsqrt(ms + NORM_EPS) * gain_ref[...]).astype(BF16)
        h_scr[...] = hb
        small_ref[...] = jnp.dot(hb, wsmall_ref[...], preferred_element_type=F32)

    acc = jnp.dot(h_scr[...], w_ref[...], preferred_element_type=F32)
    is_merge = j < n_merge_tiles
    is_rope = (j == rope_tiles[0]) | (j == rope_tiles[1])

    @pl.when(is_merge)
    def _():
        big_ref[...] = (acc + bias_ref[...]).astype(BF16)

    @pl.when(is_rope)
    def _():
        cos = cos_ref[...]
        sin = sin_ref[...]
        for g in range(NSA_GROUPS):
            sl = slice(g * DH, (g + 1) * DH)
            big_ref[:, sl] = _rope(acc[:, sl], cos, sin).astype(BF16)
        big_ref[:, NSA_KV:] = acc[:, NSA_KV:].astype(BF16)

    @pl.when(jnp.logical_not(is_merge | is_rope))
    def _():
        big_ref[...] = acc.astype(BF16)


def _proj(x2d, gain, w_all, w_small, bias_all, cos2, sin2, seq, d_merge):
    n, d = x2d.shape
    width = w_all.shape[1]
    tm = min(1024, seq)
    tn = 512
    n_merge_tiles = d_merge // tn
    rope_tiles = (n_merge_tiles + 3, n_merge_tiles + 4)
    nsb = seq // tm
    kern = functools.partial(_proj_kernel, n_merge_tiles=n_merge_tiles, rope_tiles=rope_tiles)
    return pl.pallas_call(
        kern,
        grid=(n // tm, width // tn),
        in_specs=[
            pl.BlockSpec((tm, d), lambda i, j: (i, 0)),
            pl.BlockSpec((1, d), lambda i, j: (0, 0)),
            pl.BlockSpec((d, tn), lambda i, j: (0, j)),
            pl.BlockSpec((d, SMALL_W), lambda i, j: (0, 0)),
            pl.BlockSpec((1, tn), lambda i, j: (0, j)),
            pl.BlockSpec((tm, DH), lambda i, j: (i % nsb, 0)),
            pl.BlockSpec((tm, DH), lambda i, j: (i % nsb, 0)),
        ],
        out_specs=[
            pl.BlockSpec((tm, tn), lambda i, j: (i, j)),
            pl.BlockSpec((tm, SMALL_W), lambda i, j: (i, 0)),
        ],
        out_shape=[
            jax.ShapeDtypeStruct((n, width), BF16),
            jax.ShapeDtypeStruct((n, SMALL_W), F32),
        ],
        scratch_shapes=[pltpu.VMEM((tm, d), BF16)],
        compiler_params=_cparams(("parallel", "arbitrary")),
        name="proj",
    )(x2d, gain, w_all, w_small, bias_all, cos2, sin2)


def _compress_kernel(kv_ref, pos_ref, w1_ref, w2_ref, out_ref, scr, *, seq, ncp):
    nreal = seq // CMP_STRIDE
    scr[0:seq, :] = kv_ref[...].astype(F32)
    scr[seq:seq + CMP_LEN, :] = jnp.zeros((CMP_LEN, DH), F32)
    acc = jnp.zeros((nreal, w1_ref.shape[1]), F32)
    for l in range(CMP_LEN):
        a = scr[pl.ds(l, nreal, stride=CMP_STRIDE), :] + pos_ref[l:l + 1, :]
        acc = acc + jnp.dot(a.astype(BF16), w1_ref[l * DH:(l + 1) * DH, :],
                            preferred_element_type=F32)
    hid = jax.nn.gelu(acc)
    out = jnp.dot(hid.astype(BF16), w2_ref[...], preferred_element_type=F32)
    row = lax.broadcasted_iota(jnp.int32, out.shape, 0)
    out = jnp.where(row < nreal - 1, out, 0.0).astype(BF16)
    if ncp > nreal:
        out = jnp.concatenate([out, jnp.zeros((ncp - nreal, DH), BF16)], axis=0)
    out_ref[...] = out


def _compress(big, pos, w1, w2, batch, seq, col0):
    ncp = max(seq // CMP_STRIDE, LANES)
    kern = functools.partial(_compress_kernel, seq=seq, ncp=ncp)
    hid = w1.shape[2]
    return pl.pallas_call(
        kern,
        grid=(batch, NSA_GROUPS, 2),
        in_specs=[
            pl.BlockSpec((seq, DH), lambda b, g, t: (b, col0 + 2 * t + g)),
            pl.BlockSpec((None, CMP_LEN, DH), lambda b, g, t: (t, 0, 0)),
            pl.BlockSpec((None, CMP_LEN * DH, hid), lambda b, g, t: (t, 0, 0)),
            pl.BlockSpec((None, hid, DH), lambda b, g, t: (t, 0, 0)),
        ],
        out_specs=pl.BlockSpec((None, None, None, ncp, DH), lambda b, g, t: (b, g, t, 0, 0)),
        out_shape=jax.ShapeDtypeStruct((batch, NSA_GROUPS, 2, ncp, DH), BF16),
        scratch_shapes=[pltpu.VMEM((seq + CMP_LEN, DH), F32)],
        compiler_params=_cparams(("parallel", "parallel", "arbitrary")),
        name="compress",
    )(big, pos, w1, w2)


def _stack_heads(t):
    return jnp.concatenate([t[:, h * DH:(h + 1) * DH] for h in range(NSA_HPG)], axis=0)


def _nsa_kernel(q_ref, kc_ref, vc_ref, ks_ref, vs_ref, kw_ref, vw_ref, cos_ref, sin_ref,
                gate_ref, cover_ref, et_ref, wbias_ref, o_ref, kext_scr, vsext_scr, vwext_scr,
                *, seq, tq, ck):
    i = pl.program_id(2)
    q0 = i * tq
    scale = DH ** -0.5
    rows = NSA_HPG * tq
    ncp = kc_ref.shape[0]
    n_blk = seq // SEL_LEN
    n_sel = min(SEL_TOPK, n_blk)
    wspan = tq + WINDOW
    nt = (((1,), (1,)), ((), ()))

    @pl.when(i == 0)
    def _():
        ones = jnp.ones((seq, DH), BF16)
        kext_scr[:, :DH] = ks_ref[...]
        kext_scr[:, DH:] = et_ref[...]
        vsext_scr[:, :DH] = vs_ref[...]
        vsext_scr[:, DH:] = ones
        vwext_scr[:, :DH] = vw_ref[...]
        vwext_scr[:, DH:] = ones

    q = q_ref[...]
    qs = _stack_heads(q)
    pos_q = q0 + lax.broadcasted_iota(jnp.int32, (tq, 1), 0)
    pos_rows = jnp.concatenate([pos_q] * NSA_HPG, axis=0)

    cos = cos_ref[...]
    sin = sin_ref[...]
    qr = jnp.concatenate(
        [(_rope(q[:, h * DH:(h + 1) * DH].astype(F32), cos, sin) * scale).astype(BF16)
         for h in range(NSA_HPG)], axis=0)

    s = lax.dot_general(qs, kc_ref[...], nt, preferred_element_type=F32) * scale
    n_idx = lax.broadcasted_iota(jnp.int32, (1, ncp), 1)
    cmask = (n_idx * CMP_STRIDE + (CMP_LEN - 1)) <= pos_rows
    s = jnp.where(cmask, s, NEG_INF)
    m = jnp.max(s, axis=-1, keepdims=True)
    e = jnp.exp(s - m)
    p = jnp.where(cmask, e / jnp.sum(e, axis=-1, keepdims=True), 0.0)
    o_cmp = jnp.dot(p.astype(BF16), vc_ref[...], preferred_element_type=F32)

    psum = p[0:tq]
    for h in range(1, NSA_HPG):
        psum = psum + p[h * tq:(h + 1) * tq]
    imp = jnp.dot(psum, cover_ref[...], preferred_element_type=F32,
                  precision=lax.Precision.HIGHEST)
    blk = lax.broadcasted_iota(jnp.int32, (tq, LANES), 1)
    t_blk = pos_q // SEL_LEN
    forced = (blk == 0) | (blk == t_blk) | (blk == t_blk - 1)
    bonus = jnp.where(blk > t_blk, -SEL_FORCE, jnp.where(forced, SEL_FORCE, 0.0))
    val_t = (imp + bonus).T[:n_blk]
    blk_t = lax.broadcasted_iota(jnp.int32, (n_blk, tq), 0)
    terms = []
    for c in range(n_blk):
        vc = val_t[c:c + 1, :]
        beats = (vc > val_t) | ((vc == val_t) & (blk_t > c))
        terms.append(jnp.where(beats, 1.0, 0.0))
    while len(terms) > 1:
        terms = [a + b for a, b in zip(terms[0::2], terms[1::2])]
    rank = terms[0]
    pen_t = jnp.where(rank < n_sel, 0.0, -SEL_MASK)
    pen_t = jnp.concatenate([pen_t, jnp.zeros((LANES - n_blk, tq), F32)], axis=0)
    pen = pen_t.T.astype(BF16)

    q_ext = jnp.concatenate([qr, jnp.concatenate([pen] * NSA_HPG, axis=0)], axis=1)

    n_chunks = (q0 + tq + ck - 1) // ck

    q_heads = [q_ext[h * tq:(h + 1) * tq] for h in range(NSA_HPG)]

    def sel_chunk(c, carry, causal):
        k0 = pl.multiple_of(c * ck, ck)
        kblk = kext_scr[pl.ds(k0, ck), :]
        vblk = vsext_scr[pl.ds(k0, ck), :]
        scs = [lax.dot_general(q_heads[h], kblk, nt, preferred_element_type=F32)
               for h in range(NSA_HPG)]
        out = []
        for h in range(NSA_HPG):
            m_i, acc = carry[h]
            sc = scs[h]
            if causal:
                kp = k0 + lax.broadcasted_iota(jnp.int32, (1, ck), 1)
                sc = jnp.where(kp <= pos_q, sc, NEG_INF)
            m_new = jnp.maximum(m_i, jnp.max(sc, axis=-1, keepdims=True))
            alpha = jnp.exp(m_i - m_new)
            pc = jnp.exp(sc - m_new)
            acc = alpha * acc + jnp.dot(pc.astype(BF16), vblk, preferred_element_type=F32)
            out.append((m_new, acc))
        return tuple(out)

    carry = tuple((jnp.full((tq, 1), NEG_INF, F32), jnp.zeros((tq, 2 * DH), F32))
                  for _ in range(NSA_HPG))
    carry = lax.fori_loop(0, n_chunks - 1, lambda c, cr: sel_chunk(c, cr, False), carry)
    carry = sel_chunk(n_chunks - 1, carry, True)

    kstart = pl.multiple_of(jnp.maximum(q0 - WINDOW, 0), tq)
    kwin = kw_ref[pl.ds(kstart, wspan), :]
    vwin = vwext_scr[pl.ds(kstart, wspan), :]
    wbias = wbias_ref[...]
    sws = [lax.dot_general(qr[h * tq:(h + 1) * tq], kwin, nt, preferred_element_type=F32)
           for h in range(NSA_HPG)]

    gates = jax.nn.sigmoid(gate_ref[...])
    for h in range(NSA_HPG):
        acc_s = carry[h][1]
        o_sel = acc_s[:, :DH] / acc_s[:, DH:]
        sw = sws[h] + wbias
        ew = jnp.exp(sw - jnp.max(sw, axis=-1, keepdims=True))
        acc_w = jnp.dot(ew.astype(BF16), vwin, preferred_element_type=F32)
        o_win = acc_w[:, :DH] / acc_w[:, DH:]
        c0 = 3 * h
        o_h = (gates[:, c0:c0 + 1] * o_cmp[h * tq:(h + 1) * tq] + gates[:, c0 + 1:c0 + 2] * o_sel
               + gates[:, c0 + 2:c0 + 3] * o_win)
        o_ref[:, h * DH:(h + 1) * DH] = o_h.astype(BF16)


def _nsa(big, cmp_kv, small, cos2, sin2, cover, et, batch, seq, main0):
    tq = 256
    wbias = _window_bias(tq)
    ck = min(512, seq)
    nq = seq // tq
    ncp = cmp_kv.shape[3]
    wspan = tq + WINDOW
    n_wb = wbias.shape[0]
    kern = functools.partial(_nsa_kernel, seq=seq, tq=tq, ck=ck)
    kv_col = (main0 + NSA_Q) // DH

    def kvspec(which):
        return pl.BlockSpec((seq, DH), lambda b, g, i: (b, kv_col + 2 * which + g))

    return pl.pallas_call(
        kern,
        grid=(batch, NSA_GROUPS, nq),
        in_specs=[
            pl.BlockSpec((tq, NSA_HPG * DH), lambda b, g, i: (b * nq + i, main0 // (NSA_HPG * DH) + g)),
            pl.BlockSpec((None, None, None, ncp, DH), lambda b, g, i: (b, g, 0, 0, 0)),
            pl.BlockSpec((None, None, None, ncp, DH), lambda b, g, i: (b, g, 1, 0, 0)),
            kvspec(2), kvspec(3), kvspec(4), kvspec(5),
            pl.BlockSpec((tq, DH), lambda b, g, i: (i, 0)),
            pl.BlockSpec((tq, DH), lambda b, g, i: (i, 0)),
            pl.BlockSpec((tq, LANES), lambda b, g, i: (b * nq + i, g)),
            pl.BlockSpec((ncp, LANES), lambda b, g, i: (0, 0)),
            pl.BlockSpec((seq, LANES), lambda b, g, i: (0, 0)),
            pl.BlockSpec((None, tq, wspan), lambda b, g, i: (jnp.minimum(i, n_wb - 1), 0, 0)),
        ],
        out_specs=pl.BlockSpec((tq, NSA_HPG * DH), lambda b, g, i: (b * nq + i, g)),
        out_shape=jax.ShapeDtypeStruct((batch * seq, NSA_Q), BF16),
        scratch_shapes=[pltpu.VMEM((seq, 2 * DH), BF16), pltpu.VMEM((seq, 2 * DH), BF16),
                        pltpu.VMEM((seq, 2 * DH), BF16)],
        compiler_params=_cparams(("parallel", "parallel", "arbitrary")),
        name="nsa",
    )(big, cmp_kv, cmp_kv, big, big, big, big, cos2, sin2, small, cover, et, wbias)


def _gla_kernel(q_ref, k_ref, v_ref, r_ref, small_ref, wa_ref, ba_ref, gain_ref, tri_ref,
                o_ref, qt_scr, kt_scr, ks_scr, dec_scr, *, seq):
    C = GLA_CHUNK
    nc = seq // C
    GR = GLA_GROUP * C
    nt = (((1,), (1,)), ((), ()))
    z = jnp.dot(small_ref[...], wa_ref[...], preferred_element_type=F32,
                precision=lax.Precision.HIGHEST) + ba_ref[...]
    la = jax.nn.log_sigmoid(z) / GLA_TAU

    la_r = jnp.concatenate([la[c * C:(c + 1) * C, :] for c in range(nc)], axis=1)
    cum_r = jnp.dot(tri_ref[...], la_r, preferred_element_type=F32,
                    precision=lax.Precision.HIGHEST)
    last_r = cum_r[C - 1:C, :]
    qf_r = jnp.exp(cum_r)
    kf_r = jnp.exp(-cum_r)
    sf_r = jnp.exp(last_r - cum_r)
    dec_r = jnp.exp(last_r)
    qscale = GLA_DK ** -0.5
    for c in range(nc):
        rows = slice(c * C, (c + 1) * C)
        cols = slice(c * GLA_DK, (c + 1) * GLA_DK)
        qc = q_ref[rows, :].astype(F32) * qscale
        kc = k_ref[rows, :].astype(F32)
        qt_scr[rows, :] = (qc * qf_r[:, cols]).astype(BF16)
        kt_scr[rows, :] = (kc * kf_r[:, cols]).astype(BF16)
        ks_scr[rows, :] = (kc * sf_r[:, cols]).astype(BF16)
        dec_scr[c:c + 1, :] = dec_r[:, cols]

    ri = lax.broadcasted_iota(jnp.int32, (GR, GR), 0)
    ci = lax.broadcasted_iota(jnp.int32, (GR, GR), 1)
    mask = (ri >= ci) & ((ri // C) == (ci // C))

    def body(g, st):
        r0 = pl.multiple_of(g * GR, GR)
        qt = qt_scr[pl.ds(r0, GR), :]
        kt = kt_scr[pl.ds(r0, GR), :]
        ks = ks_scr[pl.ds(r0, GR), :]
        v = v_ref[pl.ds(r0, GR), :]
        attn = lax.dot_general(qt, kt, nt, preferred_element_type=F32)
        attn = jnp.where(mask, attn, 0.0)
        o_intra = jnp.dot(attn.astype(BF16), v, preferred_element_type=F32)
        outs = []
        for cc in range(GLA_GROUP):
            sl = slice(cc * C, (cc + 1) * C)
            outs.append(o_intra[sl] + lax.dot_general(qt[sl], st.astype(BF16), nt,
                                                      preferred_element_type=F32))
            d_st = lax.dot_general(v[sl], ks[sl], (((0,), (0,)), ((), ())),
                                   preferred_element_type=F32)
            st = st * dec_scr[pl.ds(g * GLA_GROUP + cc, 1), :] + d_st
        o = jnp.concatenate(outs, axis=0)
        o = o * lax.rsqrt(jnp.mean(o * o, axis=-1, keepdims=True) + NORM_EPS)
        o = o * gain_ref[...]
        rr = r_ref[pl.ds(r0, GR), :].astype(F32)
        o_ref[pl.ds(r0, GR), :] = (o * (rr * jax.nn.sigmoid(rr))).astype(BF16)
        return st

    lax.fori_loop(0, seq // GR, body, jnp.zeros((GLA_DV, GLA_DK), F32), unroll=2)


def _gla(big, small, w_alpha, b_alpha, gain, tri, batch, seq, main0):
    qcol = (main0 + NSA_Q + 6 * NSA_KV) // GLA_DK
    kcol = qcol + GLA_QK // GLA_DK
    vcol = (main0 + NSA_Q + 6 * NSA_KV + 2 * GLA_QK) // GLA_DV
    rcol = vcol + GLA_V // GLA_DV
    kern = functools.partial(_gla_kernel, seq=seq)
    return pl.pallas_call(
        kern,
        grid=(batch, GLA_HEADS),
        in_specs=[
            pl.BlockSpec((seq, GLA_DK), lambda b, h: (b, qcol + h)),
            pl.BlockSpec((seq, GLA_DK), lambda b, h: (b, kcol + h)),
            pl.BlockSpec((seq, GLA_DV), lambda b, h: (b, vcol + h)),
            pl.BlockSpec((seq, GLA_DV), lambda b, h: (b, rcol + h)),
            pl.BlockSpec((seq, LANES), lambda b, h: (b, NSA_GROUPS)),
            pl.BlockSpec((LANES, GLA_DK), lambda b, h: (0, h)),
            pl.BlockSpec((1, GLA_DK), lambda b, h: (0, h)),
            pl.BlockSpec((1, GLA_DV), lambda b, h: (0, h)),
            pl.BlockSpec((GLA_CHUNK, GLA_CHUNK), lambda b, h: (0, 0)),
        ],
        out_specs=pl.BlockSpec((seq, GLA_DV), lambda b, h: (b, h)),
        out_shape=jax.ShapeDtypeStruct((batch * seq, GLA_V), BF16),
        scratch_shapes=[pltpu.VMEM((seq, GLA_DK), BF16), pltpu.VMEM((seq, GLA_DK), BF16),
                        pltpu.VMEM((seq, GLA_DK), BF16), pltpu.VMEM((seq // GLA_CHUNK, GLA_DK), F32)],
        compiler_params=_cparams(("parallel", "parallel")),
        name="gla",
    )(big, big, big, big, small, w_alpha, b_alpha, gain, tri)


def _mix_kernel(on_ref, og_ref, ma_ref, mb_ref, x_ref, wpn_ref, wpg_ref, wo_ref, gain_ref,
                wrh_ref, wrl_ref, br_ref, x2_ref, h2_ref, lg_ref):
    a = jnp.dot(on_ref[...], wpn_ref[...], preferred_element_type=F32)
    b = jnp.dot(og_ref[...], wpg_ref[...], preferred_element_type=F32)
    mixed = (jax.nn.sigmoid(ma_ref[...].astype(F32)) * a
             + jax.nn.sigmoid(mb_ref[...].astype(F32)) * b)
    x2 = x_ref[...] + jnp.dot(mixed.astype(BF16), wo_ref[...], preferred_element_type=F32)
    x2_ref[...] = x2
    ms = jnp.mean(x2 * x2, axis=-1, keepdims=True)
    h2 = x2 * lax.rsqrt(ms + NORM_EPS) * gain_ref[...]
    h2_ref[...] = h2
    hi = h2.astype(BF16)
    lo = (h2 - hi.astype(F32)).astype(BF16)
    lg = (jnp.dot(hi, wrh_ref[...], preferred_element_type=F32)
          + jnp.dot(lo, wrh_ref[...], preferred_element_type=F32)
          + jnp.dot(hi, wrl_ref[...], preferred_element_type=F32))
    lg_ref[...] = lg + br_ref[...]


def _mix(o_nsa, o_gla, big, x2d, wpn, wpg, wo, gain, wr_hi, wr_lo, br):
    n, d = x2d.shape
    tm = 256
    const = lambda i: (0, 0)
    return pl.pallas_call(
        _mix_kernel,
        grid=(n // tm,),
        in_specs=[
            pl.BlockSpec((tm, NSA_Q), lambda i: (i, 0)),
            pl.BlockSpec((tm, GLA_V), lambda i: (i, 0)),
            pl.BlockSpec((tm, d), lambda i: (i, 0)),
            pl.BlockSpec((tm, d), lambda i: (i, 1)),
            pl.BlockSpec((tm, d), lambda i: (i, 0)),
            pl.BlockSpec((NSA_Q, d), const, pipeline_mode=pl.Buffered(1)),
            pl.BlockSpec((GLA_V, d), const, pipeline_mode=pl.Buffered(1)),
            pl.BlockSpec((d, d), const, pipeline_mode=pl.Buffered(1)),
            pl.BlockSpec((1, d), const),
            pl.BlockSpec((d, LANES), const),
            pl.BlockSpec((d, LANES), const),
            pl.BlockSpec((1, LANES), const),
        ],
        out_specs=[
            pl.BlockSpec((tm, d), lambda i: (i, 0)),
            pl.BlockSpec((tm, d), lambda i: (i, 0)),
            pl.BlockSpec((tm, LANES), lambda i: (i, 0)),
        ],
        out_shape=[
            jax.ShapeDtypeStruct((n, d), F32),
            jax.ShapeDtypeStruct((n, d), F32),
            jax.ShapeDtypeStruct((n, LANES), F32),
        ],
        compiler_params=_cparams(("parallel",)),
        name="mix",
    )(o_nsa, o_gla, big, big, x2d, wpn, wpg, wo, gain, wr_hi, wr_lo, br)


def _route_kernel(lg_ref, tri_ref, idx_ref, w_ref, cnt_ref, carry_scr, *, n_exp):
    i = pl.program_id(0)
    tr = lg_ref.shape[0]

    @pl.when(i == 0)
    def _():
        carry_scr[...] = jnp.zeros_like(carry_scr)

    lane = lax.broadcasted_iota(jnp.int32, (tr, LANES), 1)
    work = jnp.where(lane < n_exp, lg_ref[...], -3e38)
    onehots, vals, idxs = [], [], []
    for _ in range(TOP_K):
        mval = jnp.max(work, axis=-1, keepdims=True)
        idx = jnp.min(jnp.where(work == mval, lane, LANES), axis=-1, keepdims=True)
        oh = lane == idx
        work = jnp.where(oh, -3e38, work)
        onehots.append(oh)
        vals.append(mval)
        idxs.append(idx)
    exps = [jnp.exp(v - vals[0]) for v in vals]
    den = exps[0] + exps[1] + exps[2] + exps[3]
    onehot = jnp.zeros((tr, LANES), F32)
    for oh in onehots:
        onehot = onehot + jnp.where(oh, 1.0, 0.0)
    before = jnp.dot(tri_ref[...], onehot.astype(BF16), preferred_element_type=F32) + carry_scr[...]
    carry_scr[...] = carry_scr[...] + jnp.sum(onehot, axis=0, keepdims=True)
    idx_out = jnp.zeros((tr, LANES), jnp.int32)
    w_out = jnp.zeros((tr, LANES), F32)
    for k in range(TOP_K):
        rank_k = jnp.sum(jnp.where(onehots[k], before, 0.0), axis=-1, keepdims=True)
        idx_out = jnp.where(lane == k, idxs[k], idx_out)
        idx_out = jnp.where(lane == TOP_K + k, rank_k.astype(jnp.int32), idx_out)
        w_out = jnp.where(lane == k, exps[k] / den, w_out)
    idx_ref[...] = idx_out
    w_ref[...] = w_out
    cnt_ref[...] = carry_scr[...]


def _route(logits, tri, n_exp):
    n = logits.shape[0]
    tr = tri.shape[0]
    kern = functools.partial(_route_kernel, n_exp=n_exp)
    return pl.pallas_call(
        kern,
        grid=(n // tr,),
        in_specs=[pl.BlockSpec((tr, LANES), lambda i: (i, 0)),
                  pl.BlockSpec((tr, tr), lambda i: (0, 0))],
        out_specs=[pl.BlockSpec((tr, LANES), lambda i: (i, 0)),
                   pl.BlockSpec((tr, LANES), lambda i: (i, 0)),
                   pl.BlockSpec((1, LANES), lambda i: (0, 0))],
        out_shape=[jax.ShapeDtypeStruct((n, LANES), jnp.int32),
                   jax.ShapeDtypeStruct((n, LANES), F32),
                   jax.ShapeDtypeStruct((1, LANES), F32)],
        scratch_shapes=[pltpu.VMEM((1, LANES), F32)],
        compiler_params=_cparams(("arbitrary",)),
        name="route",
    )(logits, tri)


def _dispatch_kernel(dest_ref, padlo_ref, padhi_ref, h_ref, xs_hbm, zbuf, sem, zsem, *, td, tb, n_exp):
    i = pl.program_id(0)
    n_blocks = xs_hbm.shape[0] // tb

    @pl.when(i == 0)
    def _():
        zbuf[...] = jnp.zeros_like(zbuf)

        def row_copy(s):
            return pltpu.make_async_copy(zbuf.at[pl.ds(0, 1)], xs_hbm.at[pl.ds(s, 1)], zsem)

        def per_expert(e, c):
            lo = padlo_ref[e]
            hi = padhi_ref[e]
            lax.fori_loop(lo, hi, lambda s, c2: (row_copy(s).start(), c2)[1], 0)
            lax.fori_loop(lo, hi, lambda s, c2: (row_copy(s).wait(), c2)[1], 0)
            return c

        lax.fori_loop(0, n_exp, per_expert, 0)

        def blk_copy(b):
            return pltpu.make_async_copy(zbuf, xs_hbm.at[pl.ds(b * tb, tb)], zsem)

        first_free = padhi_ref[n_exp - 1] // tb
        lax.fori_loop(first_free, n_blocks, lambda b, c: (blk_copy(b).start(), c)[1], 0)
        lax.fori_loop(first_free, n_blocks, lambda b, c: (blk_copy(b).wait(), c)[1], 0)

    base = i * td * TOP_K

    def start(t, c):
        for k in range(TOP_K):
            pltpu.make_async_copy(h_ref.at[pl.ds(t, 1)],
                                  xs_hbm.at[pl.ds(dest_ref[base + t * TOP_K + k], 1)],
                                  sem).start(priority=k % 2)
        return c

    lax.fori_loop(0, td, start, 0, unroll=2)
    for k in range(TOP_K):
        pltpu.make_async_copy(h_ref, xs_hbm.at[pl.ds(0, td)], sem).wait()


def _dispatch(dest, pad_lo, pad_hi, h2, n_slots, n_exp, tb):
    n, d = h2.shape
    td = 256
    kern = functools.partial(_dispatch_kernel, td=td, tb=tb, n_exp=n_exp)
    return pl.pallas_call(
        kern,
        grid_spec=pltpu.PrefetchScalarGridSpec(
            num_scalar_prefetch=3,
            grid=(n // td,),
            in_specs=[pl.BlockSpec((td, d), lambda i, dr, lo, hi: (i, 0))],
            out_specs=pl.BlockSpec(memory_space=pl.ANY),
            scratch_shapes=[pltpu.VMEM((tb, d), F32), pltpu.SemaphoreType.DMA, pltpu.SemaphoreType.DMA],
        ),
        out_shape=jax.ShapeDtypeStruct((n_slots, d), F32),
        compiler_params=_cparams(("arbitrary",)),
        name="dispatch",
    )(dest, pad_lo, pad_hi, h2)


def _expert_step(first_ref, count_ref, nxt_ref, w_copies, cast, in_copy, out_copy, compute,
                 zero_out, n_blocks):
    j = pl.program_id(0)
    e = pl.program_id(1)
    n_exp = pl.num_programs(1)
    fb = first_ref[e]
    nb = count_ref[e]
    first_e = nxt_ref[n_exp]

    @pl.when((j == 0) & (e == first_e))
    def _():
        for c in w_copies(e, 0):
            c.start(priority=1)

    @pl.when(nb > 0)
    def _():
        in_copy(fb, 0).start()
        for c in w_copies(e, j):
            c.wait()
        cast()
        ne = nxt_ref[e]

        @pl.when(ne < n_exp)
        def _():
            for c in w_copies(ne, j):
                c.start(priority=1)

        @pl.when((ne >= n_exp) & (j + 1 < pl.num_programs(0)))
        def _():
            for c in w_copies(first_e, j + 1):
                c.start(priority=1)

        def block(i, carry):
            slot = i % 2
            in_copy(fb + i, slot).wait()

            @pl.when(i + 1 < nb)
            def _():
                in_copy(fb + i + 1, 1 - slot).start()

            @pl.when(i >= 2)
            def _():
                out_copy(fb + i - 2, slot).wait()

            compute(slot)
            out_copy(fb + i, slot).start()
            return carry

        lax.fori_loop(0, nb, block, 0)

        @pl.when(nb >= 2)
        def _():
            out_copy(fb + nb - 2, nb % 2).wait()

        out_copy(fb + nb - 1, (nb - 1) % 2).wait()

    @pl.when(e == n_exp - 1)
    def _():
        used = first_ref[n_exp]
        zero_out()
        lax.fori_loop(used, n_blocks, lambda b, c: (out_copy(b, 0).start(), c)[1], 0)
        lax.fori_loop(used, n_blocks, lambda b, c: (out_copy(b, 0).wait(), c)[1], 0)


def _ffn_up_kernel(first_ref, count_ref, nxt_ref, xs_hbm, w_hbm, bg_ref, bu_ref, act_hbm,
                   wf32, wbf, xbuf, obuf, wsem, xsem, osem, *, tb, tf, ff):
    j = pl.program_id(0)

    def w_copies(e, jj):
        return [pltpu.make_async_copy(w_hbm.at[e, :, pl.ds(pl.multiple_of(h * ff + jj * tf, tf), tf)],
                                      wf32.at[h], wsem.at[h]) for h in range(2)]

    def cast():
        wbf[...] = wf32[...].astype(BF16)

    def in_copy(blk, slot):
        return pltpu.make_async_copy(xs_hbm.at[pl.ds(pl.multiple_of(blk * tb, tb), tb)],
                                     xbuf.at[slot], xsem.at[slot])

    def out_copy(blk, slot):
        return pltpu.make_async_copy(
            obuf.at[slot],
            act_hbm.at[pl.ds(pl.multiple_of(blk * tb, tb), tb), pl.ds(pl.multiple_of(j * tf, tf), tf)],
            osem.at[slot])

    def compute(slot):
        x = xbuf[slot].astype(BF16)
        gate = jnp.dot(x, wbf[0], preferred_element_type=F32) + bg_ref[...]
        up = jnp.dot(x, wbf[1], preferred_element_type=F32) + bu_ref[...]
        gate = jnp.minimum(gate, SWIGLU_LIMIT)
        up = jnp.clip(up, -SWIGLU_LIMIT, SWIGLU_LIMIT)
        obuf[slot] = ((up + 1.0) * gate * jax.nn.sigmoid(gate * SWIGLU_ALPHA)).astype(BF16)

    def zero_out():
        obuf[0] = jnp.zeros(obuf.shape[1:], BF16)

    _expert_step(first_ref, count_ref, nxt_ref, w_copies, cast, in_copy, out_copy, compute,
                 zero_out, act_hbm.shape[0] // tb)


def _ffn_up(first, count, nxt, xs, w_gate_up, b_gate_up, tb):
    n_slots, d = xs.shape
    n_exp, _, f2 = w_gate_up.shape
    ff = f2 // 2
    tf = min(1024, ff)
    nf = ff // tf
    kern = functools.partial(_ffn_up_kernel, tb=tb, tf=tf, ff=ff)
    return pl.pallas_call(
        kern,
        grid_spec=pltpu.PrefetchScalarGridSpec(
            num_scalar_prefetch=3,
            grid=(nf, n_exp),
            in_specs=[
                pl.BlockSpec(memory_space=pl.ANY),
                pl.BlockSpec(memory_space=pl.ANY),
                pl.BlockSpec((None, 1, tf), lambda j, e, fr, cn, nx: (e, 0, j)),
                pl.BlockSpec((None, 1, tf), lambda j, e, fr, cn, nx: (e, 0, nf + j)),
            ],
            out_specs=pl.BlockSpec(memory_space=pl.ANY),
            scratch_shapes=[pltpu.VMEM((2, d, tf), F32), pltpu.VMEM((2, d, tf), BF16),
                            pltpu.VMEM((2, tb, d), F32), pltpu.VMEM((2, tb, tf), BF16),
                            pltpu.SemaphoreType.DMA((2,)), pltpu.SemaphoreType.DMA((2,)),
                            pltpu.SemaphoreType.DMA((2,))],
        ),
        out_shape=jax.ShapeDtypeStruct((n_slots, ff), BF16),
        compiler_params=_cparams(("arbitrary", "arbitrary")),
        name="ffn_up",
    )(first, count, nxt, xs, w_gate_up, b_gate_up, b_gate_up)


def _ffn_down_kernel(first_ref, count_ref, nxt_ref, act_hbm, w_hbm, bd_ref, out_hbm,
                     wf32, wbf, abuf, obuf, wsem, asem, osem, *, tb, tn):
    j = pl.program_id(0)

    def w_copies(e, jj):
        return [pltpu.make_async_copy(w_hbm.at[e, :, pl.ds(pl.multiple_of(jj * tn, tn), tn)],
                                      wf32, wsem.at[0])]

    def cast():
        wbf[...] = wf32[...].astype(BF16)

    def in_copy(blk, slot):
        return pltpu.make_async_copy(act_hbm.at[pl.ds(pl.multiple_of(blk * tb, tb), tb)],
                                     abuf.at[slot], asem.at[slot])

    def out_copy(blk, slot):
        return pltpu.make_async_copy(
            obuf.at[slot],
            out_hbm.at[pl.ds(pl.multiple_of(blk * tb, tb), tb), pl.ds(pl.multiple_of(j * tn, tn), tn)],
            osem.at[slot])

    def compute(slot):
        obuf[slot] = jnp.dot(abuf[slot], wbf[...], preferred_element_type=F32) + bd_ref[...]

    def zero_out():
        obuf[0] = jnp.zeros(obuf.shape[1:], F32)

    _expert_step(first_ref, count_ref, nxt_ref, w_copies, cast, in_copy, out_copy, compute,
                 zero_out, out_hbm.shape[0] // tb)


def _ffn_down(first, count, nxt, act, w_down, b_down, tb):
    n_slots, ff = act.shape
    n_exp, _, d = w_down.shape
    tn = min(2048, d)
    nn = d // tn
    kern = functools.partial(_ffn_down_kernel, tb=tb, tn=tn)
    return pl.pallas_call(
        kern,
        grid_spec=pltpu.PrefetchScalarGridSpec(
            num_scalar_prefetch=3,
            grid=(nn, n_exp),
            in_specs=[
                pl.BlockSpec(memory_space=pl.ANY),
                pl.BlockSpec(memory_space=pl.ANY),
                pl.BlockSpec((None, 1, tn), lambda j, e, fr, cn, nx: (e, 0, j)),
            ],
            out_specs=pl.BlockSpec(memory_space=pl.ANY),
            scratch_shapes=[pltpu.VMEM((ff, tn), F32), pltpu.VMEM((ff, tn), BF16),
                            pltpu.VMEM((2, tb, ff), BF16), pltpu.VMEM((2, tb, tn), F32),
                            pltpu.SemaphoreType.DMA((1,)), pltpu.SemaphoreType.DMA((2,)),
                            pltpu.SemaphoreType.DMA((2,))],
        ),
        out_shape=jax.ShapeDtypeStruct((n_slots, d), F32),
        compiler_params=_cparams(("arbitrary", "arbitrary")),
        name="ffn_down",
    )(first, count, nxt, act, w_down, b_down)


def _combine_kernel(dest_ref, outs_hbm, w_ref, x2_ref, gain_ref, y_ref, buf, sem, *, tc):
    i = pl.program_id(0)
    slot = i % 2

    def gather(step, into):
        base = step * tc * TOP_K

        def start(t, c):
            for k in range(TOP_K):
                pltpu.make_async_copy(outs_hbm.at[pl.ds(dest_ref[base + t * TOP_K + k], 1)],
                                      buf.at[into, k, pl.ds(t, 1)], sem.at[into]).start(priority=k % 2)
            return c

        lax.fori_loop(0, tc, start, 0, unroll=2)

    @pl.when(i == 0)
    def _():
        gather(0, 0)

    @pl.when(i + 1 < pl.num_programs(0))
    def _():
        gather(i + 1, 1 - slot)

    for k in range(TOP_K):
        pltpu.make_async_copy(outs_hbm.at[pl.ds(0, tc)], buf.at[slot, k], sem.at[slot]).wait()
    w = w_ref[...]
    y = x2_ref[...]
    for k in range(TOP_K):
        y = y + w[:, k:k + 1] * buf[slot, k]
    ms = jnp.mean(y * y, axis=-1, keepdims=True)
    y_ref[...] = y * lax.rsqrt(ms + NORM_EPS) * gain_ref[...]


def _combine(dest, outs, w_top, x2, gain):
    n, d = x2.shape
    tc = 128
    kern = functools.partial(_combine_kernel, tc=tc)
    return pl.pallas_call(
        kern,
        grid_spec=pltpu.PrefetchScalarGridSpec(
            num_scalar_prefetch=1,
            grid=(n // tc,),
            in_specs=[
                pl.BlockSpec(memory_space=pl.ANY),
                pl.BlockSpec((tc, LANES), lambda i, dr: (i, 0)),
                pl.BlockSpec((tc, d), lambda i, dr: (i, 0)),
                pl.BlockSpec((1, d), lambda i, dr: (0, 0)),
            ],
            out_specs=pl.BlockSpec((tc, d), lambda i, dr: (i, 0)),
            scratch_shapes=[pltpu.VMEM((2, TOP_K, tc, d), F32), pltpu.SemaphoreType.DMA((2,))],
        ),
        out_shape=jax.ShapeDtypeStruct((n, d), F32),
        compiler_params=_cparams(("arbitrary",)),
        name="combine",
    )(dest, outs, w_top, x2, gain)


def _window_bias(tq):
    nv = WINDOW // tq
    r = np.arange(tq)[None, :, None]
    c = np.arange(tq + WINDOW)[None, None, :]
    v = np.arange(nv + 1)[:, None, None]
    q0 = v * tq
    kstart = np.maximum(q0 - WINDOW, 0)
    rel = (kstart + c) - (q0 + r)
    return np.where((rel <= 0) & (rel > -WINDOW), 0.0, NEG_INF).astype(np.float32)


def _rope_tables(seq):
    half = DH // 2
    inv_freq = np.float32(ROPE_THETA) ** (-np.arange(half, dtype=np.float32) / np.float32(half))
    ang = np.arange(seq, dtype=np.float32)[:, None] * inv_freq[None, :].astype(np.float32)
    cos = np.cos(ang.astype(np.float64)).astype(np.float32)
    sin = np.sin(ang.astype(np.float64)).astype(np.float32)
    return np.concatenate([cos, cos], axis=1), np.concatenate([-sin, sin], axis=1)


def _layer(x2d, batch, seq, norm_mix, w_in, cmp_pos_k, cmp_pos_v, w_cmp_k1, w_cmp_k2, w_cmp_v1,
           w_cmp_v2, w_gla_alpha, b_gla_alpha, gla_norm, w_proj_nsa, w_proj_gla, w_merge_gate,
           b_merge_gate, w_out, norm_moe, w_router, b_router, w_gate_up, b_gate_up, w_down, b_down,
           final_gain):
    n, d = x2d.shape
    n_exp = w_router.shape[1]
    d_merge = w_merge_gate.shape[1]
    main0 = d_merge

    o_g = NSA_Q + 6 * NSA_KV
    o_q = o_g + NSA_HEADS * 3
    o_a = o_q + 2 * GLA_QK + 2 * GLA_V
    w_main = jnp.concatenate([w_in[:, :o_g], w_in[:, o_q:o_a]], axis=1)
    w_all = jnp.concatenate([w_merge_gate, w_main], axis=1).astype(BF16)
    n_gl = NSA_HPG * 3
    small_parts = []
    for g in range(NSA_GROUPS):
        small_parts += [w_in[:, o_g + g * n_gl:o_g + (g + 1) * n_gl], jnp.zeros((d, LANES - n_gl), F32)]
    small_parts += [w_in[:, o_a:], jnp.zeros((d, LANES - GLA_RANK), F32)]
    w_small = jnp.concatenate(small_parts, axis=1).astype(BF16)
    bias_all = jnp.concatenate([b_merge_gate, jnp.zeros((MAIN_W,), F32)])[None, :]
    cos2, sin2 = _rope_tables(seq)

    big, small = _proj(x2d, norm_mix[None, :], w_all, w_small, bias_all, cos2, sin2, seq, d_merge)

    pos = jnp.stack([cmp_pos_k, cmp_pos_v])
    w1 = jnp.stack([w_cmp_k1, w_cmp_v1]).astype(BF16)
    w2 = jnp.stack([w_cmp_k2, w_cmp_v2]).astype(BF16)
    cmp_kv = _compress(big, pos, w1, w2, batch, seq, (main0 + NSA_Q) // DH)
    ncp = cmp_kv.shape[3]
    n_blk = seq // SEL_LEN
    cstart = np.arange(ncp)[:, None] * CMP_STRIDE
    blk = np.arange(LANES)[None, :]
    cover = ((cstart < (blk + 1) * SEL_LEN) & (cstart + CMP_LEN > blk * SEL_LEN)
             & (blk < n_blk) & (np.arange(ncp)[:, None] < seq // CMP_STRIDE - 1)).astype(np.float32)
    et = jnp.asarray((np.arange(seq)[:, None] // SEL_LEN) == np.arange(LANES)[None, :], BF16)
    o_nsa = _nsa(big, cmp_kv, small, cos2, sin2, cover, et, batch, seq, main0)

    tri_c = np.tril(np.ones((GLA_CHUNK, GLA_CHUNK), np.float32))
    w_alpha = jnp.concatenate([w_gla_alpha, jnp.zeros((LANES - GLA_RANK, GLA_QK), F32)], axis=0)
    o_gla = _gla(big, small, w_alpha, b_gla_alpha[None, :], gla_norm[None, :], tri_c,
                 batch, seq, main0)

    wr = jnp.concatenate([w_router, jnp.zeros((d, LANES - n_exp), F32)], axis=1)
    wr_hi = wr.astype(BF16)
    wr_lo = (wr - wr_hi.astype(F32)).astype(BF16)
    br = jnp.concatenate([b_router, jnp.zeros((LANES - n_exp,), F32)])[None, :]
    x2, h2, logits = _mix(o_nsa, o_gla, big, x2d, w_proj_nsa.astype(BF16), w_proj_gla.astype(BF16),
                          w_out.astype(BF16), norm_moe[None, :], wr_hi, wr_lo, br)

    tr = 256
    tri_r = jnp.asarray(np.arange(tr)[:, None] > np.arange(tr)[None, :], BF16)
    ridx, w_top, counts = _route(logits, tri_r, n_exp)

    tb = 256
    nk = n * TOP_K
    n_blocks = -(-nk // tb) + n_exp
    n_slots = n_blocks * tb
    cnt = counts[0, :n_exp].astype(jnp.int32)
    padded = (cnt + tb - 1) // tb * tb
    pad_end = jnp.cumsum(padded)
    pad_start = pad_end - padded
    top_e = ridx[:, :TOP_K]
    e_ids = jnp.arange(n_exp, dtype=jnp.int32)
    start_of = jnp.sum(jnp.where(top_e[:, :, None] == e_ids, pad_start[None, None, :], 0), axis=-1)
    dest = (start_of + ridx[:, TOP_K:2 * TOP_K]).reshape(nk)
    first = jnp.concatenate([pad_start, pad_end[-1:]]) // tb
    count = padded // tb
    ids = jnp.arange(n_exp + 1, dtype=jnp.int32)
    cand = jnp.where(count > 0, ids[:n_exp], n_exp)
    later = ids[None, :n_exp] >= ids[:, None]
    after = jnp.concatenate([later[1:], later[:1]], axis=0)
    nxt = jnp.min(jnp.where(after, cand[None, :], n_exp), axis=1)

    xs = _dispatch(dest, pad_start + cnt, pad_end, h2, n_slots, n_exp, tb)
    act = _ffn_up(first, count, nxt, xs, w_gate_up, b_gate_up[:, None, :], tb)
    outs = _ffn_down(first, count, nxt, act, w_down, b_down[:, None, :], tb)
    return _combine(dest, outs, w_top, x2, final_gain)


def kernel(x, norm_mix, w_in, cmp_pos_k, cmp_pos_v, w_cmp_k1, w_cmp_k2, w_cmp_v1, w_cmp_v2,
           w_gla_alpha, b_gla_alpha, gla_norm, w_proj_nsa, w_proj_gla, w_merge_gate, b_merge_gate,
           w_out, norm_moe, w_router, b_router, w_gate_up, b_gate_up, w_down, b_down, norm_final):
    batch, seq, d = x.shape
    depth = w_in.shape[0]
    assert depth == 1, "the final norm is fused into the (single) layer's combine stage"
    y = _layer(x.reshape(batch * seq, d), batch, seq, norm_mix[0], w_in[0], cmp_pos_k[0],
               cmp_pos_v[0], w_cmp_k1[0], w_cmp_k2[0], w_cmp_v1[0], w_cmp_v2[0], w_gla_alpha[0],
               b_gla_alpha[0], gla_norm[0], w_proj_nsa[0], w_proj_gla[0], w_merge_gate[0],
               b_merge_gate[0], w_out[0], norm_moe[0], w_router[0], b_router[0], w_gate_up[0],
               b_gate_up[0], w_down[0], b_down[0], norm_final[None, :])
    return y.reshape(batch, seq, d)
```

```python
import functools

import jax
import jax.numpy as jnp
import numpy as np
from jax import lax
from jax.experimental import pallas as pl
from jax.experimental.pallas import tpu as pltpu

F32 = jnp.float32
BF16 = jnp.bfloat16

NORM_EPS = 1e-5
ROPE_THETA = 10000.0
NEG_INF = -1e30

NSA_HEADS = 8
NSA_GROUPS = 2
NSA_HPG = NSA_HEADS // NSA_GROUPS
DH = 128
CMP_LEN = 32
CMP_STRIDE = 16
SEL_LEN = 64
SEL_TOPK = 16
SEL_FORCE = 1e3
SEL_MASK = 2.0 ** 100
WINDOW = 512

GLA_HEADS = 4
GLA_DK = 128
GLA_DV = 256
GLA_RANK = 16
GLA_TAU = 16.0
GLA_CHUNK = 64
GLA_GROUP = 4

TOP_K = 4
SWIGLU_LIMIT = 7.0
SWIGLU_ALPHA = 1.702

LANES = 128
VMEM_LIMIT = 56 * 1024 * 1024

NSA_Q = NSA_HEADS * DH
NSA_KV = NSA_GROUPS * DH
GLA_QK = GLA_HEADS * GLA_DK
GLA_V = GLA_HEADS * GLA_DV
MAIN_W = NSA_Q + 6 * NSA_KV + 2 * GLA_QK + 2 * GLA_V
SMALL_W = (NSA_GROUPS + 1) * LANES


def _cparams(sem, vmem=VMEM_LIMIT):
    return pltpu.CompilerParams(dimension_semantics=sem, vmem_limit_bytes=vmem)


def _rope(x, cos, sin_signed):
    return x * cos + pltpu.roll(x, DH // 2, axis=1) * sin_signed


def _proj_kernel(x_ref, gain_ref, w_ref, wsmall_ref, bias_ref, cos_ref, sin_ref,
                 big_ref, small_ref, h_scr, *, rope_tiles):
    j = pl.program_id(1)

    @pl.when(j == 0)
    def _():
        x = x_ref[...]
        ms = jnp.mean(x * x, axis=-1, keepdims=True)
        hb = (x * lax.rsqrt(ms + NORM_EPS) * gain_ref[...]).astype(BF16)
        h_scr[...] = hb
        small_ref[...] = jnp.dot(hb, wsmall_ref[...], preferred_element_type=F32)

    half = w_ref.shape[1] // 2
    h = h_scr[...]
    is_rope = (j == rope_tiles[0]) | (j == rope_tiles[1])
    acc0 = jnp.dot(h, w_ref[:, :half], preferred_element_type=F32) + bias_ref[:, :half]
    acc1 = jnp.dot(h, w_ref[:, half:], preferred_element_type=F32) + bias_ref[:, half:]
    cos = cos_ref[...]
    sin = sin_ref[...]
    roped = jnp.concatenate([_rope(acc0[:, g * DH:(g + 1) * DH], cos, sin)
                             for g in range(NSA_GROUPS)], axis=1)
    big_ref[:, :half] = jnp.where(is_rope, roped, acc0).astype(BF16)
    big_ref[:, half:] = acc1.astype(BF16)


def _proj(x2d, gain, w_all, w_small, bias_all, cos2, sin2, seq, d_merge):
    n, d = x2d.shape
    width = w_all.shape[1]
    tm = min(1024, seq)
    tn = 512
    n_merge_tiles = d_merge // tn
    rope_tiles = (n_merge_tiles + 3, n_merge_tiles + 4)
    nsb = seq // tm
    kern = functools.partial(_proj_kernel, rope_tiles=rope_tiles)
    return pl.pallas_call(
        kern,
        grid=(n // tm, width // tn),
        in_specs=[
            pl.BlockSpec((tm, d), lambda i, j: (i, 0)),
            pl.BlockSpec((1, d), lambda i, j: (0, 0)),
            pl.BlockSpec((d, tn), lambda i, j: (0, j)),
            pl.BlockSpec((d, SMALL_W), lambda i, j: (0, 0)),
            pl.BlockSpec((1, tn), lambda i, j: (0, j)),
            pl.BlockSpec((tm, DH), lambda i, j: (i % nsb, 0)),
            pl.BlockSpec((tm, DH), lambda i, j: (i % nsb, 0)),
        ],
        out_specs=[
            pl.BlockSpec((tm, tn), lambda i, j: (i, j)),
            pl.BlockSpec((tm, SMALL_W), lambda i, j: (i, 0)),
        ],
        out_shape=[
            jax.ShapeDtypeStruct((n, width), BF16),
            jax.ShapeDtypeStruct((n, SMALL_W), F32),
        ],
        scratch_shapes=[pltpu.VMEM((tm, d), BF16)],
        compiler_params=_cparams(("parallel", "arbitrary")),
        name="proj",
    )(x2d, gain, w_all, w_small, bias_all, cos2, sin2)


def _compress_kernel(kv_ref, pos_ref, w1_ref, w2_ref, out_ref, scr, *, seq, ncp):
    nreal = seq // CMP_STRIDE
    scr[0:seq, :] = kv_ref[...].astype(F32)
    scr[seq:seq + CMP_LEN, :] = jnp.zeros((CMP_LEN, DH), F32)
    acc = jnp.zeros((nreal, w1_ref.shape[1]), F32)
    for l in range(CMP_LEN):
        a = scr[pl.ds(l, nreal, stride=CMP_STRIDE), :] + pos_ref[l:l + 1, :]
        acc = acc + jnp.dot(a.astype(BF16), w1_ref[l * DH:(l + 1) * DH, :],
                            preferred_element_type=F32)
    hid = jax.nn.gelu(acc)
    out = jnp.dot(hid.astype(BF16), w2_ref[...], preferred_element_type=F32)
    row = lax.broadcasted_iota(jnp.int32, out.shape, 0)
    out = jnp.where(row < nreal - 1, out, 0.0).astype(BF16)
    if ncp > nreal:
        out = jnp.concatenate([out, jnp.zeros((ncp - nreal, DH), BF16)], axis=0)
    out_ref[...] = out


def _compress(big, pos, w1, w2, batch, seq, col0):
    ncp = max(seq // CMP_STRIDE, LANES)
    kern = functools.partial(_compress_kernel, seq=seq, ncp=ncp)
    hid = w1.shape[2]
    return pl.pallas_call(
        kern,
        grid=(batch, NSA_GROUPS, 2),
        in_specs=[
            pl.BlockSpec((seq, DH), lambda b, g, t: (b, col0 + 2 * t + g)),
            pl.BlockSpec((None, CMP_LEN, DH), lambda b, g, t: (t, 0, 0)),
            pl.BlockSpec((None, CMP_LEN * DH, hid), lambda b, g, t: (t, 0, 0)),
            pl.BlockSpec((None, hid, DH), lambda b, g, t: (t, 0, 0)),
        ],
        out_specs=pl.BlockSpec((None, None, None, ncp, DH), lambda b, g, t: (b, g, t, 0, 0)),
        out_shape=jax.ShapeDtypeStruct((batch, NSA_GROUPS, 2, ncp, DH), BF16),
        scratch_shapes=[pltpu.VMEM((seq + CMP_LEN, DH), F32)],
        compiler_params=_cparams(("parallel", "parallel", "arbitrary")),
        name="compress",
    )(big, pos, w1, w2)


def _stack_heads(t):
    return jnp.concatenate([t[:, h * DH:(h + 1) * DH] for h in range(NSA_HPG)], axis=0)


def _nsa_kernel(q_ref, kc_ref, vc_ref, ks_ref, vs_ref, kw_ref, vw_ref, cos_ref, sin_ref,
                gate_ref, cover_ref, et_ref, wbias_ref, o_ref, kext_scr, vsext_scr, vwext_scr,
                *, seq, tq, ck):
    i = pl.program_id(2)
    q0 = i * tq
    scale = DH ** -0.5
    rows = NSA_HPG * tq
    ncp = kc_ref.shape[0]
    n_blk = seq // SEL_LEN
    n_sel = min(SEL_TOPK, n_blk)
    wspan = tq + WINDOW
    nt = (((1,), (1,)), ((), ()))

    @pl.when(i == 0)
    def _():
        ones = jnp.ones((seq, DH), BF16)
        kext_scr[:, :DH] = ks_ref[...]
        kext_scr[:, DH:] = et_ref[...]
        vsext_scr[:, :DH] = vs_ref[...]
        vsext_scr[:, DH:] = ones
        vwext_scr[:, :DH] = vw_ref[...]
        vwext_scr[:, DH:] = ones

    q = q_ref[...]
    qs = _stack_heads(q)
    pos_q = q0 + lax.broadcasted_iota(jnp.int32, (tq, 1), 0)
    pos_rows = jnp.concatenate([pos_q] * NSA_HPG, axis=0)

    cos = cos_ref[...]
    sin = sin_ref[...]
    qr = jnp.concatenate(
        [(_rope(q[:, h * DH:(h + 1) * DH].astype(F32), cos, sin) * scale).astype(BF16)
         for h in range(NSA_HPG)], axis=0)

    s = lax.dot_general(qs, kc_ref[...], nt, preferred_element_type=F32) * scale
    n_idx = lax.broadcasted_iota(jnp.int32, (1, ncp), 1)
    cmask = (n_idx * CMP_STRIDE + (CMP_LEN - 1)) <= pos_rows
    s = jnp.where(cmask, s, NEG_INF)
    m = jnp.max(s, axis=-1, keepdims=True)
    e = jnp.exp(s - m)
    p = jnp.where(cmask, e / jnp.sum(e, axis=-1, keepdims=True), 0.0)
    o_cmp = jnp.dot(p.astype(BF16), vc_ref[...], preferred_element_type=F32)

    psum = p[0:tq]
    for h in range(1, NSA_HPG):
        psum = psum + p[h * tq:(h + 1) * tq]
    imp = jnp.dot(psum, cover_ref[...], preferred_element_type=F32,
                  precision=lax.Precision.HIGHEST)
    blk = lax.broadcasted_iota(jnp.int32, (tq, LANES), 1)
    t_blk = pos_q // SEL_LEN
    forced = (blk == 0) | (blk == t_blk) | (blk == t_blk - 1)
    bonus = jnp.where(blk > t_blk, -SEL_FORCE, jnp.where(forced, SEL_FORCE, 0.0))
    val_t = (imp + bonus).T[:n_blk]
    blk_t = lax.broadcasted_iota(jnp.int32, (n_blk, tq), 0)
    terms = []
    for c in range(n_blk):
        vc = val_t[c:c + 1, :]
        beats = (vc > val_t) | ((vc == val_t) & (blk_t > c))
        terms.append(jnp.where(beats, 1.0, 0.0))
    while len(terms) > 1:
        terms = [a + b for a, b in zip(terms[0::2], terms[1::2])]
    rank = terms[0]
    pen_t = jnp.where(rank < n_sel, 0.0, -SEL_MASK)
    pen_t = jnp.concatenate([pen_t, jnp.zeros((LANES - n_blk, tq), F32)], axis=0)
    pen = pen_t.T.astype(BF16)

    q_ext = jnp.concatenate([qr, jnp.concatenate([pen] * NSA_HPG, axis=0)], axis=1)

    n_chunks = (q0 + tq + ck - 1) // ck

    q_heads = [q_ext[h * tq:(h + 1) * tq] for h in range(NSA_HPG)]

    def sel_chunk(c, carry, causal):
        k0 = pl.multiple_of(c * ck, ck)
        kblk = kext_scr[pl.ds(k0, ck), :]
        vblk = vsext_scr[pl.ds(k0, ck), :]
        scs = [lax.dot_general(q_heads[h], kblk, nt, preferred_element_type=F32)
               for h in range(NSA_HPG)]
        out = []
        for h in range(NSA_HPG):
            m_i, acc = carry[h]
            sc = scs[h]
            if causal:
                kp = k0 + lax.broadcasted_iota(jnp.int32, (1, ck), 1)
                sc = jnp.where(kp <= pos_q, sc, NEG_INF)
            m_new = jnp.maximum(m_i, jnp.max(sc, axis=-1, keepdims=True))
            alpha = jnp.exp(m_i - m_new)
            pc = jnp.exp(sc - m_new)
            acc = alpha * acc + jnp.dot(pc.astype(BF16), vblk, preferred_element_type=F32)
            out.append((m_new, acc))
        return tuple(out)

    carry = tuple((jnp.full((tq, 1), NEG_INF, F32), jnp.zeros((tq, 2 * DH), F32))
                  for _ in range(NSA_HPG))
    carry = lax.fori_loop(0, n_chunks - 1, lambda c, cr: sel_chunk(c, cr, False), carry)
    carry = sel_chunk(n_chunks - 1, carry, True)

    kstart = pl.multiple_of(jnp.maximum(q0 - WINDOW, 0), tq)
    kwin = kw_ref[pl.ds(kstart, wspan), :]
    vwin = vwext_scr[pl.ds(kstart, wspan), :]
    wbias = wbias_ref[...]
    sws = [lax.dot_general(qr[h * tq:(h + 1) * tq], kwin, nt, preferred_element_type=F32)
           for h in range(NSA_HPG)]

    gates = jax.nn.sigmoid(gate_ref[...])
    for h in range(NSA_HPG):
        acc_s = carry[h][1]
        o_sel = acc_s[:, :DH] / acc_s[:, DH:]
        sw = sws[h] + wbias
        ew = jnp.exp(sw - jnp.max(sw, axis=-1, keepdims=True))
        acc_w = jnp.dot(ew.astype(BF16), vwin, preferred_element_type=F32)
        o_win = acc_w[:, :DH] / acc_w[:, DH:]
        c0 = 3 * h
        o_h = (gates[:, c0:c0 + 1] * o_cmp[h * tq:(h + 1) * tq] + gates[:, c0 + 1:c0 + 2] * o_sel
               + gates[:, c0 + 2:c0 + 3] * o_win)
        o_ref[:, h * DH:(h + 1) * DH] = o_h.astype(BF16)


def _nsa(big, cmp_kv, small, cos2, sin2, cover, et, batch, seq, main0):
    tq = 256
    wbias = _window_bias(tq)
    ck = min(512, seq)
    nq = seq // tq
    ncp = cmp_kv.shape[3]
    wspan = tq + WINDOW
    n_wb = wbias.shape[0]
    kern = functools.partial(_nsa_kernel, seq=seq, tq=tq, ck=ck)
    kv_col = (main0 + NSA_Q) // DH

    def kvspec(which):
        return pl.BlockSpec((seq, DH), lambda b, g, i: (b, kv_col + 2 * which + g))

    return pl.pallas_call(
        kern,
        grid=(batch, NSA_GROUPS, nq),
        in_specs=[
            pl.BlockSpec((tq, NSA_HPG * DH), lambda b, g, i: (b * nq + i, main0 // (NSA_HPG * DH) + g)),
            pl.BlockSpec((None, None, None, ncp, DH), lambda b, g, i: (b, g, 0, 0, 0)),
            pl.BlockSpec((None, None, None, ncp, DH), lambda b, g, i: (b, g, 1, 0, 0)),
            kvspec(2), kvspec(3), kvspec(4), kvspec(5),
            pl.BlockSpec((tq, DH), lambda b, g, i: (i, 0)),
            pl.BlockSpec((tq, DH), lambda b, g, i: (i, 0)),
            pl.BlockSpec((tq, LANES), lambda b, g, i: (b * nq + i, g)),
            pl.BlockSpec((ncp, LANES), lambda b, g, i: (0, 0)),
            pl.BlockSpec((seq, LANES), lambda b, g, i: (0, 0)),
            pl.BlockSpec((None, tq, wspan), lambda b, g, i: (jnp.minimum(i, n_wb - 1), 0, 0)),
        ],
        out_specs=pl.BlockSpec((tq, NSA_HPG * DH), lambda b, g, i: (b * nq + i, g)),
        out_shape=jax.ShapeDtypeStruct((batch * seq, NSA_Q), BF16),
        scratch_shapes=[pltpu.VMEM((seq, 2 * DH), BF16), pltpu.VMEM((seq, 2 * DH), BF16),
                        pltpu.VMEM((seq, 2 * DH), BF16)],
        compiler_params=_cparams(("parallel", "parallel", "arbitrary")),
        name="nsa",
    )(big, cmp_kv, cmp_kv, big, big, big, big, cos2, sin2, small, cover, et, wbias)


def _gla_kernel(q_ref, k_ref, v_ref, r_ref, small_ref, wa_ref, ba_ref, gain_ref, tri_ref,
                o_ref, qt_scr, kt_scr, ks_scr, dec_scr, *, seq):
    C = GLA_CHUNK
    nc = seq // C
    GR = GLA_GROUP * C
    nt = (((1,), (1,)), ((), ()))
    z = jnp.dot(small_ref[...], wa_ref[...], preferred_element_type=F32,
                precision=lax.Precision.HIGHEST) + ba_ref[...]
    la = jax.nn.log_sigmoid(z) / GLA_TAU

    la_r = jnp.concatenate([la[c * C:(c + 1) * C, :] for c in range(nc)], axis=1)
    cum_r = jnp.dot(tri_ref[...], la_r, preferred_element_type=F32,
                    precision=lax.Precision.HIGHEST)
    last_r = cum_r[C - 1:C, :]
    qf_r = jnp.exp(cum_r)
    kf_r = jnp.exp(-cum_r)
    sf_r = jnp.exp(last_r - cum_r)
    dec_r = jnp.exp(last_r)
    qscale = GLA_DK ** -0.5
    for c in range(nc):
        rows = slice(c * C, (c + 1) * C)
        cols = slice(c * GLA_DK, (c + 1) * GLA_DK)
        qc = q_ref[rows, :].astype(F32) * qscale
        kc = k_ref[rows, :].astype(F32)
        qt_scr[rows, :] = (qc * qf_r[:, cols]).astype(BF16)
        kt_scr[rows, :] = (kc * kf_r[:, cols]).astype(BF16)
        ks_scr[rows, :] = (kc * sf_r[:, cols]).astype(BF16)
        dec_scr[c:c + 1, :] = dec_r[:, cols]

    ri = lax.broadcasted_iota(jnp.int32, (GR, GR), 0)
    ci = lax.broadcasted_iota(jnp.int32, (GR, GR), 1)
    mask = (ri >= ci) & ((ri // C) == (ci // C))

    def body(g, st):
        r0 = pl.multiple_of(g * GR, GR)
        qt = qt_scr[pl.ds(r0, GR), :]
        kt = kt_scr[pl.ds(r0, GR), :]
        ks = ks_scr[pl.ds(r0, GR), :]
        v = v_ref[pl.ds(r0, GR), :]
        attn = lax.dot_general(qt, kt, nt, preferred_element_type=F32)
        attn = jnp.where(mask, attn, 0.0)
        o_intra = jnp.dot(attn.astype(BF16), v, preferred_element_type=F32)
        outs = []
        for cc in range(GLA_GROUP):
            sl = slice(cc * C, (cc + 1) * C)
            outs.append(o_intra[sl] + lax.dot_general(qt[sl], st.astype(BF16), nt,
                                                      preferred_element_type=F32))
            d_st = lax.dot_general(v[sl], ks[sl], (((0,), (0,)), ((), ())),
                                   preferred_element_type=F32)
            st = st * dec_scr[pl.ds(g * GLA_GROUP + cc, 1), :] + d_st
        o = jnp.concatenate(outs, axis=0)
        o = o * lax.rsqrt(jnp.mean(o * o, axis=-1, keepdims=True) + NORM_EPS)
        o = o * gain_ref[...]
        rr = r_ref[pl.ds(r0, GR), :].astype(F32)
        o_ref[pl.ds(r0, GR), :] = (o * (rr * jax.nn.sigmoid(rr))).astype(BF16)
        return st

    lax.fori_loop(0, seq // GR, body, jnp.zeros((GLA_DV, GLA_DK), F32), unroll=2)


def _gla(big, small, w_alpha, b_alpha, gain, tri, batch, seq, main0):
    qcol = (main0 + NSA_Q + 6 * NSA_KV) // GLA_DK
    kcol = qcol + GLA_QK // GLA_DK
    vcol = (main0 + NSA_Q + 6 * NSA_KV + 2 * GLA_QK) // GLA_DV
    rcol = vcol + GLA_V // GLA_DV
    kern = functools.partial(_gla_kernel, seq=seq)
    return pl.pallas_call(
        kern,
        grid=(batch, GLA_HEADS),
        in_specs=[
            pl.BlockSpec((seq, GLA_DK), lambda b, h: (b, qcol + h)),
            pl.BlockSpec((seq, GLA_DK), lambda b, h: (b, kcol + h)),
            pl.BlockSpec((seq, GLA_DV), lambda b, h: (b, vcol + h)),
            pl.BlockSpec((seq, GLA_DV), lambda b, h: (b, rcol + h)),
            pl.BlockSpec((seq, LANES), lambda b, h: (b, NSA_GROUPS)),
            pl.BlockSpec((LANES, GLA_DK), lambda b, h: (0, h)),
            pl.BlockSpec((1, GLA_DK), lambda b, h: (0, h)),
            pl.BlockSpec((1, GLA_DV), lambda b, h: (0, h)),
            pl.BlockSpec((GLA_CHUNK, GLA_CHUNK), lambda b, h: (0, 0)),
        ],
        out_specs=pl.BlockSpec((seq, GLA_DV), lambda b, h: (b, h)),
        out_shape=jax.ShapeDtypeStruct((batch * seq, GLA_V), BF16),
        scratch_shapes=[pltpu.VMEM((seq, GLA_DK), BF16), pltpu.VMEM((seq, GLA_DK), BF16),
                        pltpu.VMEM((seq, GLA_DK), BF16), pltpu.VMEM((seq // GLA_CHUNK, GLA_DK), F32)],
        compiler_params=_cparams(("parallel", "parallel")),
        name="gla",
    )(big, big, big, big, small, w_alpha, b_alpha, gain, tri)


def _mix_kernel(on_ref, og_ref, ma_ref, mb_ref, x_ref, wpn_ref, wpg_ref, wo_ref, gain_ref,
                wrh_ref, wrl_ref, br_ref, x2_ref, h2_ref, lg_ref):
    a = jnp.dot(on_ref[...], wpn_ref[...], preferred_element_type=F32)
    b = jnp.dot(og_ref[...], wpg_ref[...], preferred_element_type=F32)
    mixed = (jax.nn.sigmoid(ma_ref[...].astype(F32)) * a
             + jax.nn.sigmoid(mb_ref[...].astype(F32)) * b)
    x2 = x_ref[...] + jnp.dot(mixed.astype(BF16), wo_ref[...], preferred_element_type=F32)
    x2_ref[...] = x2
    ms = jnp.mean(x2 * x2, axis=-1, keepdims=True)
    h2 = x2 * lax.rsqrt(ms + NORM_EPS) * gain_ref[...]
    h2_ref[...] = h2
    hi = h2.astype(BF16)
    lo = (h2 - hi.astype(F32)).astype(BF16)
    lg = (jnp.dot(hi, wrh_ref[...], preferred_element_type=F32)
          + jnp.dot(lo, wrh_ref[...], preferred_element_type=F32)
          + jnp.dot(hi, wrl_ref[...], preferred_element_type=F32))
    lg_ref[...] = lg + br_ref[...]


def _mix(o_nsa, o_gla, big, x2d, wpn, wpg, wo, gain, wr_hi, wr_lo, br):
    n, d = x2d.shape
    tm = 256
    const = lambda i: (0, 0)
    return pl.pallas_call(
        _mix_kernel,
        grid=(n // tm,),
        in_specs=[
            pl.BlockSpec((tm, NSA_Q), lambda i: (i, 0)),
            pl.BlockSpec((tm, GLA_V), lambda i: (i, 0)),
            pl.BlockSpec((tm, d), lambda i: (i, 0)),
            pl.BlockSpec((tm, d), lambda i: (i, 1)),
            pl.BlockSpec((tm, d), lambda i: (i, 0)),
            pl.BlockSpec((NSA_Q, d), const, pipeline_mode=pl.Buffered(1)),
            pl.BlockSpec((GLA_V, d), const, pipeline_mode=pl.Buffered(1)),
            pl.BlockSpec((d, d), const, pipeline_mode=pl.Buffered(1)),
            pl.BlockSpec((1, d), const),
            pl.BlockSpec((d, LANES), const),
            pl.BlockSpec((d, LANES), const),
            pl.BlockSpec((1, LANES), const),
        ],
        out_specs=[
            pl.BlockSpec((tm, d), lambda i: (i, 0)),
            pl.BlockSpec((tm, d), lambda i: (i, 0)),
            pl.BlockSpec((tm, LANES), lambda i: (i, 0)),
        ],
        out_shape=[
            jax.ShapeDtypeStruct((n, d), F32),
            jax.ShapeDtypeStruct((n, d), F32),
            jax.ShapeDtypeStruct((n, LANES), F32),
        ],
        compiler_params=_cparams(("parallel",)),
        name="mix",
    )(o_nsa, o_gla, big, big, x2d, wpn, wpg, wo, gain, wr_hi, wr_lo, br)


def _route_kernel(lg_ref, tri_ref, idx_ref, w_ref, cnt_ref, carry_scr, *, n_exp):
    i = pl.program_id(0)
    tr = lg_ref.shape[0]

    @pl.when(i == 0)
    def _():
        carry_scr[...] = jnp.zeros_like(carry_scr)

    lane = lax.broadcasted_iota(jnp.int32, (tr, LANES), 1)
    work = jnp.where(lane < n_exp, lg_ref[...], -3e38)
    onehots, vals, idxs = [], [], []
    for _ in range(TOP_K):
        mval = jnp.max(work, axis=-1, keepdims=True)
        idx = jnp.min(jnp.where(work == mval, lane, LANES), axis=-1, keepdims=True)
        oh = lane == idx
        work = jnp.where(oh, -3e38, work)
        onehots.append(oh)
        vals.append(mval)
        idxs.append(idx)
    exps = [jnp.exp(v - vals[0]) for v in vals]
    den = exps[0] + exps[1] + exps[2] + exps[3]
    onehot = jnp.zeros((tr, LANES), F32)
    for oh in onehots:
        onehot = onehot + jnp.where(oh, 1.0, 0.0)
    before = jnp.dot(tri_ref[...], onehot.astype(BF16), preferred_element_type=F32) + carry_scr[...]
    carry_scr[...] = carry_scr[...] + jnp.sum(onehot, axis=0, keepdims=True)
    idx_out = jnp.zeros((tr, LANES), jnp.int32)
    w_out = jnp.zeros((tr, LANES), F32)
    for k in range(TOP_K):
        rank_k = jnp.sum(jnp.where(onehots[k], before, 0.0), axis=-1, keepdims=True)
        idx_out = jnp.where(lane == k, idxs[k], idx_out)
        idx_out = jnp.where(lane == TOP_K + k, rank_k.astype(jnp.int32), idx_out)
        w_out = jnp.where(lane == k, exps[k] / den, w_out)
    idx_ref[...] = idx_out
    w_ref[...] = w_out
    cnt_ref[...] = carry_scr[...]


def _route(logits, tri, n_exp):
    n = logits.shape[0]
    tr = tri.shape[0]
    kern = functools.partial(_route_kernel, n_exp=n_exp)
    return pl.pallas_call(
        kern,
        grid=(n // tr,),
        in_specs=[pl.BlockSpec((tr, LANES), lambda i: (i, 0)),
                  pl.BlockSpec((tr, tr), lambda i: (0, 0))],
        out_specs=[pl.BlockSpec((tr, LANES), lambda i: (i, 0)),
                   pl.BlockSpec((tr, LANES), lambda i: (i, 0)),
                   pl.BlockSpec((1, LANES), lambda i: (0, 0))],
        out_shape=[jax.ShapeDtypeStruct((n, LANES), jnp.int32),
                   jax.ShapeDtypeStruct((n, LANES), F32),
                   jax.ShapeDtypeStruct((1, LANES), F32)],
        scratch_shapes=[pltpu.VMEM((1, LANES), F32)],
        compiler_params=_cparams(("arbitrary",)),
        name="route",
    )(logits, tri)


def _dispatch_kernel(dest_ref, padlo_ref, padhi_ref, h_ref, xs_hbm, zbuf, sem, zsem, *, td, tb, n_exp):
    i = pl.program_id(0)
    n_blocks = xs_hbm.shape[0] // tb

    @pl.when(i == 0)
    def _():
        zbuf[...] = jnp.zeros_like(zbuf)

        def row_copy(s):
            return pltpu.make_async_copy(zbuf.at[pl.ds(0, 1)], xs_hbm.at[pl.ds(s, 1)], zsem)

        def per_expert(e, c):
            lo = padlo_ref[e]
            hi = padhi_ref[e]
            lax.fori_loop(lo, hi, lambda s, c2: (row_copy(s).start(), c2)[1], 0)
            lax.fori_loop(lo, hi, lambda s, c2: (row_copy(s).wait(), c2)[1], 0)
            return c

        lax.fori_loop(0, n_exp, per_expert, 0)

        def blk_copy(b):
            return pltpu.make_async_copy(zbuf, xs_hbm.at[pl.ds(b * tb, tb)], zsem)

        first_free = padhi_ref[n_exp - 1] // tb
        lax.fori_loop(first_free, n_blocks, lambda b, c: (blk_copy(b).start(), c)[1], 0)
        lax.fori_loop(first_free, n_blocks, lambda b, c: (blk_copy(b).wait(), c)[1], 0)

    base = i * td * TOP_K

    def start(t, c):
        for k in range(TOP_K):
            pltpu.make_async_copy(h_ref.at[pl.ds(t, 1)],
                                  xs_hbm.at[pl.ds(dest_ref[base + t * TOP_K + k], 1)],
                                  sem).start(priority=k % 2)
        return c

    lax.fori_loop(0, td, start, 0, unroll=2)
    for k in range(TOP_K):
        pltpu.make_async_copy(h_ref, xs_hbm.at[pl.ds(0, td)], sem).wait()


def _dispatch(dest, pad_lo, pad_hi, h2, n_slots, n_exp, tb):
    n, d = h2.shape
    td = 256
    kern = functools.partial(_dispatch_kernel, td=td, tb=tb, n_exp=n_exp)
    return pl.pallas_call(
        kern,
        grid_spec=pltpu.PrefetchScalarGridSpec(
            num_scalar_prefetch=3,
            grid=(n // td,),
            in_specs=[pl.BlockSpec((td, d), lambda i, dr, lo, hi: (i, 0))],
            out_specs=pl.BlockSpec(memory_space=pl.ANY),
            scratch_shapes=[pltpu.VMEM((tb, d), F32), pltpu.SemaphoreType.DMA, pltpu.SemaphoreType.DMA],
        ),
        out_shape=jax.ShapeDtypeStruct((n_slots, d), F32),
        compiler_params=_cparams(("arbitrary",)),
        name="dispatch",
    )(dest, pad_lo, pad_hi, h2)


def _expert_step(first_ref, count_ref, nxt_ref, w_copies, cast, in_copy, out_copy, compute,
                 zero_out, n_blocks):
    j = pl.program_id(0)
    e = pl.program_id(1)
    n_exp = pl.num_programs(1)
    fb = first_ref[e]
    nb = count_ref[e]
    first_e = nxt_ref[n_exp]

    @pl.when((j == 0) & (e == first_e))
    def _():
        for c in w_copies(e, 0):
            c.start(priority=1)

    @pl.when(nb > 0)
    def _():
        in_copy(fb, 0).start()
        for c in w_copies(e, j):
            c.wait()
        cast()
        ne = nxt_ref[e]

        @pl.when(ne < n_exp)
        def _():
            for c in w_copies(ne, j):
                c.start(priority=1)

        @pl.when((ne >= n_exp) & (j + 1 < pl.num_programs(0)))
        def _():
            for c in w_copies(first_e, j + 1):
                c.start(priority=1)

        def block(i, carry):
            slot = i % 2
            in_copy(fb + i, slot).wait()

            @pl.when(i + 1 < nb)
            def _():
                in_copy(fb + i + 1, 1 - slot).start()

            @pl.when(i >= 2)
            def _():
                out_copy(fb + i - 2, slot).wait()

            compute(slot)
            out_copy(fb + i, slot).start()
            return carry

        lax.fori_loop(0, nb, block, 0)

        @pl.when(nb >= 2)
        def _():
            out_copy(fb + nb - 2, nb % 2).wait()

        out_copy(fb + nb - 1, (nb - 1) % 2).wait()

    @pl.when(e == n_exp - 1)
    def _():
        used = first_ref[n_exp]
        zero_out()
        lax.fori_loop(used, n_blocks, lambda b, c: (out_copy(b, 0).start(), c)[1], 0)
        lax.fori_loop(used, n_blocks, lambda b, c: (out_copy(b, 0).wait(), c)[1], 0)


def _ffn_up_kernel(first_ref, count_ref, nxt_ref, xs_hbm, w_hbm, bg_ref, bu_ref, act_hbm,
                   wf32, wbf, xbuf, obuf, wsem, xsem, osem, *, tb, tf, ff):
    j = pl.program_id(0)

    def w_copies(e, jj):
        return [pltpu.make_async_copy(w_hbm.at[e, :, pl.ds(pl.multiple_of(h * ff + jj * tf, tf), tf)],
                                      wf32.at[h], wsem.at[h]) for h in range(2)]

    def cast():
        wbf[...] = wf32[...].astype(BF16)

    def in_copy(blk, slot):
        return pltpu.make_async_copy(xs_hbm.at[pl.ds(pl.multiple_of(blk * tb, tb), tb)],
                                     xbuf.at[slot], xsem.at[slot])

    def out_copy(blk, slot):
        return pltpu.make_async_copy(
            obuf.at[slot],
            act_hbm.at[pl.ds(pl.multiple_of(blk * tb, tb), tb), pl.ds(pl.multiple_of(j * tf, tf), tf)],
            osem.at[slot])

    def compute(slot):
        x = xbuf[slot].astype(BF16)
        gate = jnp.dot(x, wbf[0], preferred_element_type=F32) + bg_ref[...]
        up = jnp.dot(x, wbf[1], preferred_element_type=F32) + bu_ref[...]
        gate = jnp.minimum(gate, SWIGLU_LIMIT)
        up = jnp.clip(up, -SWIGLU_LIMIT, SWIGLU_LIMIT)
        obuf[slot] = ((up + 1.0) * gate * jax.nn.sigmoid(gate * SWIGLU_ALPHA)).astype(BF16)

    def zero_out():
        obuf[0] = jnp.zeros(obuf.shape[1:], BF16)

    _expert_step(first_ref, count_ref, nxt_ref, w_copies, cast, in_copy, out_copy, compute,
                 zero_out, act_hbm.shape[0] // tb)


def _ffn_up(first, count, nxt, xs, w_gate_up, b_gate_up, tb):
    n_slots, d = xs.shape
    n_exp, _, f2 = w_gate_up.shape
    ff = f2 // 2
    tf = min(1024, ff)
    nf = ff // tf
    kern = functools.partial(_ffn_up_kernel, tb=tb, tf=tf, ff=ff)
    return pl.pallas_call(
        kern,
        grid_spec=pltpu.PrefetchScalarGridSpec(
            num_scalar_prefetch=3,
            grid=(nf, n_exp),
            in_specs=[
                pl.BlockSpec(memory_space=pl.ANY),
                pl.BlockSpec(memory_space=pl.ANY),
                pl.BlockSpec((None, 1, tf), lambda j, e, fr, cn, nx: (e, 0, j)),
                pl.BlockSpec((None, 1, tf), lambda j, e, fr, cn, nx: (e, 0, nf + j)),
            ],
            out_specs=pl.BlockSpec(memory_space=pl.ANY),
            scratch_shapes=[pltpu.VMEM((2, d, tf), F32), pltpu.VMEM((2, d, tf), BF16),
                            pltpu.VMEM((2, tb, d), F32), pltpu.VMEM((2, tb, tf), BF16),
                            pltpu.SemaphoreType.DMA((2,)), pltpu.SemaphoreType.DMA((2,)),
                            pltpu.SemaphoreType.DMA((2,))],
        ),
        out_shape=jax.ShapeDtypeStruct((n_slots, ff), BF16),
        compiler_params=_cparams(("arbitrary", "arbitrary")),
        name="ffn_up",
    )(first, count, nxt, xs, w_gate_up, b_gate_up, b_gate_up)


def _ffn_down_kernel(first_ref, count_ref, nxt_ref, act_hbm, w_hbm, bd_ref, out_hbm,
                     wf32, wbf, abuf, obuf, wsem, asem, osem, *, tb, tn):
    j = pl.program_id(0)

    def w_copies(e, jj):
        return [pltpu.make_async_copy(w_hbm.at[e, :, pl.ds(pl.multiple_of(jj * tn, tn), tn)],
                                      wf32, wsem.at[0])]

    def cast():
        wbf[...] = wf32[...].astype(BF16)

    def in_copy(blk, slot):
        return pltpu.make_async_copy(act_hbm.at[pl.ds(pl.multiple_of(blk * tb, tb), tb)],
                                     abuf.at[slot], asem.at[slot])

    def out_copy(blk, slot):
        return pltpu.make_async_copy(
            obuf.at[slot],
            out_hbm.at[pl.ds(pl.multiple_of(blk * tb, tb), tb), pl.ds(pl.multiple_of(j * tn, tn), tn)],
            osem.at[slot])

    def compute(slot):
        obuf[slot] = jnp.dot(abuf[slot], wbf[...], preferred_element_type=F32) + bd_ref[...]

    def zero_out():
        obuf[0] = jnp.zeros(obuf.shape[1:], F32)

    _expert_step(first_ref, count_ref, nxt_ref, w_copies, cast, in_copy, out_copy, compute,
                 zero_out, out_hbm.shape[0] // tb)


def _ffn_down(first, count, nxt, act, w_down, b_down, tb):
    n_slots, ff = act.shape
    n_exp, _, d = w_down.shape
    tn = min(2048, d)
    nn = d // tn
    kern = functools.partial(_ffn_down_kernel, tb=tb, tn=tn)
    return pl.pallas_call(
        kern,
        grid_spec=pltpu.PrefetchScalarGridSpec(
            num_scalar_prefetch=3,
            grid=(nn, n_exp),
            in_specs=[
                pl.BlockSpec(memory_space=pl.ANY),
                pl.BlockSpec(memory_space=pl.ANY),
                pl.BlockSpec((None, 1, tn), lambda j, e, fr, cn, nx: (e, 0, j)),
            ],
            out_specs=pl.BlockSpec(memory_space=pl.ANY),
            scratch_shapes=[pltpu.VMEM((ff, tn), F32), pltpu.VMEM((ff, tn), BF16),
                            pltpu.VMEM((2, tb, ff), BF16), pltpu.VMEM((2, tb, tn), F32),
                            pltpu.SemaphoreType.DMA((1,)), pltpu.SemaphoreType.DMA((2,)),
                            pltpu.SemaphoreType.DMA((2,))],
        ),
        out_shape=jax.ShapeDtypeStruct((n_slots, d), F32),
        compiler_params=_cparams(("arbitrary", "arbitrary")),
        name="ffn_down",
    )(first, count, nxt, act, w_down, b_down)


def _combine_kernel(dest_ref, outs_hbm, w_ref, x2_ref, gain_ref, y_ref, buf, sem, *, tc):
    i = pl.program_id(0)
    slot = i % 2

    def gather(step, into):
        base = step * tc * TOP_K

        def start(t, c):
            for k in range(TOP_K):
                pltpu.make_async_copy(outs_hbm.at[pl.ds(dest_ref[base + t * TOP_K + k], 1)],
                                      buf.at[into, k, pl.ds(t, 1)], sem.at[into]).start(priority=k % 2)
            return c

        lax.fori_loop(0, tc, start, 0, unroll=2)

    @pl.when(i == 0)
    def _():
        gather(0, 0)

    @pl.when(i + 1 < pl.num_programs(0))
    def _():
        gather(i + 1, 1 - slot)

    for k in range(TOP_K):
        pltpu.make_async_copy(outs_hbm.at[pl.ds(0, tc)], buf.at[slot, k], sem.at[slot]).wait()
    w = w_ref[...]
    y = x2_ref[...]
    for k in range(TOP_K):
        y = y + w[:, k:k + 1] * buf[slot, k]
    ms = jnp.mean(y * y, axis=-1, keepdims=True)
    y_ref[...] = y * lax.rsqrt(ms + NORM_EPS) * gain_ref[...]


def _combine(dest, outs, w_top, x2, gain):
    n, d = x2.shape
    tc = 128
    kern = functools.partial(_combine_kernel, tc=tc)
    return pl.pallas_call(
        kern,
        grid_spec=pltpu.PrefetchScalarGridSpec(
            num_scalar_prefetch=1,
            grid=(n // tc,),
            in_specs=[
                pl.BlockSpec(memory_space=pl.ANY),
                pl.BlockSpec((tc, LANES), lambda i, dr: (i, 0)),
                pl.BlockSpec((tc, d), lambda i, dr: (i, 0)),
                pl.BlockSpec((1, d), lambda i, dr: (0, 0)),
            ],
            out_specs=pl.BlockSpec((tc, d), lambda i, dr: (i, 0)),
            scratch_shapes=[pltpu.VMEM((2, TOP_K, tc, d), F32), pltpu.SemaphoreType.DMA((2,))],
        ),
        out_shape=jax.ShapeDtypeStruct((n, d), F32),
        compiler_params=_cparams(("arbitrary",)),
        name="combine",
    )(dest, outs, w_top, x2, gain)


def _window_bias(tq):
    nv = WINDOW // tq
    r = np.arange(tq)[None, :, None]
    c = np.arange(tq + WINDOW)[None, None, :]
    v = np.arange(nv + 1)[:, None, None]
    q0 = v * tq
    kstart = np.maximum(q0 - WINDOW, 0)
    rel = (kstart + c) - (q0 + r)
    return np.where((rel <= 0) & (rel > -WINDOW), 0.0, NEG_INF).astype(np.float32)


def _rope_tables(seq):
    half = DH // 2
    inv_freq = np.float32(ROPE_THETA) ** (-np.arange(half, dtype=np.float32) / np.float32(half))
    ang = np.arange(seq, dtype=np.float32)[:, None] * inv_freq[None, :].astype(np.float32)
    cos = np.cos(ang.astype(np.float64)).astype(np.float32)
    sin = np.sin(ang.astype(np.float64)).astype(np.float32)
    return np.concatenate([cos, cos], axis=1), np.concatenate([-sin, sin], axis=1)


def _layer(x2d, batch, seq, norm_mix, w_in, cmp_pos_k, cmp_pos_v, w_cmp_k1, w_cmp_k2, w_cmp_v1,
           w_cmp_v2, w_gla_alpha, b_gla_alpha, gla_norm, w_proj_nsa, w_proj_gla, w_merge_gate,
           b_merge_gate, w_out, norm_moe, w_router, b_router, w_gate_up, b_gate_up, w_down, b_down,
           final_gain):
    n, d = x2d.shape
    n_exp = w_router.shape[1]
    d_merge = w_merge_gate.shape[1]
    main0 = d_merge

    o_g = NSA_Q + 6 * NSA_KV
    o_q = o_g + NSA_HEADS * 3
    o_a = o_q + 2 * GLA_QK + 2 * GLA_V
    w_main = jnp.concatenate([w_in[:, :o_g], w_in[:, o_q:o_a]], axis=1)
    w_all = jnp.concatenate([w_merge_gate, w_main], axis=1).astype(BF16)
    n_gl = NSA_HPG * 3
    small_parts = []
    for g in range(NSA_GROUPS):
        small_parts += [w_in[:, o_g + g * n_gl:o_g + (g + 1) * n_gl], jnp.zeros((d, LANES - n_gl), F32)]
    small_parts += [w_in[:, o_a:], jnp.zeros((d, LANES - GLA_RANK), F32)]
    w_small = jnp.concatenate(small_parts, axis=1).astype(BF16)
    bias_all = jnp.concatenate([b_merge_gate, jnp.zeros((MAIN_W,), F32)])[None, :]
    cos2, sin2 = _rope_tables(seq)

    big, small = _proj(x2d, norm_mix[None, :], w_all, w_small, bias_all, cos2, sin2, seq, d_merge)

    pos = jnp.stack([cmp_pos_k, cmp_pos_v])
    w1 = jnp.stack([w_cmp_k1, w_cmp_v1]).astype(BF16)
    w2 = jnp.stack([w_cmp_k2, w_cmp_v2]).astype(BF16)
    cmp_kv = _compress(big, pos, w1, w2, batch, seq, (main0 + NSA_Q) // DH)
    ncp = cmp_kv.shape[3]
    n_blk = seq // SEL_LEN
    cstart = np.arange(ncp)[:, None] * CMP_STRIDE
    blk = np.arange(LANES)[None, :]
    cover = ((cstart < (blk + 1) * SEL_LEN) & (cstart + CMP_LEN > blk * SEL_LEN)
             & (blk < n_blk) & (np.arange(ncp)[:, None] < seq // CMP_STRIDE - 1)).astype(np.float32)
    et = jnp.asarray((np.arange(seq)[:, None] // SEL_LEN) == np.arange(LANES)[None, :], BF16)
    o_nsa = _nsa(big, cmp_kv, small, cos2, sin2, cover, et, batch, seq, main0)

    tri_c = np.tril(np.ones((GLA_CHUNK, GLA_CHUNK), np.float32))
    w_alpha = jnp.concatenate([w_gla_alpha, jnp.zeros((LANES - GLA_RANK, GLA_QK), F32)], axis=0)
    o_gla = _gla(big, small, w_alpha, b_gla_alpha[None, :], gla_norm[None, :], tri_c,
                 batch, seq, main0)

    wr = jnp.concatenate([w_router, jnp.zeros((d, LANES - n_exp), F32)], axis=1)
    wr_hi = wr.astype(BF16)
    wr_lo = (wr - wr_hi.astype(F32)).astype(BF16)
    br = jnp.concatenate([b_router, jnp.zeros((LANES - n_exp,), F32)])[None, :]
    x2, h2, logits = _mix(o_nsa, o_gla, big, x2d, w_proj_nsa.astype(BF16), w_proj_gla.astype(BF16),
                          w_out.astype(BF16), norm_moe[None, :], wr_hi, wr_lo, br)

    tr = 512
    tri_r = jnp.asarray(np.arange(tr)[:, None] > np.arange(tr)[None, :], BF16)
    ridx, w_top, counts = _route(logits, tri_r, n_exp)

    tb = 256
    nk = n * TOP_K
    n_blocks = -(-nk // tb) + n_exp
    n_slots = n_blocks * tb
    cnt = counts[0, :n_exp].astype(jnp.int32)
    padded = (cnt + tb - 1) // tb * tb
    pad_end = jnp.cumsum(padded)
    pad_start = pad_end - padded
    top_e = ridx[:, :TOP_K]
    e_ids = jnp.arange(n_exp, dtype=jnp.int32)
    start_of = jnp.sum(jnp.where(top_e[:, :, None] == e_ids, pad_start[None, None, :], 0), axis=-1)
    dest = (start_of + ridx[:, TOP_K:2 * TOP_K]).reshape(nk)
    first = jnp.concatenate([pad_start, pad_end[-1:]]) // tb
    count = padded // tb
    ids = jnp.arange(n_exp + 1, dtype=jnp.int32)
    cand = jnp.where(count > 0, ids[:n_exp], n_exp)
    later = ids[None, :n_exp] >= ids[:, None]
    after = jnp.concatenate([later[1:], later[:1]], axis=0)
    nxt = jnp.min(jnp.where(after, cand[None, :], n_exp), axis=1)

    xs = _dispatch(dest, pad_start + cnt, pad_end, h2, n_slots, n_exp, tb)
    act = _ffn_up(first, count, nxt, xs, w_gate_up, b_gate_up[:, None, :], tb)
    outs = _ffn_down(first, count, nxt, act, w_down, b_down[:, None, :], tb)
    return _combine(dest, outs, w_top, x2, final_gain)


def kernel(x, norm_mix, w_in, cmp_pos_k, cmp_pos_v, w_cmp_k1, w_cmp_k2, w_cmp_v1, w_cmp_v2,
           w_gla_alpha, b_gla_alpha, gla_norm, w_proj_nsa, w_proj_gla, w_merge_gate, b_merge_gate,
           w_out, norm_moe, w_router, b_router, w_gate_up, b_gate_up, w_down, b_down, norm_final):
    batch, seq, d = x.shape
    depth = w_in.shape[0]
    assert depth == 1, "the final norm is fused into the (single) layer's combine stage"
    y = _layer(x.reshape(batch * seq, d), batch, seq, norm_mix[0], w_in[0], cmp_pos_k[0],
               cmp_pos_v[0], w_cmp_k1[0], w_cmp_k2[0], w_cmp_v1[0], w_cmp_v2[0], w_gla_alpha[0],
               b_gla_alpha[0], gla_norm[0], w_proj_nsa[0], w_proj_gla[0], w_merge_gate[0],
               b_merge_gate[0], w_out[0], norm_moe[0], w_router[0], b_router[0], w_gate_up[0],
               b_gate_up[0], w_down[0], b_down[0], norm_final[None, :])
    return y.reshape(batch, seq, d)
```

```python
import functools

import jax
import jax.numpy as jnp
import numpy as np
from jax import lax
from jax.experimental import pallas as pl
from jax.experimental.pallas import tpu as pltpu

F32 = jnp.float32
BF16 = jnp.bfloat16

NORM_EPS = 1e-5
ROPE_THETA = 10000.0
NEG_INF = -1e30

NSA_HEADS = 8
NSA_GROUPS = 2
NSA_HPG = NSA_HEADS // NSA_GROUPS
DH = 128
CMP_LEN = 32
CMP_STRIDE = 16
SEL_LEN = 64
SEL_TOPK = 16
SEL_FORCE = 1e3
SEL_MASK = 2.0 ** 100
WINDOW = 512

GLA_HEADS = 4
GLA_DK = 128
GLA_DV = 256
GLA_RANK = 16
GLA_TAU = 16.0
GLA_CHUNK = 64
GLA_GROUP = 4

TOP_K = 4
SWIGLU_LIMIT = 7.0
SWIGLU_ALPHA = 1.702

LANES = 128
VMEM_LIMIT = 56 * 1024 * 1024

PROJ_TM, PROJ_TN = 1024, 512
NSA_TQ, NSA_CK = 256, 512
MIX_TM = 256
ROUTE_TR = 512
MOE_TB = 256
DISPATCH_TD, COMBINE_TC = 256, 128
FFN_TF, FFN_TN = 1024, 2048

NSA_Q = NSA_HEADS * DH
NSA_KV = NSA_GROUPS * DH
GLA_QK = GLA_HEADS * GLA_DK
GLA_V = GLA_HEADS * GLA_DV
MAIN_W = NSA_Q + 6 * NSA_KV + 2 * GLA_QK + 2 * GLA_V
SMALL_W = (NSA_GROUPS + 1) * LANES


def _cparams(sem, vmem=VMEM_LIMIT):
    return pltpu.CompilerParams(dimension_semantics=sem, vmem_limit_bytes=vmem)


def _rope(x, cos, sin_signed):
    return x * cos + pltpu.roll(x, DH // 2, axis=1) * sin_signed


def _proj_kernel(x_ref, gain_ref, w_ref, wsmall_ref, bias_ref, cos_ref, sin_ref,
                 big_ref, small_ref, h_scr, *, rope_tiles):
    j = pl.program_id(1)

    @pl.when(j == 0)
    def _():
        x = x_ref[...]
        ms = jnp.mean(x * x, axis=-1, keepdims=True)
        hb = (x * lax.rsqrt(ms + NORM_EPS) * gain_ref[...]).astype(BF16)
        h_scr[...] = hb
        small_ref[...] = jnp.dot(hb, wsmall_ref[...], preferred_element_type=F32)

    half = w_ref.shape[1] // 2
    h = h_scr[...]
    is_rope = (j == rope_tiles[0]) | (j == rope_tiles[1])
    acc0 = jnp.dot(h, w_ref[:, :half], preferred_element_type=F32) + bias_ref[:, :half]
    acc1 = jnp.dot(h, w_ref[:, half:], preferred_element_type=F32) + bias_ref[:, half:]
    cos = cos_ref[...]
    sin = sin_ref[...]
    roped = jnp.concatenate([_rope(acc0[:, g * DH:(g + 1) * DH], cos, sin)
                             for g in range(NSA_GROUPS)], axis=1)
    big_ref[:, :half] = jnp.where(is_rope, roped, acc0).astype(BF16)
    big_ref[:, half:] = acc1.astype(BF16)


def _proj(x2d, gain, w_all, w_small, bias_all, cos2, sin2, seq, d_merge):
    n, d = x2d.shape
    width = w_all.shape[1]
    tm = min(PROJ_TM, seq)
    tn = PROJ_TN
    n_merge_tiles = d_merge // tn
    rope_tiles = (n_merge_tiles + 3, n_merge_tiles + 4)
    nsb = seq // tm
    kern = functools.partial(_proj_kernel, rope_tiles=rope_tiles)
    return pl.pallas_call(
        kern,
        grid=(n // tm, width // tn),
        in_specs=[
            pl.BlockSpec((tm, d), lambda i, j: (i, 0)),
            pl.BlockSpec((1, d), lambda i, j: (0, 0)),
            pl.BlockSpec((d, tn), lambda i, j: (0, j)),
            pl.BlockSpec((d, SMALL_W), lambda i, j: (0, 0)),
            pl.BlockSpec((1, tn), lambda i, j: (0, j)),
            pl.BlockSpec((tm, DH), lambda i, j: (i % nsb, 0)),
            pl.BlockSpec((tm, DH), lambda i, j: (i % nsb, 0)),
        ],
        out_specs=[
            pl.BlockSpec((tm, tn), lambda i, j: (i, j)),
            pl.BlockSpec((tm, SMALL_W), lambda i, j: (i, 0)),
        ],
        out_shape=[
            jax.ShapeDtypeStruct((n, width), BF16),
            jax.ShapeDtypeStruct((n, SMALL_W), F32),
        ],
        scratch_shapes=[pltpu.VMEM((tm, d), BF16)],
        compiler_params=_cparams(("parallel", "arbitrary")),
        name="proj",
    )(x2d, gain, w_all, w_small, bias_all, cos2, sin2)


def _compress_kernel(kv_ref, pos_ref, w1_ref, w2_ref, out_ref, scr, *, seq, ncp):
    nreal = seq // CMP_STRIDE
    scr[0:seq, :] = kv_ref[...].astype(F32)
    scr[seq:seq + CMP_LEN, :] = jnp.zeros((CMP_LEN, DH), F32)
    acc = jnp.zeros((nreal, w1_ref.shape[1]), F32)
    for l in range(CMP_LEN):
        a = scr[pl.ds(l, nreal, stride=CMP_STRIDE), :] + pos_ref[l:l + 1, :]
        acc = acc + jnp.dot(a.astype(BF16), w1_ref[l * DH:(l + 1) * DH, :],
                            preferred_element_type=F32)
    hid = jax.nn.gelu(acc)
    out = jnp.dot(hid.astype(BF16), w2_ref[...], preferred_element_type=F32)
    row = lax.broadcasted_iota(jnp.int32, out.shape, 0)
    out = jnp.where(row < nreal - 1, out, 0.0).astype(BF16)
    if ncp > nreal:
        out = jnp.concatenate([out, jnp.zeros((ncp - nreal, DH), BF16)], axis=0)
    out_ref[...] = out


def _compress(big, pos, w1, w2, batch, seq, col0):
    ncp = max(seq // CMP_STRIDE, LANES)
    kern = functools.partial(_compress_kernel, seq=seq, ncp=ncp)
    hid = w1.shape[2]
    return pl.pallas_call(
        kern,
        grid=(batch, NSA_GROUPS, 2),
        in_specs=[
            pl.BlockSpec((seq, DH), lambda b, g, t: (b, col0 + 2 * t + g)),
            pl.BlockSpec((None, CMP_LEN, DH), lambda b, g, t: (t, 0, 0)),
            pl.BlockSpec((None, CMP_LEN * DH, hid), lambda b, g, t: (t, 0, 0)),
            pl.BlockSpec((None, hid, DH), lambda b, g, t: (t, 0, 0)),
        ],
        out_specs=pl.BlockSpec((None, None, None, ncp, DH), lambda b, g, t: (b, g, t, 0, 0)),
        out_shape=jax.ShapeDtypeStruct((batch, NSA_GROUPS, 2, ncp, DH), BF16),
        scratch_shapes=[pltpu.VMEM((seq + CMP_LEN, DH), F32)],
        compiler_params=_cparams(("parallel", "parallel", "arbitrary")),
        name="compress",
    )(big, pos, w1, w2)


def _stack_heads(t):
    return jnp.concatenate([t[:, h * DH:(h + 1) * DH] for h in range(NSA_HPG)], axis=0)


def _nsa_kernel(q_ref, kc_ref, vc_ref, ks_ref, vs_ref, kw_ref, vw_ref, cos_ref, sin_ref,
                gate_ref, cover_ref, et_ref, wbias_ref, o_ref, kext_scr, vsext_scr, vwext_scr,
                *, seq, tq, ck):
    i = pl.program_id(2)
    q0 = i * tq
    scale = DH ** -0.5
    ncp = kc_ref.shape[0]
    n_blk = seq // SEL_LEN
    n_sel = min(SEL_TOPK, n_blk)
    wspan = tq + WINDOW
    nt = (((1,), (1,)), ((), ()))

    @pl.when(i == 0)
    def _():
        ones = jnp.ones((seq, DH), BF16)
        kext_scr[:, :DH] = ks_ref[...]
        kext_scr[:, DH:] = et_ref[...]
        vsext_scr[:, :DH] = vs_ref[...]
        vsext_scr[:, DH:] = ones
        vwext_scr[:, :DH] = vw_ref[...]
        vwext_scr[:, DH:] = ones

    q = q_ref[...]
    qs = _stack_heads(q)
    pos_q = q0 + lax.broadcasted_iota(jnp.int32, (tq, 1), 0)
    pos_rows = jnp.concatenate([pos_q] * NSA_HPG, axis=0)

    cos = cos_ref[...]
    sin = sin_ref[...]
    qr = jnp.concatenate(
        [(_rope(q[:, h * DH:(h + 1) * DH].astype(F32), cos, sin) * scale).astype(BF16)
         for h in range(NSA_HPG)], axis=0)

    s = lax.dot_general(qs, kc_ref[...], nt, preferred_element_type=F32) * scale
    n_idx = lax.broadcasted_iota(jnp.int32, (1, ncp), 1)
    cmask = (n_idx * CMP_STRIDE + (CMP_LEN - 1)) <= pos_rows
    s = jnp.where(cmask, s, NEG_INF)
    m = jnp.max(s, axis=-1, keepdims=True)
    e = jnp.exp(s - m)
    p = jnp.where(cmask, e / jnp.sum(e, axis=-1, keepdims=True), 0.0)
    o_cmp = jnp.dot(p.astype(BF16), vc_ref[...], preferred_element_type=F32)

    psum = p[0:tq]
    for h in range(1, NSA_HPG):
        psum = psum + p[h * tq:(h + 1) * tq]
    imp = jnp.dot(psum, cover_ref[...], preferred_element_type=F32,
                  precision=lax.Precision.HIGHEST)
    blk = lax.broadcasted_iota(jnp.int32, (tq, LANES), 1)
    t_blk = pos_q // SEL_LEN
    forced = (blk == 0) | (blk == t_blk) | (blk == t_blk - 1)
    bonus = jnp.where(blk > t_blk, -SEL_FORCE, jnp.where(forced, SEL_FORCE, 0.0))
    val_t = (imp + bonus).T[:n_blk]
    blk_t = lax.broadcasted_iota(jnp.int32, (n_blk, tq), 0)
    terms = []
    for c in range(n_blk):
        vc = val_t[c:c + 1, :]
        beats = (vc > val_t) | ((vc == val_t) & (blk_t > c))
        terms.append(jnp.where(beats, 1.0, 0.0))
    while len(terms) > 1:
        terms = [a + b for a, b in zip(terms[0::2], terms[1::2])]
    rank = terms[0]
    pen_t = jnp.where(rank < n_sel, 0.0, -SEL_MASK)
    pen_t = jnp.concatenate([pen_t, jnp.zeros((LANES - n_blk, tq), F32)], axis=0)
    pen = pen_t.T.astype(BF16)

    q_ext = jnp.concatenate([qr, jnp.concatenate([pen] * NSA_HPG, axis=0)], axis=1)

    n_chunks = (q0 + tq + ck - 1) // ck

    q_heads = [q_ext[h * tq:(h + 1) * tq] for h in range(NSA_HPG)]

    def sel_chunk(c, carry, causal):
        k0 = pl.multiple_of(c * ck, ck)
        kblk = kext_scr[pl.ds(k0, ck), :]
        vblk = vsext_scr[pl.ds(k0, ck), :]
        scs = [lax.dot_general(q_heads[h], kblk, nt, preferred_element_type=F32)
               for h in range(NSA_HPG)]
        out = []
        for h in range(NSA_HPG):
            m_i, acc = carry[h]
            sc = scs[h]
            if causal:
                kp = k0 + lax.broadcasted_iota(jnp.int32, (1, ck), 1)
                sc = jnp.where(kp <= pos_q, sc, NEG_INF)
            m_new = jnp.maximum(m_i, jnp.max(sc, axis=-1, keepdims=True))
            alpha = jnp.exp(m_i - m_new)
            pc = jnp.exp(sc - m_new)
            acc = alpha * acc + jnp.dot(pc.astype(BF16), vblk, preferred_element_type=F32)
            out.append((m_new, acc))
        return tuple(out)

    carry = tuple((jnp.full((tq, 1), NEG_INF, F32), jnp.zeros((tq, 2 * DH), F32))
                  for _ in range(NSA_HPG))
    carry = lax.fori_loop(0, n_chunks - 1, lambda c, cr: sel_chunk(c, cr, False), carry)
    carry = sel_chunk(n_chunks - 1, carry, True)

    kstart = pl.multiple_of(jnp.maximum(q0 - WINDOW, 0), tq)
    kwin = kw_ref[pl.ds(kstart, wspan), :]
    vwin = vwext_scr[pl.ds(kstart, wspan), :]
    wbias = wbias_ref[...]
    sws = [lax.dot_general(qr[h * tq:(h + 1) * tq], kwin, nt, preferred_element_type=F32)
           for h in range(NSA_HPG)]

    gates = jax.nn.sigmoid(gate_ref[...])
    for h in range(NSA_HPG):
        acc_s = carry[h][1]
        o_sel = acc_s[:, :DH] / acc_s[:, DH:]
        sw = sws[h] + wbias
        ew = jnp.exp(sw - jnp.max(sw, axis=-1, keepdims=True))
        acc_w = jnp.dot(ew.astype(BF16), vwin, preferred_element_type=F32)
        o_win = acc_w[:, :DH] / acc_w[:, DH:]
        c0 = 3 * h
        o_h = (gates[:, c0:c0 + 1] * o_cmp[h * tq:(h + 1) * tq] + gates[:, c0 + 1:c0 + 2] * o_sel
               + gates[:, c0 + 2:c0 + 3] * o_win)
        o_ref[:, h * DH:(h + 1) * DH] = o_h.astype(BF16)


def _nsa(big, cmp_kv, small, cos2, sin2, cover, et, batch, seq, main0):
    tq = NSA_TQ
    wbias = _window_bias(tq)
    ck = min(NSA_CK, seq)
    nq = seq // tq
    ncp = cmp_kv.shape[3]
    wspan = tq + WINDOW
    n_wb = wbias.shape[0]
    kern = functools.partial(_nsa_kernel, seq=seq, tq=tq, ck=ck)
    kv_col = (main0 + NSA_Q) // DH

    def kvspec(which):
        return pl.BlockSpec((seq, DH), lambda b, g, i: (b, kv_col + 2 * which + g))

    return pl.pallas_call(
        kern,
        grid=(batch, NSA_GROUPS, nq),
        in_specs=[
            pl.BlockSpec((tq, NSA_HPG * DH), lambda b, g, i: (b * nq + i, main0 // (NSA_HPG * DH) + g)),
            pl.BlockSpec((None, None, None, ncp, DH), lambda b, g, i: (b, g, 0, 0, 0)),
            pl.BlockSpec((None, None, None, ncp, DH), lambda b, g, i: (b, g, 1, 0, 0)),
            kvspec(2), kvspec(3), kvspec(4), kvspec(5),
            pl.BlockSpec((tq, DH), lambda b, g, i: (i, 0)),
            pl.BlockSpec((tq, DH), lambda b, g, i: (i, 0)),
            pl.BlockSpec((tq, LANES), lambda b, g, i: (b * nq + i, g)),
            pl.BlockSpec((ncp, LANES), lambda b, g, i: (0, 0)),
            pl.BlockSpec((seq, LANES), lambda b, g, i: (0, 0)),
            pl.BlockSpec((None, tq, wspan), lambda b, g, i: (jnp.minimum(i, n_wb - 1), 0, 0)),
        ],
        out_specs=pl.BlockSpec((tq, NSA_HPG * DH), lambda b, g, i: (b * nq + i, g)),
        out_shape=jax.ShapeDtypeStruct((batch * seq, NSA_Q), BF16),
        scratch_shapes=[pltpu.VMEM((seq, 2 * DH), BF16), pltpu.VMEM((seq, 2 * DH), BF16),
                        pltpu.VMEM((seq, 2 * DH), BF16)],
        compiler_params=_cparams(("parallel", "parallel", "arbitrary")),
        name="nsa",
    )(big, cmp_kv, cmp_kv, big, big, big, big, cos2, sin2, small, cover, et, wbias)


def _gla_kernel(q_ref, k_ref, v_ref, r_ref, small_ref, wa_ref, ba_ref, gain_ref, tri_ref,
                o_ref, qt_scr, kt_scr, ks_scr, dec_scr, *, seq):
    C = GLA_CHUNK
    nc = seq // C
    GR = GLA_GROUP * C
    nt = (((1,), (1,)), ((), ()))
    z = jnp.dot(small_ref[...], wa_ref[...], preferred_element_type=F32,
                precision=lax.Precision.HIGHEST) + ba_ref[...]
    la = jax.nn.log_sigmoid(z) / GLA_TAU

    la_r = jnp.concatenate([la[c * C:(c + 1) * C, :] for c in range(nc)], axis=1)
    cum_r = jnp.dot(tri_ref[...], la_r, preferred_element_type=F32,
                    precision=lax.Precision.HIGHEST)
    last_r = cum_r[C - 1:C, :]
    qf_r = jnp.exp(cum_r)
    kf_r = jnp.exp(-cum_r)
    sf_r = jnp.exp(last_r - cum_r)
    dec_r = jnp.exp(last_r)
    qscale = GLA_DK ** -0.5
    for c in range(nc):
        rows = slice(c * C, (c + 1) * C)
        cols = slice(c * GLA_DK, (c + 1) * GLA_DK)
        qc = q_ref[rows, :].astype(F32) * qscale
        kc = k_ref[rows, :].astype(F32)
        qt_scr[rows, :] = (qc * qf_r[:, cols]).astype(BF16)
        kt_scr[rows, :] = (kc * kf_r[:, cols]).astype(BF16)
        ks_scr[rows, :] = (kc * sf_r[:, cols]).astype(BF16)
        dec_scr[c:c + 1, :] = dec_r[:, cols]

    ri = lax.broadcasted_iota(jnp.int32, (GR, GR), 0)
    ci = lax.broadcasted_iota(jnp.int32, (GR, GR), 1)
    mask = (ri >= ci) & ((ri // C) == (ci // C))

    def body(g, st):
        r0 = pl.multiple_of(g * GR, GR)
        qt = qt_scr[pl.ds(r0, GR), :]
        kt = kt_scr[pl.ds(r0, GR), :]
        ks = ks_scr[pl.ds(r0, GR), :]
        v = v_ref[pl.ds(r0, GR), :]
        attn = lax.dot_general(qt, kt, nt, preferred_element_type=F32)
        attn = jnp.where(mask, attn, 0.0)
        o_intra = jnp.dot(attn.astype(BF16), v, preferred_element_type=F32)
        outs = []
        for cc in range(GLA_GROUP):
            sl = slice(cc * C, (cc + 1) * C)
            outs.append(o_intra[sl] + lax.dot_general(qt[sl], st.astype(BF16), nt,
                                                      preferred_element_type=F32))
            d_st = lax.dot_general(v[sl], ks[sl], (((0,), (0,)), ((), ())),
                                   preferred_element_type=F32)
            st = st * dec_scr[pl.ds(g * GLA_GROUP + cc, 1), :] + d_st
        o = jnp.concatenate(outs, axis=0)
        o = o * lax.rsqrt(jnp.mean(o * o, axis=-1, keepdims=True) + NORM_EPS)
        o = o * gain_ref[...]
        rr = r_ref[pl.ds(r0, GR), :].astype(F32)
        o_ref[pl.ds(r0, GR), :] = (o * (rr * jax.nn.sigmoid(rr))).astype(BF16)
        return st

    lax.fori_loop(0, seq // GR, body, jnp.zeros((GLA_DV, GLA_DK), F32), unroll=2)


def _gla(big, small, w_alpha, b_alpha, gain, tri, batch, seq, main0):
    qcol = (main0 + NSA_Q + 6 * NSA_KV) // GLA_DK
    kcol = qcol + GLA_QK // GLA_DK
    vcol = (main0 + NSA_Q + 6 * NSA_KV + 2 * GLA_QK) // GLA_DV
    rcol = vcol + GLA_V // GLA_DV
    kern = functools.partial(_gla_kernel, seq=seq)
    return pl.pallas_call(
        kern,
        grid=(batch, GLA_HEADS),
        in_specs=[
            pl.BlockSpec((seq, GLA_DK), lambda b, h: (b, qcol + h)),
            pl.BlockSpec((seq, GLA_DK), lambda b, h: (b, kcol + h)),
            pl.BlockSpec((seq, GLA_DV), lambda b, h: (b, vcol + h)),
            pl.BlockSpec((seq, GLA_DV), lambda b, h: (b, rcol + h)),
            pl.BlockSpec((seq, LANES), lambda b, h: (b, NSA_GROUPS)),
            pl.BlockSpec((LANES, GLA_DK), lambda b, h: (0, h)),
            pl.BlockSpec((1, GLA_DK), lambda b, h: (0, h)),
            pl.BlockSpec((1, GLA_DV), lambda b, h: (0, h)),
            pl.BlockSpec((GLA_CHUNK, GLA_CHUNK), lambda b, h: (0, 0)),
        ],
        out_specs=pl.BlockSpec((seq, GLA_DV), lambda b, h: (b, h)),
        out_shape=jax.ShapeDtypeStruct((batch * seq, GLA_V), BF16),
        scratch_shapes=[pltpu.VMEM((seq, GLA_DK), BF16), pltpu.VMEM((seq, GLA_DK), BF16),
                        pltpu.VMEM((seq, GLA_DK), BF16), pltpu.VMEM((seq // GLA_CHUNK, GLA_DK), F32)],
        compiler_params=_cparams(("parallel", "parallel")),
        name="gla",
    )(big, big, big, big, small, w_alpha, b_alpha, gain, tri)


def _mix_kernel(on_ref, og_ref, ma_ref, mb_ref, x_ref, wpn_ref, wpg_ref, wo_ref, gain_ref,
                wrh_ref, wrl_ref, br_ref, x2_ref, h2_ref, lg_ref):
    a = jnp.dot(on_ref[...], wpn_ref[...], preferred_element_type=F32)
    b = jnp.dot(og_ref[...], wpg_ref[...], preferred_element_type=F32)
    mixed = (jax.nn.sigmoid(ma_ref[...].astype(F32)) * a
             + jax.nn.sigmoid(mb_ref[...].astype(F32)) * b)
    x2 = x_ref[...] + jnp.dot(mixed.astype(BF16), wo_ref[...], preferred_element_type=F32)
    x2_ref[...] = x2
    ms = jnp.mean(x2 * x2, axis=-1, keepdims=True)
    h2 = x2 * lax.rsqrt(ms + NORM_EPS) * gain_ref[...]
    h2_ref[...] = h2
    hi = h2.astype(BF16)
    lo = (h2 - hi.astype(F32)).astype(BF16)
    lg = (jnp.dot(hi, wrh_ref[...], preferred_element_type=F32)
          + jnp.dot(lo, wrh_ref[...], preferred_element_type=F32)
          + jnp.dot(hi, wrl_ref[...], preferred_element_type=F32))
    lg_ref[...] = lg + br_ref[...]


def _mix(o_nsa, o_gla, big, x2d, wpn, wpg, wo, gain, wr_hi, wr_lo, br):
    n, d = x2d.shape
    tm = MIX_TM
    const = lambda i: (0, 0)
    return pl.pallas_call(
        _mix_kernel,
        grid=(n // tm,),
        in_specs=[
            pl.BlockSpec((tm, NSA_Q), lambda i: (i, 0)),
            pl.BlockSpec((tm, GLA_V), lambda i: (i, 0)),
            pl.BlockSpec((tm, d), lambda i: (i, 0)),
            pl.BlockSpec((tm, d), lambda i: (i, 1)),
            pl.BlockSpec((tm, d), lambda i: (i, 0)),
            pl.BlockSpec((NSA_Q, d), const, pipeline_mode=pl.Buffered(1)),
            pl.BlockSpec((GLA_V, d), const, pipeline_mode=pl.Buffered(1)),
            pl.BlockSpec((d, d), const, pipeline_mode=pl.Buffered(1)),
            pl.BlockSpec((1, d), const),
            pl.BlockSpec((d, LANES), const),
            pl.BlockSpec((d, LANES), const),
            pl.BlockSpec((1, LANES), const),
        ],
        out_specs=[
            pl.BlockSpec((tm, d), lambda i: (i, 0)),
            pl.BlockSpec((tm, d), lambda i: (i, 0)),
            pl.BlockSpec((tm, LANES), lambda i: (i, 0)),
        ],
        out_shape=[
            jax.ShapeDtypeStruct((n, d), F32),
            jax.ShapeDtypeStruct((n, d), F32),
            jax.ShapeDtypeStruct((n, LANES), F32),
        ],
        compiler_params=_cparams(("parallel",)),
        name="mix",
    )(o_nsa, o_gla, big, big, x2d, wpn, wpg, wo, gain, wr_hi, wr_lo, br)


def _route_kernel(lg_ref, tri_ref, idx_ref, w_ref, cnt_ref, carry_scr, *, n_exp):
    i = pl.program_id(0)
    tr = lg_ref.shape[0]

    @pl.when(i == 0)
    def _():
        carry_scr[...] = jnp.zeros_like(carry_scr)

    lane = lax.broadcasted_iota(jnp.int32, (tr, LANES), 1)
    work = jnp.where(lane < n_exp, lg_ref[...], -3e38)
    onehots, vals, idxs = [], [], []
    for _ in range(TOP_K):
        mval = jnp.max(work, axis=-1, keepdims=True)
        idx = jnp.min(jnp.where(work == mval, lane, LANES), axis=-1, keepdims=True)
        oh = lane == idx
        work = jnp.where(oh, -3e38, work)
        onehots.append(oh)
        vals.append(mval)
        idxs.append(idx)
    exps = [jnp.exp(v - vals[0]) for v in vals]
    den = exps[0] + exps[1] + exps[2] + exps[3]
    onehot = jnp.zeros((tr, LANES), F32)
    for oh in onehots:
        onehot = onehot + jnp.where(oh, 1.0, 0.0)
    before = jnp.dot(tri_ref[...], onehot.astype(BF16), preferred_element_type=F32) + carry_scr[...]
    carry_scr[...] = carry_scr[...] + jnp.sum(onehot, axis=0, keepdims=True)
    idx_out = jnp.zeros((tr, LANES), jnp.int32)
    w_out = jnp.zeros((tr, LANES), F32)
    for k in range(TOP_K):
        rank_k = jnp.sum(jnp.where(onehots[k], before, 0.0), axis=-1, keepdims=True)
        idx_out = jnp.where(lane == k, idxs[k], idx_out)
        idx_out = jnp.where(lane == TOP_K + k, rank_k.astype(jnp.int32), idx_out)
        w_out = jnp.where(lane == k, exps[k] / den, w_out)
    idx_ref[...] = idx_out
    w_ref[...] = w_out
    cnt_ref[...] = carry_scr[...]


def _route(logits, tri, n_exp):
    n = logits.shape[0]
    tr = tri.shape[0]
    kern = functools.partial(_route_kernel, n_exp=n_exp)
    return pl.pallas_call(
        kern,
        grid=(n // tr,),
        in_specs=[pl.BlockSpec((tr, LANES), lambda i: (i, 0)),
                  pl.BlockSpec((tr, tr), lambda i: (0, 0))],
        out_specs=[pl.BlockSpec((tr, LANES), lambda i: (i, 0)),
                   pl.BlockSpec((tr, LANES), lambda i: (i, 0)),
                   pl.BlockSpec((1, LANES), lambda i: (0, 0))],
        out_shape=[jax.ShapeDtypeStruct((n, LANES), jnp.int32),
                   jax.ShapeDtypeStruct((n, LANES), F32),
                   jax.ShapeDtypeStruct((1, LANES), F32)],
        scratch_shapes=[pltpu.VMEM((1, LANES), F32)],
        compiler_params=_cparams(("arbitrary",)),
        name="route",
    )(logits, tri)


def _dispatch_kernel(dest_ref, padlo_ref, padhi_ref, h_ref, xs_hbm, zbuf, sem, zsem, *, td, tb, n_exp):
    i = pl.program_id(0)
    n_blocks = xs_hbm.shape[0] // tb

    @pl.when(i == 0)
    def _():
        zbuf[...] = jnp.zeros_like(zbuf)

        def row_copy(s):
            return pltpu.make_async_copy(zbuf.at[pl.ds(0, 1)], xs_hbm.at[pl.ds(s, 1)], zsem)

        def per_expert(e, c):
            lo = padlo_ref[e]
            hi = padhi_ref[e]
            lax.fori_loop(lo, hi, lambda s, c2: (row_copy(s).start(), c2)[1], 0)
            lax.fori_loop(lo, hi, lambda s, c2: (row_copy(s).wait(), c2)[1], 0)
            return c

        lax.fori_loop(0, n_exp, per_expert, 0)

        def blk_copy(b):
            return pltpu.make_async_copy(zbuf, xs_hbm.at[pl.ds(b * tb, tb)], zsem)

        first_free = padhi_ref[n_exp - 1] // tb
        lax.fori_loop(first_free, n_blocks, lambda b, c: (blk_copy(b).start(), c)[1], 0)
        lax.fori_loop(first_free, n_blocks, lambda b, c: (blk_copy(b).wait(), c)[1], 0)

    base = i * td * TOP_K

    def start(t, c):
        for k in range(TOP_K):
            pltpu.make_async_copy(h_ref.at[pl.ds(t, 1)],
                                  xs_hbm.at[pl.ds(dest_ref[base + t * TOP_K + k], 1)],
                                  sem).start(priority=k % 2)
        return c

    lax.fori_loop(0, td, start, 0, unroll=2)
    for k in range(TOP_K):
        pltpu.make_async_copy(h_ref, xs_hbm.at[pl.ds(0, td)], sem).wait()


def _dispatch(dest, pad_lo, pad_hi, h2, n_slots, n_exp, tb):
    n, d = h2.shape
    td = DISPATCH_TD
    kern = functools.partial(_dispatch_kernel, td=td, tb=tb, n_exp=n_exp)
    return pl.pallas_call(
        kern,
        grid_spec=pltpu.PrefetchScalarGridSpec(
            num_scalar_prefetch=3,
            grid=(n // td,),
            in_specs=[pl.BlockSpec((td, d), lambda i, dr, lo, hi: (i, 0))],
            out_specs=pl.BlockSpec(memory_space=pl.ANY),
            scratch_shapes=[pltpu.VMEM((tb, d), F32), pltpu.SemaphoreType.DMA, pltpu.SemaphoreType.DMA],
        ),
        out_shape=jax.ShapeDtypeStruct((n_slots, d), F32),
        compiler_params=_cparams(("arbitrary",)),
        name="dispatch",
    )(dest, pad_lo, pad_hi, h2)


def _expert_changed(blk_e_ref, b):
    return (b == 0) | (blk_e_ref[b] != blk_e_ref[jnp.maximum(b - 1, 0)])


def _stream_expert_weights(blk_e_ref, nxt_ref, nvalid_ref, copies, cast):
    j = pl.program_id(0)
    b = pl.program_id(1)
    nvalid = nvalid_ref[0]

    @pl.when((j == 0) & (b == 0))
    def _():
        for c in copies(blk_e_ref[0], 0):
            c.start(priority=1)

    @pl.when((b < nvalid) & _expert_changed(blk_e_ref, b))
    def _():
        for c in copies(blk_e_ref[b], j):
            c.wait()
        cast()
        nxt = nxt_ref[b]
        more = nxt < nvalid

        @pl.when(more)
        def _():
            for c in copies(blk_e_ref[jnp.minimum(nxt, nvalid - 1)], j):
                c.start(priority=1)

        @pl.when(jnp.logical_not(more) & (j + 1 < pl.num_programs(0)))
        def _():
            for c in copies(blk_e_ref[0], j + 1):
                c.start(priority=1)


def _ffn_up_kernel(blk_e_ref, nxt_ref, nvalid_ref, xs_ref, w_hbm, bg_ref, bu_ref, act_ref,
                   wf32, wbf, sem, *, tf, ff):
    b = pl.program_id(1)
    valid = b < nvalid_ref[0]

    def copies(e, j):
        return [pltpu.make_async_copy(w_hbm.at[e, :, pl.ds(pl.multiple_of(h * ff + j * tf, tf), tf)],
                                      wf32.at[h], sem.at[h]) for h in range(2)]

    def cast():
        wbf[...] = wf32[...].astype(BF16)

    _stream_expert_weights(blk_e_ref, nxt_ref, nvalid_ref, copies, cast)

    @pl.when(valid)
    def _():
        x = xs_ref[...].astype(BF16)
        gate = jnp.dot(x, wbf[0], preferred_element_type=F32) + bg_ref[...]
        up = jnp.dot(x, wbf[1], preferred_element_type=F32) + bu_ref[...]
        gate = jnp.minimum(gate, SWIGLU_LIMIT)
        up = jnp.clip(up, -SWIGLU_LIMIT, SWIGLU_LIMIT)
        act_ref[...] = ((up + 1.0) * gate * jax.nn.sigmoid(gate * SWIGLU_ALPHA)).astype(BF16)

    @pl.when(jnp.logical_not(valid))
    def _():
        act_ref[...] = jnp.zeros_like(act_ref)


def _ffn_up(blk_e, nxt, nvalid, xs, w_gate_up, b_gate_up, tb):
    n_slots, d = xs.shape
    n_exp, _, f2 = w_gate_up.shape
    ff = f2 // 2
    tf = min(FFN_TF, ff)
    nf = ff // tf
    nb = n_slots // tb

    def xmap(j, b, be, nx, nv):
        return (jnp.minimum(b, nv[0] - 1), 0)

    kern = functools.partial(_ffn_up_kernel, tf=tf, ff=ff)
    return pl.pallas_call(
        kern,
        grid_spec=pltpu.PrefetchScalarGridSpec(
            num_scalar_prefetch=3,
            grid=(nf, nb),
            in_specs=[
                pl.BlockSpec((tb, d), xmap),
                pl.BlockSpec(memory_space=pl.ANY),
                pl.BlockSpec((None, 1, tf), lambda j, b, be, nx, nv: (be[b], 0, j)),
                pl.BlockSpec((None, 1, tf), lambda j, b, be, nx, nv: (be[b], 0, nf + j)),
            ],
            out_specs=pl.BlockSpec((tb, tf), lambda j, b, be, nx, nv: (b, j)),
            scratch_shapes=[pltpu.VMEM((2, d, tf), F32), pltpu.VMEM((2, d, tf), BF16),
                            pltpu.SemaphoreType.DMA((2,))],
        ),
        out_shape=jax.ShapeDtypeStruct((n_slots, ff), BF16),
        compiler_params=_cparams(("arbitrary", "arbitrary")),
        name="ffn_up",
    )(blk_e, nxt, nvalid, xs, w_gate_up, b_gate_up, b_gate_up)


def _ffn_down_kernel(blk_e_ref, nxt_ref, nvalid_ref, act_ref, w_hbm, bd_ref, out_ref,
                     wf32, wbf, sem, *, tn):
    b = pl.program_id(1)
    valid = b < nvalid_ref[0]

    def copies(e, j):
        return [pltpu.make_async_copy(w_hbm.at[e, :, pl.ds(pl.multiple_of(j * tn, tn), tn)],
                                      wf32, sem.at[0])]

    def cast():
        wbf[...] = wf32[...].astype(BF16)

    _stream_expert_weights(blk_e_ref, nxt_ref, nvalid_ref, copies, cast)

    @pl.when(valid)
    def _():
        out_ref[...] = jnp.dot(act_ref[...], wbf[...], preferred_element_type=F32) + bd_ref[...]

    @pl.when(jnp.logical_not(valid))
    def _():
        out_ref[...] = jnp.zeros_like(out_ref)


def _ffn_down(blk_e, nxt, nvalid, act, w_down, b_down, tb):
    n_slots, ff = act.shape
    n_exp, _, d = w_down.shape
    tn = min(FFN_TN, d)
    nn = d // tn
    nb = n_slots // tb

    def amap(j, b, be, nx, nv):
        return (jnp.minimum(b, nv[0] - 1), 0)

    kern = functools.partial(_ffn_down_kernel, tn=tn)
    return pl.pallas_call(
        kern,
        grid_spec=pltpu.PrefetchScalarGridSpec(
            num_scalar_prefetch=3,
            grid=(nn, nb),
            in_specs=[
                pl.BlockSpec((tb, ff), amap),
                pl.BlockSpec(memory_space=pl.ANY),
                pl.BlockSpec((None, 1, tn), lambda j, b, be, nx, nv: (be[b], 0, j)),
            ],
            out_specs=pl.BlockSpec((tb, tn), lambda j, b, be, nx, nv: (b, j)),
            scratch_shapes=[pltpu.VMEM((ff, tn), F32), pltpu.VMEM((ff, tn), BF16),
                            pltpu.SemaphoreType.DMA((1,))],
        ),
        out_shape=jax.ShapeDtypeStruct((n_slots, d), F32),
        compiler_params=_cparams(("arbitrary", "arbitrary")),
        name="ffn_down",
    )(blk_e, nxt, nvalid, act, w_down, b_down)


def _combine_kernel(dest_ref, outs_hbm, w_ref, x2_ref, gain_ref, y_ref, buf, sem, *, tc):
    i = pl.program_id(0)
    slot = i % 2

    def gather(step, into):
        base = step * tc * TOP_K

        def start(t, c):
            for k in range(TOP_K):
                pltpu.make_async_copy(outs_hbm.at[pl.ds(dest_ref[base + t * TOP_K + k], 1)],
                                      buf.at[into, k, pl.ds(t, 1)], sem.at[into]).start(priority=k % 2)
            return c

        lax.fori_loop(0, tc, start, 0, unroll=2)

    @pl.when(i == 0)
    def _():
        gather(0, 0)

    @pl.when(i + 1 < pl.num_programs(0))
    def _():
        gather(i + 1, 1 - slot)

    for k in range(TOP_K):
        pltpu.make_async_copy(outs_hbm.at[pl.ds(0, tc)], buf.at[slot, k], sem.at[slot]).wait()
    w = w_ref[...]
    y = x2_ref[...]
    for k in range(TOP_K):
        y = y + w[:, k:k + 1] * buf[slot, k]
    ms = jnp.mean(y * y, axis=-1, keepdims=True)
    y_ref[...] = y * lax.rsqrt(ms + NORM_EPS) * gain_ref[...]


def _combine(dest, outs, w_top, x2, gain):
    n, d = x2.shape
    tc = COMBINE_TC
    kern = functools.partial(_combine_kernel, tc=tc)
    return pl.pallas_call(
        kern,
        grid_spec=pltpu.PrefetchScalarGridSpec(
            num_scalar_prefetch=1,
            grid=(n // tc,),
            in_specs=[
                pl.BlockSpec(memory_space=pl.ANY),
                pl.BlockSpec((tc, LANES), lambda i, dr: (i, 0)),
                pl.BlockSpec((tc, d), lambda i, dr: (i, 0)),
                pl.BlockSpec((1, d), lambda i, dr: (0, 0)),
            ],
            out_specs=pl.BlockSpec((tc, d), lambda i, dr: (i, 0)),
            scratch_shapes=[pltpu.VMEM((2, TOP_K, tc, d), F32), pltpu.SemaphoreType.DMA((2,))],
        ),
        out_shape=jax.ShapeDtypeStruct((n, d), F32),
        compiler_params=_cparams(("arbitrary",)),
        name="combine",
    )(dest, outs, w_top, x2, gain)


def _window_bias(tq):
    nv = WINDOW // tq
    r = np.arange(tq)[None, :, None]
    c = np.arange(tq + WINDOW)[None, None, :]
    v = np.arange(nv + 1)[:, None, None]
    q0 = v * tq
    kstart = np.maximum(q0 - WINDOW, 0)
    rel = (kstart + c) - (q0 + r)
    return np.where((rel <= 0) & (rel > -WINDOW), 0.0, NEG_INF).astype(np.float32)


def _rope_tables(seq):
    half = DH // 2
    inv_freq = np.float32(ROPE_THETA) ** (-np.arange(half, dtype=np.float32) / np.float32(half))
    ang = np.arange(seq, dtype=np.float32)[:, None] * inv_freq[None, :].astype(np.float32)
    cos = np.cos(ang.astype(np.float64)).astype(np.float32)
    sin = np.sin(ang.astype(np.float64)).astype(np.float32)
    return np.concatenate([cos, cos], axis=1), np.concatenate([-sin, sin], axis=1)


def _layer(x2d, batch, seq, norm_mix, w_in, cmp_pos_k, cmp_pos_v, w_cmp_k1, w_cmp_k2, w_cmp_v1,
           w_cmp_v2, w_gla_alpha, b_gla_alpha, gla_norm, w_proj_nsa, w_proj_gla, w_merge_gate,
           b_merge_gate, w_out, norm_moe, w_router, b_router, w_gate_up, b_gate_up, w_down, b_down,
           final_gain):
    n, d = x2d.shape
    n_exp = w_router.shape[1]
    d_merge = w_merge_gate.shape[1]
    main0 = d_merge

    o_g = NSA_Q + 6 * NSA_KV
    o_q = o_g + NSA_HEADS * 3
    o_a = o_q + 2 * GLA_QK + 2 * GLA_V
    w_main = jnp.concatenate([w_in[:, :o_g], w_in[:, o_q:o_a]], axis=1)
    w_all = jnp.concatenate([w_merge_gate, w_main], axis=1).astype(BF16)
    n_gl = NSA_HPG * 3
    small_parts = []
    for g in range(NSA_GROUPS):
        small_parts += [w_in[:, o_g + g * n_gl:o_g + (g + 1) * n_gl], jnp.zeros((d, LANES - n_gl), F32)]
    small_parts += [w_in[:, o_a:], jnp.zeros((d, LANES - GLA_RANK), F32)]
    w_small = jnp.concatenate(small_parts, axis=1).astype(BF16)
    bias_all = jnp.concatenate([b_merge_gate, jnp.zeros((MAIN_W,), F32)])[None, :]
    cos2, sin2 = _rope_tables(seq)

    big, small = _proj(x2d, norm_mix[None, :], w_all, w_small, bias_all, cos2, sin2, seq, d_merge)

    pos = jnp.stack([cmp_pos_k, cmp_pos_v])
    w1 = jnp.stack([w_cmp_k1, w_cmp_v1]).astype(BF16)
    w2 = jnp.stack([w_cmp_k2, w_cmp_v2]).astype(BF16)
    cmp_kv = _compress(big, pos, w1, w2, batch, seq, (main0 + NSA_Q) // DH)
    ncp = cmp_kv.shape[3]
    n_blk = seq // SEL_LEN
    cstart = np.arange(ncp)[:, None] * CMP_STRIDE
    blk = np.arange(LANES)[None, :]
    cover = ((cstart < (blk + 1) * SEL_LEN) & (cstart + CMP_LEN > blk * SEL_LEN)
             & (blk < n_blk) & (np.arange(ncp)[:, None] < seq // CMP_STRIDE - 1)).astype(np.float32)
    et = jnp.asarray((np.arange(seq)[:, None] // SEL_LEN) == np.arange(LANES)[None, :], BF16)
    o_nsa = _nsa(big, cmp_kv, small, cos2, sin2, cover, et, batch, seq, main0)

    tri_c = np.tril(np.ones((GLA_CHUNK, GLA_CHUNK), np.float32))
    w_alpha = jnp.concatenate([w_gla_alpha, jnp.zeros((LANES - GLA_RANK, GLA_QK), F32)], axis=0)
    o_gla = _gla(big, small, w_alpha, b_gla_alpha[None, :], gla_norm[None, :], tri_c,
                 batch, seq, main0)

    wr = jnp.concatenate([w_router, jnp.zeros((d, LANES - n_exp), F32)], axis=1)
    wr_hi = wr.astype(BF16)
    wr_lo = (wr - wr_hi.astype(F32)).astype(BF16)
    br = jnp.concatenate([b_router, jnp.zeros((LANES - n_exp,), F32)])[None, :]
    x2, h2, logits = _mix(o_nsa, o_gla, big, x2d, w_proj_nsa.astype(BF16), w_proj_gla.astype(BF16),
                          w_out.astype(BF16), norm_moe[None, :], wr_hi, wr_lo, br)

    tr = ROUTE_TR
    tri_r = jnp.asarray(np.arange(tr)[:, None] > np.arange(tr)[None, :], BF16)
    ridx, w_top, counts = _route(logits, tri_r, n_exp)

    tb = MOE_TB
    nk = n * TOP_K
    n_blocks = -(-nk // tb) + n_exp
    n_slots = n_blocks * tb
    cnt = counts[0, :n_exp].astype(jnp.int32)
    padded = (cnt + tb - 1) // tb * tb
    pad_end = jnp.cumsum(padded)
    pad_start = pad_end - padded
    top_e = ridx[:, :TOP_K]
    e_ids = jnp.arange(n_exp, dtype=jnp.int32)
    start_of = jnp.sum(jnp.where(top_e[:, :, None] == e_ids, pad_start[None, None, :], 0), axis=-1)
    dest = (start_of + ridx[:, TOP_K:2 * TOP_K]).reshape(nk)
    blk_first = jnp.arange(n_blocks, dtype=jnp.int32) * tb
    blk_e = jnp.minimum(jnp.sum((pad_end[None, :] <= blk_first[:, None]).astype(jnp.int32), axis=1),
                        n_exp - 1)
    nxt = (pad_end[blk_e] // tb).astype(jnp.int32)
    nvalid = (pad_end[-1:] // tb).astype(jnp.int32)

    xs = _dispatch(dest, pad_start + cnt, pad_end, h2, n_slots, n_exp, tb)
    act = _ffn_up(blk_e, nxt, nvalid, xs, w_gate_up, b_gate_up[:, None, :], tb)
    outs = _ffn_down(blk_e, nxt, nvalid, act, w_down, b_down[:, None, :], tb)
    return _combine(dest, outs, w_top, x2, final_gain)


def kernel(x, norm_mix, w_in, cmp_pos_k, cmp_pos_v, w_cmp_k1, w_cmp_k2, w_cmp_v1, w_cmp_v2,
           w_gla_alpha, b_gla_alpha, gla_norm, w_proj_nsa, w_proj_gla, w_merge_gate, b_merge_gate,
           w_out, norm_moe, w_router, b_router, w_gate_up, b_gate_up, w_down, b_down, norm_final):
    batch, seq, d = x.shape
    depth = w_in.shape[0]
    assert depth == 1, "the final norm is fused into the (single) layer's combine stage"
    y = _layer(x.reshape(batch * seq, d), batch, seq, norm_mix[0], w_in[0], cmp_pos_k[0],
               cmp_pos_v[0], w_cmp_k1[0], w_cmp_k2[0], w_cmp_v1[0], w_cmp_v2[0], w_gla_alpha[0],
               b_gla_alpha[0], gla_norm[0], w_proj_nsa[0], w_proj_gla[0], w_merge_gate[0],
               b_merge_gate[0], w_out[0], norm_moe[0], w_router[0], b_router[0], w_gate_up[0],
               b_gate_up[0], w_down[0], b_down[0], norm_final[None, :])
    return y.reshape(batch, seq, d)
```

```python
import functools

import jax
import jax.numpy as jnp
import numpy as np
from jax import lax
from jax.experimental import pallas as pl
from jax.experimental.pallas import tpu as pltpu

F32 = jnp.float32
BF16 = jnp.bfloat16

NORM_EPS = 1e-5
ROPE_THETA = 10000.0
NEG_INF = -1e30

NSA_HEADS = 8
NSA_GROUPS = 2
NSA_HPG = NSA_HEADS // NSA_GROUPS
DH = 128
CMP_LEN = 32
CMP_STRIDE = 16
SEL_LEN = 64
SEL_TOPK = 16
SEL_FORCE = 1e3
SEL_MASK = 2.0 ** 100
WINDOW = 512

GLA_HEADS = 4
GLA_DK = 128
GLA_DV = 256
GLA_RANK = 16
GLA_TAU = 16.0
GLA_CHUNK = 64
GLA_GROUP = 4

TOP_K = 4
SWIGLU_LIMIT = 7.0
SWIGLU_ALPHA = 1.702

LANES = 128
VMEM_LIMIT = 56 * 1024 * 1024

PROJ_TM, PROJ_TN = 1024, 512
NSA_TQ, NSA_CK = 256, 512
MIX_TM = 256
ROUTE_TR = 512
MOE_TB = 256
DISPATCH_TD, COMBINE_TC = 512, 256
FFN_TF, FFN_TN = 1024, 2048

NSA_Q = NSA_HEADS * DH
NSA_KV = NSA_GROUPS * DH
GLA_QK = GLA_HEADS * GLA_DK
GLA_V = GLA_HEADS * GLA_DV
MAIN_W = NSA_Q + 6 * NSA_KV + 2 * GLA_QK + 2 * GLA_V
SMALL_W = (NSA_GROUPS + 1) * LANES


def _cparams(sem, vmem=VMEM_LIMIT):
    return pltpu.CompilerParams(dimension_semantics=sem, vmem_limit_bytes=vmem)


def _rope(x, cos, sin_signed):
    return x * cos + pltpu.roll(x, DH // 2, axis=1) * sin_signed


def _proj_kernel(x_ref, gain_ref, w_ref, wsmall_ref, bias_ref, cos_ref, sin_ref,
                 big_ref, small_ref, h_scr, *, rope_tiles):
    j = pl.program_id(1)

    @pl.when(j == 0)
    def _():
        x = x_ref[...]
        ms = jnp.mean(x * x, axis=-1, keepdims=True)
        hb = (x * lax.rsqrt(ms + NORM_EPS) * gain_ref[...]).astype(BF16)
        h_scr[...] = hb
        small_ref[...] = jnp.dot(hb, wsmall_ref[...], preferred_element_type=F32)

    half = w_ref.shape[1] // 2
    h = h_scr[...]
    is_rope = (j == rope_tiles[0]) | (j == rope_tiles[1])
    acc0 = jnp.dot(h, w_ref[:, :half], preferred_element_type=F32) + bias_ref[:, :half]
    acc1 = jnp.dot(h, w_ref[:, half:], preferred_element_type=F32) + bias_ref[:, half:]
    cos = cos_ref[...]
    sin = sin_ref[...]
    roped = jnp.concatenate([_rope(acc0[:, g * DH:(g + 1) * DH], cos, sin)
                             for g in range(NSA_GROUPS)], axis=1)
    big_ref[:, :half] = jnp.where(is_rope, roped, acc0).astype(BF16)
    big_ref[:, half:] = acc1.astype(BF16)


def _proj(x2d, gain, w_all, w_small, bias_all, cos2, sin2, seq, d_merge):
    n, d = x2d.shape
    width = w_all.shape[1]
    tm = min(PROJ_TM, seq)
    tn = PROJ_TN
    n_merge_tiles = d_merge // tn
    rope_tiles = (n_merge_tiles + 3, n_merge_tiles + 4)
    nsb = seq // tm
    kern = functools.partial(_proj_kernel, rope_tiles=rope_tiles)
    return pl.pallas_call(
        kern,
        grid=(n // tm, width // tn),
        in_specs=[
            pl.BlockSpec((tm, d), lambda i, j: (i, 0)),
            pl.BlockSpec((1, d), lambda i, j: (0, 0)),
            pl.BlockSpec((d, tn), lambda i, j: (0, j)),
            pl.BlockSpec((d, SMALL_W), lambda i, j: (0, 0)),
            pl.BlockSpec((1, tn), lambda i, j: (0, j)),
            pl.BlockSpec((tm, DH), lambda i, j: (i % nsb, 0)),
            pl.BlockSpec((tm, DH), lambda i, j: (i % nsb, 0)),
        ],
        out_specs=[
            pl.BlockSpec((tm, tn), lambda i, j: (i, j)),
            pl.BlockSpec((tm, SMALL_W), lambda i, j: (i, 0)),
        ],
        out_shape=[
            jax.ShapeDtypeStruct((n, width), BF16),
            jax.ShapeDtypeStruct((n, SMALL_W), F32),
        ],
        scratch_shapes=[pltpu.VMEM((tm, d), BF16)],
        compiler_params=_cparams(("parallel", "arbitrary")),
        name="proj",
    )(x2d, gain, w_all, w_small, bias_all, cos2, sin2)


def _compress_kernel(kv_ref, pos_ref, w1_ref, w2_ref, out_ref, scr, *, seq, ncp):
    nreal = seq // CMP_STRIDE
    scr[0:seq, :] = kv_ref[...].astype(F32)
    scr[seq:seq + CMP_LEN, :] = jnp.zeros((CMP_LEN, DH), F32)
    acc = jnp.zeros((nreal, w1_ref.shape[1]), F32)
    for l in range(CMP_LEN):
        a = scr[pl.ds(l, nreal, stride=CMP_STRIDE), :] + pos_ref[l:l + 1, :]
        acc = acc + jnp.dot(a.astype(BF16), w1_ref[l * DH:(l + 1) * DH, :],
                            preferred_element_type=F32)
    hid = jax.nn.gelu(acc)
    out = jnp.dot(hid.astype(BF16), w2_ref[...], preferred_element_type=F32)
    row = lax.broadcasted_iota(jnp.int32, out.shape, 0)
    out = jnp.where(row < nreal - 1, out, 0.0).astype(BF16)
    if ncp > nreal:
        out = jnp.concatenate([out, jnp.zeros((ncp - nreal, DH), BF16)], axis=0)
    out_ref[...] = out


def _compress(big, pos, w1, w2, batch, seq, col0):
    ncp = max(seq // CMP_STRIDE, LANES)
    kern = functools.partial(_compress_kernel, seq=seq, ncp=ncp)
    hid = w1.shape[2]
    return pl.pallas_call(
        kern,
        grid=(batch, NSA_GROUPS, 2),
        in_specs=[
            pl.BlockSpec((seq, DH), lambda b, g, t: (b, col0 + 2 * t + g)),
            pl.BlockSpec((None, CMP_LEN, DH), lambda b, g, t: (t, 0, 0)),
            pl.BlockSpec((None, CMP_LEN * DH, hid), lambda b, g, t: (t, 0, 0)),
            pl.BlockSpec((None, hid, DH), lambda b, g, t: (t, 0, 0)),
        ],
        out_specs=pl.BlockSpec((None, None, None, ncp, DH), lambda b, g, t: (b, g, t, 0, 0)),
        out_shape=jax.ShapeDtypeStruct((batch, NSA_GROUPS, 2, ncp, DH), BF16),
        scratch_shapes=[pltpu.VMEM((seq + CMP_LEN, DH), F32)],
        compiler_params=_cparams(("parallel", "parallel", "arbitrary")),
        name="compress",
    )(big, pos, w1, w2)


def _stack_heads(t):
    return jnp.concatenate([t[:, h * DH:(h + 1) * DH] for h in range(NSA_HPG)], axis=0)


def _nsa_kernel(q_ref, kc_ref, vc_ref, ks_ref, vs_ref, kw_ref, vw_ref, cos_ref, sin_ref,
                gate_ref, cover_ref, et_ref, wbias_ref, o_ref, kext_scr, vsext_scr, vwext_scr,
                *, seq, tq, ck):
    i = pl.program_id(2)
    q0 = i * tq
    scale = DH ** -0.5
    ncp = kc_ref.shape[0]
    n_blk = seq // SEL_LEN
    n_sel = min(SEL_TOPK, n_blk)
    wspan = tq + WINDOW
    nt = (((1,), (1,)), ((), ()))

    @pl.when(i == 0)
    def _():
        ones = jnp.ones((seq, DH), BF16)
        kext_scr[:, :DH] = ks_ref[...]
        kext_scr[:, DH:] = et_ref[...]
        vsext_scr[:, :DH] = vs_ref[...]
        vsext_scr[:, DH:] = ones
        vwext_scr[:, :DH] = vw_ref[...]
        vwext_scr[:, DH:] = ones

    q = q_ref[...]
    qs = _stack_heads(q)
    pos_q = q0 + lax.broadcasted_iota(jnp.int32, (tq, 1), 0)
    pos_rows = jnp.concatenate([pos_q] * NSA_HPG, axis=0)

    cos = cos_ref[...]
    sin = sin_ref[...]
    qr = jnp.concatenate(
        [(_rope(q[:, h * DH:(h + 1) * DH].astype(F32), cos, sin) * scale).astype(BF16)
         for h in range(NSA_HPG)], axis=0)

    s = lax.dot_general(qs, kc_ref[...], nt, preferred_element_type=F32) * scale
    n_idx = lax.broadcasted_iota(jnp.int32, (1, ncp), 1)
    cmask = (n_idx * CMP_STRIDE + (CMP_LEN - 1)) <= pos_rows
    s = jnp.where(cmask, s, NEG_INF)
    m = jnp.max(s, axis=-1, keepdims=True)
    e = jnp.exp(s - m)
    p = jnp.where(cmask, e / jnp.sum(e, axis=-1, keepdims=True), 0.0)
    o_cmp = jnp.dot(p.astype(BF16), vc_ref[...], preferred_element_type=F32)

    psum = p[0:tq]
    for h in range(1, NSA_HPG):
        psum = psum + p[h * tq:(h + 1) * tq]
    imp = jnp.dot(psum, cover_ref[...], preferred_element_type=F32,
                  precision=lax.Precision.HIGHEST)
    blk = lax.broadcasted_iota(jnp.int32, (tq, LANES), 1)
    t_blk = pos_q // SEL_LEN
    forced = (blk == 0) | (blk == t_blk) | (blk == t_blk - 1)
    bonus = jnp.where(blk > t_blk, -SEL_FORCE, jnp.where(forced, SEL_FORCE, 0.0))
    val_t = (imp + bonus).T[:n_blk]
    blk_t = lax.broadcasted_iota(jnp.int32, (n_blk, tq), 0)
    terms = []
    for c in range(n_blk):
        vc = val_t[c:c + 1, :]
        beats = (vc > val_t) | ((vc == val_t) & (blk_t > c))
        terms.append(jnp.where(beats, 1.0, 0.0))
    while len(terms) > 1:
        terms = [a + b for a, b in zip(terms[0::2], terms[1::2])]
    rank = terms[0]
    pen_t = jnp.where(rank < n_sel, 0.0, -SEL_MASK)
    pen_t = jnp.concatenate([pen_t, jnp.zeros((LANES - n_blk, tq), F32)], axis=0)
    pen = pen_t.T.astype(BF16)

    q_ext = jnp.concatenate([qr, jnp.concatenate([pen] * NSA_HPG, axis=0)], axis=1)

    n_chunks = (q0 + tq + ck - 1) // ck

    q_heads = [q_ext[h * tq:(h + 1) * tq] for h in range(NSA_HPG)]

    def sel_chunk(c, carry, causal):
        k0 = pl.multiple_of(c * ck, ck)
        kblk = kext_scr[pl.ds(k0, ck), :]
        vblk = vsext_scr[pl.ds(k0, ck), :]
        scs = [lax.dot_general(q_heads[h], kblk, nt, preferred_element_type=F32)
               for h in range(NSA_HPG)]
        out = []
        for h in range(NSA_HPG):
            m_i, acc = carry[h]
            sc = scs[h]
            if causal:
                kp = k0 + lax.broadcasted_iota(jnp.int32, (1, ck), 1)
                sc = jnp.where(kp <= pos_q, sc, NEG_INF)
            m_new = jnp.maximum(m_i, jnp.max(sc, axis=-1, keepdims=True))
            alpha = jnp.exp(m_i - m_new)
            pc = jnp.exp(sc - m_new)
            acc = alpha * acc + jnp.dot(pc.astype(BF16), vblk, preferred_element_type=F32)
            out.append((m_new, acc))
        return tuple(out)

    carry = tuple((jnp.full((tq, 1), NEG_INF, F32), jnp.zeros((tq, 2 * DH), F32))
                  for _ in range(NSA_HPG))
    carry = lax.fori_loop(0, n_chunks - 1, lambda c, cr: sel_chunk(c, cr, False), carry)
    carry = sel_chunk(n_chunks - 1, carry, True)

    kstart = pl.multiple_of(jnp.maximum(q0 - WINDOW, 0), tq)
    kwin = kw_ref[pl.ds(kstart, wspan), :]
    vwin = vwext_scr[pl.ds(kstart, wspan), :]
    wbias = wbias_ref[...]
    sws = [lax.dot_general(qr[h * tq:(h + 1) * tq], kwin, nt, preferred_element_type=F32)
           for h in range(NSA_HPG)]

    gates = jax.nn.sigmoid(gate_ref[...])
    for h in range(NSA_HPG):
        acc_s = carry[h][1]
        o_sel = acc_s[:, :DH] / acc_s[:, DH:]
        sw = sws[h] + wbias
        ew = jnp.exp(sw - jnp.max(sw, axis=-1, keepdims=True))
        acc_w = jnp.dot(ew.astype(BF16), vwin, preferred_element_type=F32)
        o_win = acc_w[:, :DH] / acc_w[:, DH:]
        c0 = 3 * h
        o_h = (gates[:, c0:c0 + 1] * o_cmp[h * tq:(h + 1) * tq] + gates[:, c0 + 1:c0 + 2] * o_sel
               + gates[:, c0 + 2:c0 + 3] * o_win)
        o_ref[:, h * DH:(h + 1) * DH] = o_h.astype(BF16)


def _nsa(big, cmp_kv, small, cos2, sin2, cover, et, batch, seq, main0):
    tq = NSA_TQ
    wbias = _window_bias(tq)
    ck = min(NSA_CK, seq)
    nq = seq // tq
    ncp = cmp_kv.shape[3]
    wspan = tq + WINDOW
    n_wb = wbias.shape[0]
    kern = functools.partial(_nsa_kernel, seq=seq, tq=tq, ck=ck)
    kv_col = (main0 + NSA_Q) // DH

    def kvspec(which):
        return pl.BlockSpec((seq, DH), lambda b, g, i: (b, kv_col + 2 * which + g))

    return pl.pallas_call(
        kern,
        grid=(batch, NSA_GROUPS, nq),
        in_specs=[
            pl.BlockSpec((tq, NSA_HPG * DH), lambda b, g, i: (b * nq + i, main0 // (NSA_HPG * DH) + g)),
            pl.BlockSpec((None, None, None, ncp, DH), lambda b, g, i: (b, g, 0, 0, 0)),
            pl.BlockSpec((None, None, None, ncp, DH), lambda b, g, i: (b, g, 1, 0, 0)),
            kvspec(2), kvspec(3), kvspec(4), kvspec(5),
            pl.BlockSpec((tq, DH), lambda b, g, i: (i, 0)),
            pl.BlockSpec((tq, DH), lambda b, g, i: (i, 0)),
            pl.BlockSpec((tq, LANES), lambda b, g, i: (b * nq + i, g)),
            pl.BlockSpec((ncp, LANES), lambda b, g, i: (0, 0)),
            pl.BlockSpec((seq, LANES), lambda b, g, i: (0, 0)),
            pl.BlockSpec((None, tq, wspan), lambda b, g, i: (jnp.minimum(i, n_wb - 1), 0, 0)),
        ],
        out_specs=pl.BlockSpec((tq, NSA_HPG * DH), lambda b, g, i: (b * nq + i, g)),
        out_shape=jax.ShapeDtypeStruct((batch * seq, NSA_Q), BF16),
        scratch_shapes=[pltpu.VMEM((seq, 2 * DH), BF16), pltpu.VMEM((seq, 2 * DH), BF16),
                        pltpu.VMEM((seq, 2 * DH), BF16)],
        compiler_params=_cparams(("parallel", "parallel", "arbitrary")),
        name="nsa",
    )(big, cmp_kv, cmp_kv, big, big, big, big, cos2, sin2, small, cover, et, wbias)


def _gla_kernel(q_ref, k_ref, v_ref, r_ref, small_ref, wa_ref, ba_ref, gain_ref, tri_ref,
                o_ref, qt_scr, kt_scr, ks_scr, dec_scr, *, seq):
    C = GLA_CHUNK
    nc = seq // C
    GR = GLA_GROUP * C
    nt = (((1,), (1,)), ((), ()))
    z = jnp.dot(small_ref[...], wa_ref[...], preferred_element_type=F32,
                precision=lax.Precision.HIGHEST) + ba_ref[...]
    la = jax.nn.log_sigmoid(z) / GLA_TAU

    la_r = jnp.concatenate([la[c * C:(c + 1) * C, :] for c in range(nc)], axis=1)
    cum_r = jnp.dot(tri_ref[...], la_r, preferred_element_type=F32,
                    precision=lax.Precision.HIGHEST)
    last_r = cum_r[C - 1:C, :]
    qf_r = jnp.exp(cum_r)
    kf_r = jnp.exp(-cum_r)
    sf_r = jnp.exp(last_r - cum_r)
    dec_r = jnp.exp(last_r)
    qscale = GLA_DK ** -0.5
    for c in range(nc):
        rows = slice(c * C, (c + 1) * C)
        cols = slice(c * GLA_DK, (c + 1) * GLA_DK)
        qc = q_ref[rows, :].astype(F32) * qscale
        kc = k_ref[rows, :].astype(F32)
        qt_scr[rows, :] = (qc * qf_r[:, cols]).astype(BF16)
        kt_scr[rows, :] = (kc * kf_r[:, cols]).astype(BF16)
        ks_scr[rows, :] = (kc * sf_r[:, cols]).astype(BF16)
        dec_scr[c:c + 1, :] = dec_r[:, cols]

    ri = lax.broadcasted_iota(jnp.int32, (GR, GR), 0)
    ci = lax.broadcasted_iota(jnp.int32, (GR, GR), 1)
    mask = (ri >= ci) & ((ri // C) == (ci // C))

    def body(g, st):
        r0 = pl.multiple_of(g * GR, GR)
        qt = qt_scr[pl.ds(r0, GR), :]
        kt = kt_scr[pl.ds(r0, GR), :]
        ks = ks_scr[pl.ds(r0, GR), :]
        v = v_ref[pl.ds(r0, GR), :]
        attn = lax.dot_general(qt, kt, nt, preferred_element_type=F32)
        attn = jnp.where(mask, attn, 0.0)
        o_intra = jnp.dot(attn.astype(BF16), v, preferred_element_type=F32)
        outs = []
        for cc in range(GLA_GROUP):
            sl = slice(cc * C, (cc + 1) * C)
            outs.append(o_intra[sl] + lax.dot_general(qt[sl], st.astype(BF16), nt,
                                                      preferred_element_type=F32))
            d_st = lax.dot_general(v[sl], ks[sl], (((0,), (0,)), ((), ())),
                                   preferred_element_type=F32)
            st = st * dec_scr[pl.ds(g * GLA_GROUP + cc, 1), :] + d_st
        o = jnp.concatenate(outs, axis=0)
        o = o * lax.rsqrt(jnp.mean(o * o, axis=-1, keepdims=True) + NORM_EPS)
        o = o * gain_ref[...]
        rr = r_ref[pl.ds(r0, GR), :].astype(F32)
        o_ref[pl.ds(r0, GR), :] = (o * (rr * jax.nn.sigmoid(rr))).astype(BF16)
        return st

    lax.fori_loop(0, seq // GR, body, jnp.zeros((GLA_DV, GLA_DK), F32), unroll=2)


def _gla(big, small, w_alpha, b_alpha, gain, tri, batch, seq, main0):
    qcol = (main0 + NSA_Q + 6 * NSA_KV) // GLA_DK
    kcol = qcol + GLA_QK // GLA_DK
    vcol = (main0 + NSA_Q + 6 * NSA_KV + 2 * GLA_QK) // GLA_DV
    rcol = vcol + GLA_V // GLA_DV
    kern = functools.partial(_gla_kernel, seq=seq)
    return pl.pallas_call(
        kern,
        grid=(batch, GLA_HEADS),
        in_specs=[
            pl.BlockSpec((seq, GLA_DK), lambda b, h: (b, qcol + h)),
            pl.BlockSpec((seq, GLA_DK), lambda b, h: (b, kcol + h)),
            pl.BlockSpec((seq, GLA_DV), lambda b, h: (b, vcol + h)),
            pl.BlockSpec((seq, GLA_DV), lambda b, h: (b, rcol + h)),
            pl.BlockSpec((seq, LANES), lambda b, h: (b, NSA_GROUPS)),
            pl.BlockSpec((LANES, GLA_DK), lambda b, h: (0, h)),
            pl.BlockSpec((1, GLA_DK), lambda b, h: (0, h)),
            pl.BlockSpec((1, GLA_DV), lambda b, h: (0, h)),
            pl.BlockSpec((GLA_CHUNK, GLA_CHUNK), lambda b, h: (0, 0)),
        ],
        out_specs=pl.BlockSpec((seq, GLA_DV), lambda b, h: (b, h)),
        out_shape=jax.ShapeDtypeStruct((batch * seq, GLA_V), BF16),
        scratch_shapes=[pltpu.VMEM((seq, GLA_DK), BF16), pltpu.VMEM((seq, GLA_DK), BF16),
                        pltpu.VMEM((seq, GLA_DK), BF16), pltpu.VMEM((seq // GLA_CHUNK, GLA_DK), F32)],
        compiler_params=_cparams(("parallel", "parallel")),
        name="gla",
    )(big, big, big, big, small, w_alpha, b_alpha, gain, tri)


def _mix_kernel(on_ref, og_ref, ma_ref, mb_ref, x_ref, wpn_ref, wpg_ref, wo_ref, gain_ref,
                wrh_ref, wrl_ref, br_ref, x2_ref, h2_ref, lg_ref):
    a = jnp.dot(on_ref[...], wpn_ref[...], preferred_element_type=F32)
    b = jnp.dot(og_ref[...], wpg_ref[...], preferred_element_type=F32)
    mixed = (jax.nn.sigmoid(ma_ref[...].astype(F32)) * a
             + jax.nn.sigmoid(mb_ref[...].astype(F32)) * b)
    x2 = x_ref[...] + jnp.dot(mixed.astype(BF16), wo_ref[...], preferred_element_type=F32)
    x2_ref[...] = x2
    ms = jnp.mean(x2 * x2, axis=-1, keepdims=True)
    h2 = x2 * lax.rsqrt(ms + NORM_EPS) * gain_ref[...]
    h2_ref[...] = h2
    hi = h2.astype(BF16)
    lo = (h2 - hi.astype(F32)).astype(BF16)
    lg = (jnp.dot(hi, wrh_ref[...], preferred_element_type=F32)
          + jnp.dot(lo, wrh_ref[...], preferred_element_type=F32)
          + jnp.dot(hi, wrl_ref[...], preferred_element_type=F32))
    lg_ref[...] = lg + br_ref[...]


def _mix(o_nsa, o_gla, big, x2d, wpn, wpg, wo, gain, wr_hi, wr_lo, br):
    n, d = x2d.shape
    tm = MIX_TM
    const = lambda i: (0, 0)
    return pl.pallas_call(
        _mix_kernel,
        grid=(n // tm,),
        in_specs=[
            pl.BlockSpec((tm, NSA_Q), lambda i: (i, 0)),
            pl.BlockSpec((tm, GLA_V), lambda i: (i, 0)),
            pl.BlockSpec((tm, d), lambda i: (i, 0)),
            pl.BlockSpec((tm, d), lambda i: (i, 1)),
            pl.BlockSpec((tm, d), lambda i: (i, 0)),
            pl.BlockSpec((NSA_Q, d), const, pipeline_mode=pl.Buffered(1)),
            pl.BlockSpec((GLA_V, d), const, pipeline_mode=pl.Buffered(1)),
            pl.BlockSpec((d, d), const, pipeline_mode=pl.Buffered(1)),
            pl.BlockSpec((1, d), const),
            pl.BlockSpec((d, LANES), const),
            pl.BlockSpec((d, LANES), const),
            pl.BlockSpec((1, LANES), const),
        ],
        out_specs=[
            pl.BlockSpec((tm, d), lambda i: (i, 0)),
            pl.BlockSpec((tm, d), lambda i: (i, 0)),
            pl.BlockSpec((tm, LANES), lambda i: (i, 0)),
        ],
        out_shape=[
            jax.ShapeDtypeStruct((n, d), F32),
            jax.ShapeDtypeStruct((n, d), F32),
            jax.ShapeDtypeStruct((n, LANES), F32),
        ],
        compiler_params=_cparams(("parallel",)),
        name="mix",
    )(o_nsa, o_gla, big, big, x2d, wpn, wpg, wo, gain, wr_hi, wr_lo, br)


def _route_kernel(lg_ref, tri_ref, idx_ref, w_ref, cnt_ref, carry_scr, *, n_exp):
    i = pl.program_id(0)
    tr = lg_ref.shape[0]

    @pl.when(i == 0)
    def _():
        carry_scr[...] = jnp.zeros_like(carry_scr)

    lane = lax.broadcasted_iota(jnp.int32, (tr, LANES), 1)
    work = jnp.where(lane < n_exp, lg_ref[...], -3e38)
    onehots, vals, idxs = [], [], []
    for _ in range(TOP_K):
        mval = jnp.max(work, axis=-1, keepdims=True)
        idx = jnp.min(jnp.where(work == mval, lane, LANES), axis=-1, keepdims=True)
        oh = lane == idx
        work = jnp.where(oh, -3e38, work)
        onehots.append(oh)
        vals.append(mval)
        idxs.append(idx)
    exps = [jnp.exp(v - vals[0]) for v in vals]
    den = exps[0] + exps[1] + exps[2] + exps[3]
    onehot = jnp.zeros((tr, LANES), F32)
    for oh in onehots:
        onehot = onehot + jnp.where(oh, 1.0, 0.0)
    before = jnp.dot(tri_ref[...], onehot.astype(BF16), preferred_element_type=F32) + carry_scr[...]
    carry_scr[...] = carry_scr[...] + jnp.sum(onehot, axis=0, keepdims=True)
    idx_out = jnp.zeros((tr, LANES), jnp.int32)
    w_out = jnp.zeros((tr, LANES), F32)
    for k in range(TOP_K):
        rank_k = jnp.sum(jnp.where(onehots[k], before, 0.0), axis=-1, keepdims=True)
        idx_out = jnp.where(lane == k, idxs[k], idx_out)
        idx_out = jnp.where(lane == TOP_K + k, rank_k.astype(jnp.int32), idx_out)
        w_out = jnp.where(lane == k, exps[k] / den, w_out)
    idx_ref[...] = idx_out
    w_ref[...] = w_out
    cnt_ref[...] = carry_scr[...]


def _route(logits, tri, n_exp):
    n = logits.shape[0]
    tr = tri.shape[0]
    kern = functools.partial(_route_kernel, n_exp=n_exp)
    return pl.pallas_call(
        kern,
        grid=(n // tr,),
        in_specs=[pl.BlockSpec((tr, LANES), lambda i: (i, 0)),
                  pl.BlockSpec((tr, tr), lambda i: (0, 0))],
        out_specs=[pl.BlockSpec((tr, LANES), lambda i: (i, 0)),
                   pl.BlockSpec((tr, LANES), lambda i: (i, 0)),
                   pl.BlockSpec((1, LANES), lambda i: (0, 0))],
        out_shape=[jax.ShapeDtypeStruct((n, LANES), jnp.int32),
                   jax.ShapeDtypeStruct((n, LANES), F32),
                   jax.ShapeDtypeStruct((1, LANES), F32)],
        scratch_shapes=[pltpu.VMEM((1, LANES), F32)],
        compiler_params=_cparams(("arbitrary",)),
        name="route",
    )(logits, tri)


def _dispatch_kernel(dest_ref, padlo_ref, padhi_ref, h_ref, xs_hbm, zbuf, sem, zsem, *, td, tb, n_exp):
    i = pl.program_id(0)
    n_blocks = xs_hbm.shape[0] // tb

    def row_copy(s):
        return pltpu.make_async_copy(zbuf.at[pl.ds(0, 1)], xs_hbm.at[pl.ds(s, 1)], zsem)

    def blk_copy(b):
        return pltpu.make_async_copy(zbuf, xs_hbm.at[pl.ds(b * tb, tb)], zsem)

    def zero_fill(start):
        def per_expert(e, c):
            def row(s, c2):
                if start:
                    row_copy(s).start()
                else:
                    row_copy(s).wait()
                return c2

            lax.fori_loop(padlo_ref[e], padhi_ref[e], row, 0)
            return c

        lax.fori_loop(0, n_exp, per_expert, 0)

        def blk(b, c):
            if start:
                blk_copy(b).start()
            else:
                blk_copy(b).wait()
            return c

        lax.fori_loop(padhi_ref[n_exp - 1] // tb, n_blocks, blk, 0)

    @pl.when(i == 0)
    def _():
        zbuf[...] = jnp.zeros_like(zbuf)
        zero_fill(start=True)

    base = i * td * TOP_K

    def start(t, c):
        for k in range(TOP_K):
            pltpu.make_async_copy(h_ref.at[pl.ds(t, 1)],
                                  xs_hbm.at[pl.ds(dest_ref[base + t * TOP_K + k], 1)],
                                  sem).start(priority=k % 2)
        return c

    lax.fori_loop(0, td, start, 0, unroll=2)
    for k in range(TOP_K):
        pltpu.make_async_copy(h_ref, xs_hbm.at[pl.ds(0, td)], sem).wait()

    @pl.when(i == 0)
    def _():
        zero_fill(start=False)


def _dispatch(dest, pad_lo, pad_hi, h2, n_slots, n_exp, tb):
    n, d = h2.shape
    td = DISPATCH_TD
    kern = functools.partial(_dispatch_kernel, td=td, tb=tb, n_exp=n_exp)
    return pl.pallas_call(
        kern,
        grid_spec=pltpu.PrefetchScalarGridSpec(
            num_scalar_prefetch=3,
            grid=(n // td,),
            in_specs=[pl.BlockSpec((td, d), lambda i, dr, lo, hi: (i, 0))],
            out_specs=pl.BlockSpec(memory_space=pl.ANY),
            scratch_shapes=[pltpu.VMEM((tb, d), F32), pltpu.SemaphoreType.DMA, pltpu.SemaphoreType.DMA],
        ),
        out_shape=jax.ShapeDtypeStruct((n_slots, d), F32),
        compiler_params=_cparams(("arbitrary",)),
        name="dispatch",
    )(dest, pad_lo, pad_hi, h2)


def _expert_changed(blk_e_ref, b):
    return (b == 0) | (blk_e_ref[b] != blk_e_ref[jnp.maximum(b - 1, 0)])


def _stream_expert_weights(blk_e_ref, nxt_ref, nvalid_ref, copies, cast):
    j = pl.program_id(0)
    b = pl.program_id(1)
    nvalid = nvalid_ref[0]

    @pl.when((j == 0) & (b == 0))
    def _():
        for c in copies(blk_e_ref[0], 0):
            c.start(priority=1)

    @pl.when((b < nvalid) & _expert_changed(blk_e_ref, b))
    def _():
        for c in copies(blk_e_ref[b], j):
            c.wait()
        cast()
        nxt = nxt_ref[b]
        more = nxt < nvalid

        @pl.when(more)
        def _():
            for c in copies(blk_e_ref[jnp.minimum(nxt, nvalid - 1)], j):
                c.start(priority=1)

        @pl.when(jnp.logical_not(more) & (j + 1 < pl.num_programs(0)))
        def _():
            for c in copies(blk_e_ref[0], j + 1):
                c.start(priority=1)


def _ffn_up_kernel(blk_e_ref, nxt_ref, nvalid_ref, xs_ref, w_hbm, bg_ref, bu_ref, act_ref,
                   wf32, wbf, sem, *, tf, ff):
    b = pl.program_id(1)
    valid = b < nvalid_ref[0]

    def copies(e, j):
        return [pltpu.make_async_copy(w_hbm.at[e, :, pl.ds(pl.multiple_of(h * ff + j * tf, tf), tf)],
                                      wf32.at[h], sem.at[h]) for h in range(2)]

    def cast():
        wbf[...] = wf32[...].astype(BF16)

    _stream_expert_weights(blk_e_ref, nxt_ref, nvalid_ref, copies, cast)

    @pl.when(valid)
    def _():
        x = xs_ref[...].astype(BF16)
        gate = jnp.dot(x, wbf[0], preferred_element_type=F32) + bg_ref[...]
        up = jnp.dot(x, wbf[1], preferred_element_type=F32) + bu_ref[...]
        gate = jnp.minimum(gate, SWIGLU_LIMIT)
        up = jnp.clip(up, -SWIGLU_LIMIT, SWIGLU_LIMIT)
        act_ref[...] = ((up + 1.0) * gate * jax.nn.sigmoid(gate * SWIGLU_ALPHA)).astype(BF16)

    @pl.when(jnp.logical_not(valid))
    def _():
        act_ref[...] = jnp.zeros_like(act_ref)


def _ffn_up(blk_e, nxt, nvalid, xs, w_gate_up, b_gate_up, tb):
    n_slots, d = xs.shape
    n_exp, _, f2 = w_gate_up.shape
    ff = f2 // 2
    tf = min(FFN_TF, ff)
    nf = ff // tf
    nb = n_slots // tb

    def xmap(j, b, be, nx, nv):
        return (jnp.minimum(b, nv[0] - 1), 0)

    kern = functools.partial(_ffn_up_kernel, tf=tf, ff=ff)
    return pl.pallas_call(
        kern,
        grid_spec=pltpu.PrefetchScalarGridSpec(
            num_scalar_prefetch=3,
            grid=(nf, nb),
            in_specs=[
                pl.BlockSpec((tb, d), xmap),
                pl.BlockSpec(memory_space=pl.ANY),
                pl.BlockSpec((None, 1, tf), lambda j, b, be, nx, nv: (be[b], 0, j)),
                pl.BlockSpec((None, 1, tf), lambda j, b, be, nx, nv: (be[b], 0, nf + j)),
            ],
            out_specs=pl.BlockSpec((tb, tf), lambda j, b, be, nx, nv: (b, j)),
            scratch_shapes=[pltpu.VMEM((2, d, tf), F32), pltpu.VMEM((2, d, tf), BF16),
                            pltpu.SemaphoreType.DMA((2,))],
        ),
        out_shape=jax.ShapeDtypeStruct((n_slots, ff), BF16),
        compiler_params=_cparams(("arbitrary", "arbitrary")),
        name="ffn_up",
    )(blk_e, nxt, nvalid, xs, w_gate_up, b_gate_up, b_gate_up)


def _ffn_down_kernel(blk_e_ref, nxt_ref, nvalid_ref, act_ref, w_hbm, bd_ref, out_ref,
                     wf32, wbf, sem, *, tn):
    b = pl.program_id(1)
    valid = b < nvalid_ref[0]

    def copies(e, j):
        return [pltpu.make_async_copy(w_hbm.at[e, :, pl.ds(pl.multiple_of(j * tn, tn), tn)],
                                      wf32, sem.at[0])]

    def cast():
        wbf[...] = wf32[...].astype(BF16)

    _stream_expert_weights(blk_e_ref, nxt_ref, nvalid_ref, copies, cast)

    @pl.when(valid)
    def _():
        out_ref[...] = jnp.dot(act_ref[...], wbf[...], preferred_element_type=F32) + bd_ref[...]

    @pl.when(jnp.logical_not(valid))
    def _():
        out_ref[...] = jnp.zeros_like(out_ref)


def _ffn_down(blk_e, nxt, nvalid, act, w_down, b_down, tb):
    n_slots, ff = act.shape
    n_exp, _, d = w_down.shape
    tn = min(FFN_TN, d)
    nn = d // tn
    nb = n_slots // tb

    def amap(j, b, be, nx, nv):
        return (jnp.minimum(b, nv[0] - 1), 0)

    kern = functools.partial(_ffn_down_kernel, tn=tn)
    return pl.pallas_call(
        kern,
        grid_spec=pltpu.PrefetchScalarGridSpec(
            num_scalar_prefetch=3,
            grid=(nn, nb),
            in_specs=[
                pl.BlockSpec((tb, ff), amap),
                pl.BlockSpec(memory_space=pl.ANY),
                pl.BlockSpec((None, 1, tn), lambda j, b, be, nx, nv: (be[b], 0, j)),
            ],
            out_specs=pl.BlockSpec((tb, tn), lambda j, b, be, nx, nv: (b, j)),
            scratch_shapes=[pltpu.VMEM((ff, tn), F32), pltpu.VMEM((ff, tn), BF16),
                            pltpu.SemaphoreType.DMA((1,))],
        ),
        out_shape=jax.ShapeDtypeStruct((n_slots, d), F32),
        compiler_params=_cparams(("arbitrary", "arbitrary")),
        name="ffn_down",
    )(blk_e, nxt, nvalid, act, w_down, b_down)


def _combine_kernel(dest_ref, outs_hbm, w_ref, x2_ref, gain_ref, y_ref, buf, sem, *, tc):
    i = pl.program_id(0)
    slot = i % 2

    def gather(step, into):
        base = step * tc * TOP_K

        def start(t, c):
            for k in range(TOP_K):
                pltpu.make_async_copy(outs_hbm.at[pl.ds(dest_ref[base + t * TOP_K + k], 1)],
                                      buf.at[into, k, pl.ds(t, 1)], sem.at[into]).start(priority=k % 2)
            return c

        lax.fori_loop(0, tc, start, 0, unroll=2)

    @pl.when(i == 0)
    def _():
        gather(0, 0)

    @pl.when(i + 1 < pl.num_programs(0))
    def _():
        gather(i + 1, 1 - slot)

    for k in range(TOP_K):
        pltpu.make_async_copy(outs_hbm.at[pl.ds(0, tc)], buf.at[slot, k], sem.at[slot]).wait()
    w = w_ref[...]
    y = x2_ref[...]
    for k in range(TOP_K):
        y = y + w[:, k:k + 1] * buf[slot, k]
    ms = jnp.mean(y * y, axis=-1, keepdims=True)
    y_ref[...] = y * lax.rsqrt(ms + NORM_EPS) * gain_ref[...]


def _combine(dest, outs, w_top, x2, gain):
    n, d = x2.shape
    tc = COMBINE_TC
    kern = functools.partial(_combine_kernel, tc=tc)
    return pl.pallas_call(
        kern,
        grid_spec=pltpu.PrefetchScalarGridSpec(
            num_scalar_prefetch=1,
            grid=(n // tc,),
            in_specs=[
                pl.BlockSpec(memory_space=pl.ANY),
                pl.BlockSpec((tc, LANES), lambda i, dr: (i, 0)),
                pl.BlockSpec((tc, d), lambda i, dr: (i, 0)),
                pl.BlockSpec((1, d), lambda i, dr: (0, 0)),
            ],
            out_specs=pl.BlockSpec((tc, d), lambda i, dr: (i, 0)),
            scratch_shapes=[pltpu.VMEM((2, TOP_K, tc, d), F32), pltpu.SemaphoreType.DMA((2,))],
        ),
        out_shape=jax.ShapeDtypeStruct((n, d), F32),
        compiler_params=_cparams(("arbitrary",)),
        name="combine",
    )(dest, outs, w_top, x2, gain)


def _window_bias(tq):
    nv = WINDOW // tq
    r = np.arange(tq)[None, :, None]
    c = np.arange(tq + WINDOW)[None, None, :]
    v = np.arange(nv + 1)[:, None, None]
    q0 = v * tq
    kstart = np.maximum(q0 - WINDOW, 0)
    rel = (kstart + c) - (q0 + r)
    return np.where((rel <= 0) & (rel > -WINDOW), 0.0, NEG_INF).astype(np.float32)


def _rope_tables(seq):
    half = DH // 2
    inv_freq = np.float32(ROPE_THETA) ** (-np.arange(half, dtype=np.float32) / np.float32(half))
    ang = np.arange(seq, dtype=np.float32)[:, None] * inv_freq[None, :].astype(np.float32)
    cos = np.cos(ang.astype(np.float64)).astype(np.float32)
    sin = np.sin(ang.astype(np.float64)).astype(np.float32)
    return np.concatenate([cos, cos], axis=1), np.concatenate([-sin, sin], axis=1)


def _layer(x2d, batch, seq, norm_mix, w_in, cmp_pos_k, cmp_pos_v, w_cmp_k1, w_cmp_k2, w_cmp_v1,
           w_cmp_v2, w_gla_alpha, b_gla_alpha, gla_norm, w_proj_nsa, w_proj_gla, w_merge_gate,
           b_merge_gate, w_out, norm_moe, w_router, b_router, w_gate_up, b_gate_up, w_down, b_down,
           final_gain):
    n, d = x2d.shape
    n_exp = w_router.shape[1]
    d_merge = w_merge_gate.shape[1]
    main0 = d_merge

    o_g = NSA_Q + 6 * NSA_KV
    o_q = o_g + NSA_HEADS * 3
    o_a = o_q + 2 * GLA_QK + 2 * GLA_V
    w_main = jnp.concatenate([w_in[:, :o_g], w_in[:, o_q:o_a]], axis=1)
    w_all = jnp.concatenate([w_merge_gate, w_main], axis=1).astype(BF16)
    n_gl = NSA_HPG * 3
    small_parts = []
    for g in range(NSA_GROUPS):
        small_parts += [w_in[:, o_g + g * n_gl:o_g + (g + 1) * n_gl], jnp.zeros((d, LANES - n_gl), F32)]
    small_parts += [w_in[:, o_a:], jnp.zeros((d, LANES - GLA_RANK), F32)]
    w_small = jnp.concatenate(small_parts, axis=1).astype(BF16)
    bias_all = jnp.concatenate([b_merge_gate, jnp.zeros((MAIN_W,), F32)])[None, :]
    cos2, sin2 = _rope_tables(seq)

    big, small = _proj(x2d, norm_mix[None, :], w_all, w_small, bias_all, cos2, sin2, seq, d_merge)

    pos = jnp.stack([cmp_pos_k, cmp_pos_v])
    w1 = jnp.stack([w_cmp_k1, w_cmp_v1]).astype(BF16)
    w2 = jnp.stack([w_cmp_k2, w_cmp_v2]).astype(BF16)
    cmp_kv = _compress(big, pos, w1, w2, batch, seq, (main0 + NSA_Q) // DH)
    ncp = cmp_kv.shape[3]
    n_blk = seq // SEL_LEN
    cstart = np.arange(ncp)[:, None] * CMP_STRIDE
    blk = np.arange(LANES)[None, :]
    cover = ((cstart < (blk + 1) * SEL_LEN) & (cstart + CMP_LEN > blk * SEL_LEN)
             & (blk < n_blk) & (np.arange(ncp)[:, None] < seq // CMP_STRIDE - 1)).astype(np.float32)
    et = jnp.asarray((np.arange(seq)[:, None] // SEL_LEN) == np.arange(LANES)[None, :], BF16)
    o_nsa = _nsa(big, cmp_kv, small, cos2, sin2, cover, et, batch, seq, main0)

    tri_c = np.tril(np.ones((GLA_CHUNK, GLA_CHUNK), np.float32))
    w_alpha = jnp.concatenate([w_gla_alpha, jnp.zeros((LANES - GLA_RANK, GLA_QK), F32)], axis=0)
    o_gla = _gla(big, small, w_alpha, b_gla_alpha[None, :], gla_norm[None, :], tri_c,
                 batch, seq, main0)

    wr = jnp.concatenate([w_router, jnp.zeros((d, LANES - n_exp), F32)], axis=1)
    wr_hi = wr.astype(BF16)
    wr_lo = (wr - wr_hi.astype(F32)).astype(BF16)
    br = jnp.concatenate([b_router, jnp.zeros((LANES - n_exp,), F32)])[None, :]
    x2, h2, logits = _mix(o_nsa, o_gla, big, x2d, w_proj_nsa.astype(BF16), w_proj_gla.astype(BF16),
                          w_out.astype(BF16), norm_moe[None, :], wr_hi, wr_lo, br)

    tr = ROUTE_TR
    tri_r = jnp.asarray(np.arange(tr)[:, None] > np.arange(tr)[None, :], BF16)
    ridx, w_top, counts = _route(logits, tri_r, n_exp)

    tb = MOE_TB
    nk = n * TOP_K
    n_blocks = -(-nk // tb) + n_exp
    n_slots = n_blocks * tb
    cnt = counts[0, :n_exp].astype(jnp.int32)
    padded = (cnt + tb - 1) // tb * tb
    pad_end = jnp.cumsum(padded)
    pad_start = pad_end - padded
    top_e = ridx[:, :TOP_K]
    e_ids = jnp.arange(n_exp, dtype=jnp.int32)
    start_of = jnp.sum(jnp.where(top_e[:, :, None] == e_ids, pad_start[None, None, :], 0), axis=-1)
    dest = (start_of + ridx[:, TOP_K:2 * TOP_K]).reshape(nk)
    blk_first = jnp.arange(n_blocks, dtype=jnp.int32) * tb
    blk_e = jnp.minimum(jnp.sum((pad_end[None, :] <= blk_first[:, None]).astype(jnp.int32), axis=1),
                        n_exp - 1)
    nxt = (pad_end[blk_e] // tb).astype(jnp.int32)
    nvalid = (pad_end[-1:] // tb).astype(jnp.int32)

    xs = _dispatch(dest, pad_start + cnt, pad_end, h2, n_slots, n_exp, tb)
    act = _ffn_up(blk_e, nxt, nvalid, xs, w_gate_up, b_gate_up[:, None, :], tb)
    outs = _ffn_down(blk_e, nxt, nvalid, act, w_down, b_down[:, None, :], tb)
    return _combine(dest, outs, w_top, x2, final_gain)


def kernel(x, norm_mix, w_in, cmp_pos_k, cmp_pos_v, w_cmp_k1, w_cmp_k2, w_cmp_v1, w_cmp_v2,
           w_gla_alpha, b_gla_alpha, gla_norm, w_proj_nsa, w_proj_gla, w_merge_gate, b_merge_gate,
           w_out, norm_moe, w_router, b_router, w_gate_up, b_gate_up, w_down, b_down, norm_final):
    batch, seq, d = x.shape
    depth = w_in.shape[0]
    assert depth == 1, "the final norm is fused into the (single) layer's combine stage"
    y = _layer(x.reshape(batch * seq, d), batch, seq, norm_mix[0], w_in[0], cmp_pos_k[0],
               cmp_pos_v[0], w_cmp_k1[0], w_cmp_k2[0], w_cmp_v1[0], w_cmp_v2[0], w_gla_alpha[0],
               b_gla_alpha[0], gla_norm[0], w_proj_nsa[0], w_proj_gla[0], w_merge_gate[0],
               b_merge_gate[0], w_out[0], norm_moe[0], w_router[0], b_router[0], w_gate_up[0],
               b_gate_up[0], w_down[0], b_down[0], norm_final[None, :])
    return y.reshape(batch, seq, d)
```

```python
import functools

import jax
import jax.numpy as jnp
import numpy as np
from jax import lax
from jax.experimental import pallas as pl
from jax.experimental.pallas import tpu as pltpu

F32 = jnp.float32
BF16 = jnp.bfloat16

NORM_EPS = 1e-5
ROPE_THETA = 10000.0
NEG_INF = -1e30

NSA_HEADS = 8
NSA_GROUPS = 2
NSA_HPG = NSA_HEADS // NSA_GROUPS
DH = 128
CMP_LEN = 32
CMP_STRIDE = 16
SEL_LEN = 64
SEL_TOPK = 16
SEL_FORCE = 1e3
SEL_MASK = 2.0 ** 100
WINDOW = 512

GLA_HEADS = 4
GLA_DK = 128
GLA_DV = 256
GLA_RANK = 16
GLA_TAU = 16.0
GLA_CHUNK = 64
GLA_GROUP = 4

TOP_K = 4
SWIGLU_LIMIT = 7.0
SWIGLU_ALPHA = 1.702

LANES = 128
VMEM_LIMIT = 56 * 1024 * 1024

PROJ_TM, PROJ_TN = 1024, 512
NSA_TQ, NSA_CK = 256, 512
MIX_TM = 256
ROUTE_TR = 512
MOE_TB = 256
DISPATCH_TD, COMBINE_TC = 512, 256
FFN_TF, FFN_TN = 1024, 2048

NSA_Q = NSA_HEADS * DH
NSA_KV = NSA_GROUPS * DH
GLA_QK = GLA_HEADS * GLA_DK
GLA_V = GLA_HEADS * GLA_DV
MAIN_W = NSA_Q + 6 * NSA_KV + 2 * GLA_QK + 2 * GLA_V
SMALL_W = (NSA_GROUPS + 1) * LANES


def _cparams(sem, vmem=VMEM_LIMIT):
    return pltpu.CompilerParams(dimension_semantics=sem, vmem_limit_bytes=vmem)


def _rope(x, cos, sin_signed):
    return x * cos + pltpu.roll(x, DH // 2, axis=1) * sin_signed


def _proj_kernel(x_ref, gain_ref, w_ref, wsmall_ref, bias_ref, cos_ref, sin_ref,
                 big_ref, small_ref, h_scr, *, rope_tiles):
    j = pl.program_id(1)

    @pl.when(j == 0)
    def _():
        x = x_ref[...]
        ms = jnp.mean(x * x, axis=-1, keepdims=True)
        hb = (x * lax.rsqrt(ms + NORM_EPS) * gain_ref[...]).astype(BF16)
        h_scr[...] = hb
        small_ref[...] = jnp.dot(hb, wsmall_ref[...], preferred_element_type=F32)

    half = w_ref.shape[1] // 2
    h = h_scr[...]
    is_rope = (j == rope_tiles[0]) | (j == rope_tiles[1])
    acc0 = jnp.dot(h, w_ref[:, :half], preferred_element_type=F32) + bias_ref[:, :half]
    acc1 = jnp.dot(h, w_ref[:, half:], preferred_element_type=F32) + bias_ref[:, half:]
    cos = cos_ref[...]
    sin = sin_ref[...]
    roped = jnp.concatenate([_rope(acc0[:, g * DH:(g + 1) * DH], cos, sin)
                             for g in range(NSA_GROUPS)], axis=1)
    big_ref[:, :half] = jnp.where(is_rope, roped, acc0).astype(BF16)
    big_ref[:, half:] = acc1.astype(BF16)


def _proj(x2d, gain, w_all, w_small, bias_all, cos2, sin2, seq, d_merge):
    n, d = x2d.shape
    width = w_all.shape[1]
    tm = min(PROJ_TM, seq)
    tn = PROJ_TN
    n_merge_tiles = d_merge // tn
    rope_tiles = (n_merge_tiles + 3, n_merge_tiles + 4)
    nsb = seq // tm
    kern = functools.partial(_proj_kernel, rope_tiles=rope_tiles)
    return pl.pallas_call(
        kern,
        grid=(n // tm, width // tn),
        in_specs=[
            pl.BlockSpec((tm, d), lambda i, j: (i, 0)),
            pl.BlockSpec((1, d), lambda i, j: (0, 0)),
            pl.BlockSpec((d, tn), lambda i, j: (0, j)),
            pl.BlockSpec((d, SMALL_W), lambda i, j: (0, 0)),
            pl.BlockSpec((1, tn), lambda i, j: (0, j)),
            pl.BlockSpec((tm, DH), lambda i, j: (i % nsb, 0)),
            pl.BlockSpec((tm, DH), lambda i, j: (i % nsb, 0)),
        ],
        out_specs=[
            pl.BlockSpec((tm, tn), lambda i, j: (i, j)),
            pl.BlockSpec((tm, SMALL_W), lambda i, j: (i, 0)),
        ],
        out_shape=[
            jax.ShapeDtypeStruct((n, width), BF16),
            jax.ShapeDtypeStruct((n, SMALL_W), F32),
        ],
        scratch_shapes=[pltpu.VMEM((tm, d), BF16)],
        compiler_params=_cparams(("parallel", "arbitrary")),
        name="proj",
    )(x2d, gain, w_all, w_small, bias_all, cos2, sin2)


def _compress_kernel(kv_ref, pos_ref, w1_ref, w2_ref, out_ref, scr, *, seq, ncp):
    nreal = seq // CMP_STRIDE
    scr[0:seq, :] = kv_ref[...].astype(F32)
    scr[seq:seq + CMP_LEN, :] = jnp.zeros((CMP_LEN, DH), F32)
    acc = jnp.zeros((nreal, w1_ref.shape[1]), F32)
    for l in range(CMP_LEN):
        a = scr[pl.ds(l, nreal, stride=CMP_STRIDE), :] + pos_ref[l:l + 1, :]
        acc = acc + jnp.dot(a.astype(BF16), w1_ref[l * DH:(l + 1) * DH, :],
                            preferred_element_type=F32)
    hid = jax.nn.gelu(acc)
    out = jnp.dot(hid.astype(BF16), w2_ref[...], preferred_element_type=F32)
    row = lax.broadcasted_iota(jnp.int32, out.shape, 0)
    out = jnp.where(row < nreal - 1, out, 0.0).astype(BF16)
    if ncp > nreal:
        out = jnp.concatenate([out, jnp.zeros((ncp - nreal, DH), BF16)], axis=0)
    out_ref[...] = out


def _compress(big, pos, w1, w2, batch, seq, col0):
    ncp = max(seq // CMP_STRIDE, LANES)
    kern = functools.partial(_compress_kernel, seq=seq, ncp=ncp)
    hid = w1.shape[2]
    return pl.pallas_call(
        kern,
        grid=(batch, NSA_GROUPS, 2),
        in_specs=[
            pl.BlockSpec((seq, DH), lambda b, g, t: (b, col0 + 2 * t + g)),
            pl.BlockSpec((None, CMP_LEN, DH), lambda b, g, t: (t, 0, 0)),
            pl.BlockSpec((None, CMP_LEN * DH, hid), lambda b, g, t: (t, 0, 0)),
            pl.BlockSpec((None, hid, DH), lambda b, g, t: (t, 0, 0)),
        ],
        out_specs=pl.BlockSpec((None, None, None, ncp, DH), lambda b, g, t: (b, g, t, 0, 0)),
        out_shape=jax.ShapeDtypeStruct((batch, NSA_GROUPS, 2, ncp, DH), BF16),
        scratch_shapes=[pltpu.VMEM((seq + CMP_LEN, DH), F32)],
        compiler_params=_cparams(("parallel", "parallel", "arbitrary")),
        name="compress",
    )(big, pos, w1, w2)


def _stack_heads(t):
    return jnp.concatenate([t[:, h * DH:(h + 1) * DH] for h in range(NSA_HPG)], axis=0)


def _nsa_kernel(q_ref, kc_ref, vc_ref, ks_ref, vs_ref, kw_ref, vw_ref, cos_ref, sin_ref,
                gate_ref, cover_ref, et_ref, wbias_ref, o_ref, kext_scr, vsext_scr, vwext_scr,
                *, seq, tq, ck):
    i = pl.program_id(2)
    q0 = i * tq
    scale = DH ** -0.5
    ncp = kc_ref.shape[0]
    n_blk = seq // SEL_LEN
    n_sel = min(SEL_TOPK, n_blk)
    wspan = tq + WINDOW
    nt = (((1,), (1,)), ((), ()))

    @pl.when(i == 0)
    def _():
        ones = jnp.ones((seq, DH), BF16)
        kext_scr[:, :DH] = ks_ref[...]
        kext_scr[:, DH:] = et_ref[...]
        vsext_scr[:, :DH] = vs_ref[...]
        vsext_scr[:, DH:] = ones
        vwext_scr[:, :DH] = vw_ref[...]
        vwext_scr[:, DH:] = ones

    q = q_ref[...]
    qs = _stack_heads(q)
    pos_q = q0 + lax.broadcasted_iota(jnp.int32, (tq, 1), 0)
    pos_rows = jnp.concatenate([pos_q] * NSA_HPG, axis=0)

    cos = cos_ref[...]
    sin = sin_ref[...]
    qr = jnp.concatenate(
        [(_rope(q[:, h * DH:(h + 1) * DH].astype(F32), cos, sin) * scale).astype(BF16)
         for h in range(NSA_HPG)], axis=0)

    s = lax.dot_general(qs, kc_ref[...], nt, preferred_element_type=F32) * scale
    n_idx = lax.broadcasted_iota(jnp.int32, (1, ncp), 1)
    cmask = (n_idx * CMP_STRIDE + (CMP_LEN - 1)) <= pos_rows
    s = jnp.where(cmask, s, NEG_INF)
    m = jnp.max(s, axis=-1, keepdims=True)
    e = jnp.exp(s - m)
    p = jnp.where(cmask, e / jnp.sum(e, axis=-1, keepdims=True), 0.0)
    o_cmp = jnp.dot(p.astype(BF16), vc_ref[...], preferred_element_type=F32)

    psum = p[0:tq]
    for h in range(1, NSA_HPG):
        psum = psum + p[h * tq:(h + 1) * tq]
    p_hi = psum.astype(BF16)
    p_lo = (psum - p_hi.astype(F32)).astype(BF16)
    cov = cover_ref[...]
    imp = (jnp.dot(p_hi, cov, preferred_element_type=F32)
           + jnp.dot(p_lo, cov, preferred_element_type=F32))
    blk = lax.broadcasted_iota(jnp.int32, (tq, LANES), 1)
    t_blk = pos_q // SEL_LEN
    forced = (blk == 0) | (blk == t_blk) | (blk == t_blk - 1)
    bonus = jnp.where(blk > t_blk, -SEL_FORCE, jnp.where(forced, SEL_FORCE, 0.0))
    val_t = (imp + bonus).T[:n_blk]
    blk_t = lax.broadcasted_iota(jnp.int32, (n_blk, tq), 0)
    terms = []
    for c in range(n_blk):
        vc = val_t[c:c + 1, :]
        beats = (vc > val_t) | ((vc == val_t) & (blk_t > c))
        terms.append(jnp.where(beats, 1.0, 0.0))
    while len(terms) > 1:
        terms = [a + b for a, b in zip(terms[0::2], terms[1::2])]
    rank = terms[0]
    pen_t = jnp.where(rank < n_sel, 0.0, -SEL_MASK)
    pen_t = jnp.concatenate([pen_t, jnp.zeros((LANES - n_blk, tq), F32)], axis=0)
    pen = pen_t.T.astype(BF16)

    q_ext = jnp.concatenate([qr, jnp.concatenate([pen] * NSA_HPG, axis=0)], axis=1)

    n_chunks = (q0 + tq + ck - 1) // ck

    q_heads = [q_ext[h * tq:(h + 1) * tq] for h in range(NSA_HPG)]

    def sel_chunk(c, carry, causal):
        k0 = pl.multiple_of(c * ck, ck)
        kblk = kext_scr[pl.ds(k0, ck), :]
        vblk = vsext_scr[pl.ds(k0, ck), :]
        scs = [lax.dot_general(q_heads[h], kblk, nt, preferred_element_type=F32)
               for h in range(NSA_HPG)]
        out = []
        for h in range(NSA_HPG):
            m_i, acc = carry[h]
            sc = scs[h]
            if causal:
                kp = k0 + lax.broadcasted_iota(jnp.int32, (1, ck), 1)
                sc = jnp.where(kp <= pos_q, sc, NEG_INF)
            m_new = jnp.maximum(m_i, jnp.max(sc, axis=-1, keepdims=True))
            alpha = jnp.exp(m_i - m_new)
            pc = jnp.exp(sc - m_new)
            acc = alpha * acc + jnp.dot(pc.astype(BF16), vblk, preferred_element_type=F32)
            out.append((m_new, acc))
        return tuple(out)

    carry = tuple((jnp.full((tq, 1), NEG_INF, F32), jnp.zeros((tq, 2 * DH), F32))
                  for _ in range(NSA_HPG))
    carry = lax.fori_loop(0, n_chunks - 1, lambda c, cr: sel_chunk(c, cr, False), carry)
    carry = sel_chunk(n_chunks - 1, carry, True)

    kstart = pl.multiple_of(jnp.maximum(q0 - WINDOW, 0), tq)
    kwin = kw_ref[pl.ds(kstart, wspan), :]
    vwin = vwext_scr[pl.ds(kstart, wspan), :]
    wbias = wbias_ref[...]
    sws = [lax.dot_general(qr[h * tq:(h + 1) * tq], kwin, nt, preferred_element_type=F32)
           for h in range(NSA_HPG)]

    gates = jax.nn.sigmoid(gate_ref[...])
    for h in range(NSA_HPG):
        acc_s = carry[h][1]
        o_sel = acc_s[:, :DH] / acc_s[:, DH:]
        sw = sws[h] + wbias
        ew = jnp.exp(sw - jnp.max(sw, axis=-1, keepdims=True))
        acc_w = jnp.dot(ew.astype(BF16), vwin, preferred_element_type=F32)
        o_win = acc_w[:, :DH] / acc_w[:, DH:]
        c0 = 3 * h
        o_h = (gates[:, c0:c0 + 1] * o_cmp[h * tq:(h + 1) * tq] + gates[:, c0 + 1:c0 + 2] * o_sel
               + gates[:, c0 + 2:c0 + 3] * o_win)
        o_ref[:, h * DH:(h + 1) * DH] = o_h.astype(BF16)


def _nsa(big, cmp_kv, small, cos2, sin2, cover, et, batch, seq, main0):
    tq = NSA_TQ
    wbias = _window_bias(tq)
    ck = min(NSA_CK, seq)
    nq = seq // tq
    ncp = cmp_kv.shape[3]
    wspan = tq + WINDOW
    n_wb = wbias.shape[0]
    kern = functools.partial(_nsa_kernel, seq=seq, tq=tq, ck=ck)
    kv_col = (main0 + NSA_Q) // DH

    def kvspec(which):
        return pl.BlockSpec((seq, DH), lambda b, g, i: (b, kv_col + 2 * which + g))

    return pl.pallas_call(
        kern,
        grid=(batch, NSA_GROUPS, nq),
        in_specs=[
            pl.BlockSpec((tq, NSA_HPG * DH), lambda b, g, i: (b * nq + i, main0 // (NSA_HPG * DH) + g)),
            pl.BlockSpec((None, None, None, ncp, DH), lambda b, g, i: (b, g, 0, 0, 0)),
            pl.BlockSpec((None, None, None, ncp, DH), lambda b, g, i: (b, g, 1, 0, 0)),
            kvspec(2), kvspec(3), kvspec(4), kvspec(5),
            pl.BlockSpec((tq, DH), lambda b, g, i: (i, 0)),
            pl.BlockSpec((tq, DH), lambda b, g, i: (i, 0)),
            pl.BlockSpec((tq, LANES), lambda b, g, i: (b * nq + i, g)),
            pl.BlockSpec((ncp, LANES), lambda b, g, i: (0, 0)),
            pl.BlockSpec((seq, LANES), lambda b, g, i: (0, 0)),
            pl.BlockSpec((None, tq, wspan), lambda b, g, i: (jnp.minimum(i, n_wb - 1), 0, 0)),
        ],
        out_specs=pl.BlockSpec((tq, NSA_HPG * DH), lambda b, g, i: (b * nq + i, g)),
        out_shape=jax.ShapeDtypeStruct((batch * seq, NSA_Q), BF16),
        scratch_shapes=[pltpu.VMEM((seq, 2 * DH), BF16), pltpu.VMEM((seq, 2 * DH), BF16),
                        pltpu.VMEM((seq, 2 * DH), BF16)],
        compiler_params=_cparams(("parallel", "parallel", "arbitrary")),
        name="nsa",
    )(big, cmp_kv, cmp_kv, big, big, big, big, cos2, sin2, small, cover, et, wbias)


def _gla_kernel(q_ref, k_ref, v_ref, r_ref, small_ref, wa_ref, ba_ref, gain_ref, tri_ref,
                o_ref, qt_scr, kt_scr, ks_scr, dec_scr, *, seq):
    C = GLA_CHUNK
    nc = seq // C
    GR = GLA_GROUP * C
    nt = (((1,), (1,)), ((), ()))
    z = jnp.dot(small_ref[...], wa_ref[...], preferred_element_type=F32,
                precision=lax.Precision.HIGHEST) + ba_ref[...]
    la = jax.nn.log_sigmoid(z) / GLA_TAU

    la_r = jnp.concatenate([la[c * C:(c + 1) * C, :] for c in range(nc)], axis=1)
    cum_r = jnp.dot(tri_ref[...], la_r, preferred_element_type=F32,
                    precision=lax.Precision.HIGHEST)
    last_r = cum_r[C - 1:C, :]
    qf_r = jnp.exp(cum_r)
    kf_r = jnp.exp(-cum_r)
    sf_r = jnp.exp(last_r - cum_r)
    dec_r = jnp.exp(last_r)
    qscale = GLA_DK ** -0.5
    for c in range(nc):
        rows = slice(c * C, (c + 1) * C)
        cols = slice(c * GLA_DK, (c + 1) * GLA_DK)
        qc = q_ref[rows, :].astype(F32) * qscale
        kc = k_ref[rows, :].astype(F32)
        qt_scr[rows, :] = (qc * qf_r[:, cols]).astype(BF16)
        kt_scr[rows, :] = (kc * kf_r[:, cols]).astype(BF16)
        ks_scr[rows, :] = (kc * sf_r[:, cols]).astype(BF16)
        dec_scr[c:c + 1, :] = dec_r[:, cols]

    ri = lax.broadcasted_iota(jnp.int32, (GR, GR), 0)
    ci = lax.broadcasted_iota(jnp.int32, (GR, GR), 1)
    mask = (ri >= ci) & ((ri // C) == (ci // C))

    def body(g, st):
        r0 = pl.multiple_of(g * GR, GR)
        qt = qt_scr[pl.ds(r0, GR), :]
        kt = kt_scr[pl.ds(r0, GR), :]
        ks = ks_scr[pl.ds(r0, GR), :]
        v = v_ref[pl.ds(r0, GR), :]
        attn = lax.dot_general(qt, kt, nt, preferred_element_type=F32)
        attn = jnp.where(mask, attn, 0.0)
        o_intra = jnp.dot(attn.astype(BF16), v, preferred_element_type=F32)
        outs = []
        for cc in range(GLA_GROUP):
            sl = slice(cc * C, (cc + 1) * C)
            outs.append(o_intra[sl] + lax.dot_general(qt[sl], st.astype(BF16), nt,
                                                      preferred_element_type=F32))
            d_st = lax.dot_general(v[sl], ks[sl], (((0,), (0,)), ((), ())),
                                   preferred_element_type=F32)
            st = st * dec_scr[pl.ds(g * GLA_GROUP + cc, 1), :] + d_st
        o = jnp.concatenate(outs, axis=0)
        o = o * lax.rsqrt(jnp.mean(o * o, axis=-1, keepdims=True) + NORM_EPS)
        o = o * gain_ref[...]
        rr = r_ref[pl.ds(r0, GR), :].astype(F32)
        o_ref[pl.ds(r0, GR), :] = (o * (rr * jax.nn.sigmoid(rr))).astype(BF16)
        return st

    lax.fori_loop(0, seq // GR, body, jnp.zeros((GLA_DV, GLA_DK), F32), unroll=2)


def _gla(big, small, w_alpha, b_alpha, gain, tri, batch, seq, main0):
    qcol = (main0 + NSA_Q + 6 * NSA_KV) // GLA_DK
    kcol = qcol + GLA_QK // GLA_DK
    vcol = (main0 + NSA_Q + 6 * NSA_KV + 2 * GLA_QK) // GLA_DV
    rcol = vcol + GLA_V // GLA_DV
    kern = functools.partial(_gla_kernel, seq=seq)
    return pl.pallas_call(
        kern,
        grid=(batch, GLA_HEADS),
        in_specs=[
            pl.BlockSpec((seq, GLA_DK), lambda b, h: (b, qcol + h)),
            pl.BlockSpec((seq, GLA_DK), lambda b, h: (b, kcol + h)),
            pl.BlockSpec((seq, GLA_DV), lambda b, h: (b, vcol + h)),
            pl.BlockSpec((seq, GLA_DV), lambda b, h: (b, rcol + h)),
            pl.BlockSpec((seq, LANES), lambda b, h: (b, NSA_GROUPS)),
            pl.BlockSpec((LANES, GLA_DK), lambda b, h: (0, h)),
            pl.BlockSpec((1, GLA_DK), lambda b, h: (0, h)),
            pl.BlockSpec((1, GLA_DV), lambda b, h: (0, h)),
            pl.BlockSpec((GLA_CHUNK, GLA_CHUNK), lambda b, h: (0, 0)),
        ],
        out_specs=pl.BlockSpec((seq, GLA_DV), lambda b, h: (b, h)),
        out_shape=jax.ShapeDtypeStruct((batch * seq, GLA_V), BF16),
        scratch_shapes=[pltpu.VMEM((seq, GLA_DK), BF16), pltpu.VMEM((seq, GLA_DK), BF16),
                        pltpu.VMEM((seq, GLA_DK), BF16), pltpu.VMEM((seq // GLA_CHUNK, GLA_DK), F32)],
        compiler_params=_cparams(("parallel", "parallel")),
        name="gla",
    )(big, big, big, big, small, w_alpha, b_alpha, gain, tri)


def _mix_kernel(on_ref, og_ref, ma_ref, mb_ref, x_ref, wpn_ref, wpg_ref, wo_ref, gain_ref,
                wrh_ref, wrl_ref, br_ref, x2_ref, h2_ref, lg_ref):
    a = jnp.dot(on_ref[...], wpn_ref[...], preferred_element_type=F32)
    b = jnp.dot(og_ref[...], wpg_ref[...], preferred_element_type=F32)
    mixed = (jax.nn.sigmoid(ma_ref[...].astype(F32)) * a
             + jax.nn.sigmoid(mb_ref[...].astype(F32)) * b)
    x2 = x_ref[...] + jnp.dot(mixed.astype(BF16), wo_ref[...], preferred_element_type=F32)
    x2_ref[...] = x2
    ms = jnp.mean(x2 * x2, axis=-1, keepdims=True)
    h2 = x2 * lax.rsqrt(ms + NORM_EPS) * gain_ref[...]
    h2_ref[...] = h2
    hi = h2.astype(BF16)
    lo = (h2 - hi.astype(F32)).astype(BF16)
    lg = (jnp.dot(hi, wrh_ref[...], preferred_element_type=F32)
          + jnp.dot(lo, wrh_ref[...], preferred_element_type=F32)
          + jnp.dot(hi, wrl_ref[...], preferred_element_type=F32))
    lg_ref[...] = lg + br_ref[...]


def _mix(o_nsa, o_gla, big, x2d, wpn, wpg, wo, gain, wr_hi, wr_lo, br):
    n, d = x2d.shape
    tm = MIX_TM
    const = lambda i: (0, 0)
    return pl.pallas_call(
        _mix_kernel,
        grid=(n // tm,),
        in_specs=[
            pl.BlockSpec((tm, NSA_Q), lambda i: (i, 0)),
            pl.BlockSpec((tm, GLA_V), lambda i: (i, 0)),
            pl.BlockSpec((tm, d), lambda i: (i, 0)),
            pl.BlockSpec((tm, d), lambda i: (i, 1)),
            pl.BlockSpec((tm, d), lambda i: (i, 0)),
            pl.BlockSpec((NSA_Q, d), const, pipeline_mode=pl.Buffered(1)),
            pl.BlockSpec((GLA_V, d), const, pipeline_mode=pl.Buffered(1)),
            pl.BlockSpec((d, d), const, pipeline_mode=pl.Buffered(1)),
            pl.BlockSpec((1, d), const),
            pl.BlockSpec((d, LANES), const),
            pl.BlockSpec((d, LANES), const),
            pl.BlockSpec((1, LANES), const),
        ],
        out_specs=[
            pl.BlockSpec((tm, d), lambda i: (i, 0)),
            pl.BlockSpec((tm, d), lambda i: (i, 0)),
            pl.BlockSpec((tm, LANES), lambda i: (i, 0)),
        ],
        out_shape=[
            jax.ShapeDtypeStruct((n, d), F32),
            jax.ShapeDtypeStruct((n, d), F32),
            jax.ShapeDtypeStruct((n, LANES), F32),
        ],
        compiler_params=_cparams(("parallel",)),
        name="mix",
    )(o_nsa, o_gla, big, big, x2d, wpn, wpg, wo, gain, wr_hi, wr_lo, br)


def _route_kernel(lg_ref, tri_ref, idx_ref, w_ref, cnt_ref, carry_scr, *, n_exp):
    i = pl.program_id(0)
    tr = lg_ref.shape[0]

    @pl.when(i == 0)
    def _():
        carry_scr[...] = jnp.zeros_like(carry_scr)

    lane = lax.broadcasted_iota(jnp.int32, (tr, LANES), 1)
    work = jnp.where(lane < n_exp, lg_ref[...], -3e38)
    onehots, vals, idxs = [], [], []
    for _ in range(TOP_K):
        mval = jnp.max(work, axis=-1, keepdims=True)
        idx = jnp.min(jnp.where(work == mval, lane, LANES), axis=-1, keepdims=True)
        oh = lane == idx
        work = jnp.where(oh, -3e38, work)
        onehots.append(oh)
        vals.append(mval)
        idxs.append(idx)
    exps = [jnp.exp(v - vals[0]) for v in vals]
    den = exps[0] + exps[1] + exps[2] + exps[3]
    onehot = jnp.zeros((tr, LANES), F32)
    for oh in onehots:
        onehot = onehot + jnp.where(oh, 1.0, 0.0)
    before = jnp.dot(tri_ref[...], onehot.astype(BF16), preferred_element_type=F32) + carry_scr[...]
    carry_scr[...] = carry_scr[...] + jnp.sum(onehot, axis=0, keepdims=True)
    idx_out = jnp.zeros((tr, LANES), jnp.int32)
    w_out = jnp.zeros((tr, LANES), F32)
    for k in range(TOP_K):
        rank_k = jnp.sum(jnp.where(onehots[k], before, 0.0), axis=-1, keepdims=True)
        idx_out = jnp.where(lane == k, idxs[k], idx_out)
        idx_out = jnp.where(lane == TOP_K + k, rank_k.astype(jnp.int32), idx_out)
        w_out = jnp.where(lane == k, exps[k] / den, w_out)
    idx_ref[...] = idx_out
    w_ref[...] = w_out
    cnt_ref[...] = carry_scr[...]


def _route(logits, tri, n_exp):
    n = logits.shape[0]
    tr = tri.shape[0]
    kern = functools.partial(_route_kernel, n_exp=n_exp)
    return pl.pallas_call(
        kern,
        grid=(n // tr,),
        in_specs=[pl.BlockSpec((tr, LANES), lambda i: (i, 0)),
                  pl.BlockSpec((tr, tr), lambda i: (0, 0))],
        out_specs=[pl.BlockSpec((tr, LANES), lambda i: (i, 0)),
                   pl.BlockSpec((tr, LANES), lambda i: (i, 0)),
                   pl.BlockSpec((1, LANES), lambda i: (0, 0))],
        out_shape=[jax.ShapeDtypeStruct((n, LANES), jnp.int32),
                   jax.ShapeDtypeStruct((n, LANES), F32),
                   jax.ShapeDtypeStruct((1, LANES), F32)],
        scratch_shapes=[pltpu.VMEM((1, LANES), F32)],
        compiler_params=_cparams(("arbitrary",)),
        name="route",
    )(logits, tri)


def _dispatch_kernel(dest_ref, padlo_ref, padhi_ref, h_ref, xs_hbm, zbuf, sem, zsem, *, td, tb, n_exp):
    i = pl.program_id(0)
    n_blocks = xs_hbm.shape[0] // tb

    def row_copy(s):
        return pltpu.make_async_copy(zbuf.at[pl.ds(0, 1)], xs_hbm.at[pl.ds(s, 1)], zsem)

    def blk_copy(b):
        return pltpu.make_async_copy(zbuf, xs_hbm.at[pl.ds(b * tb, tb)], zsem)

    def zero_fill(start):
        def per_expert(e, c):
            def row(s, c2):
                if start:
                    row_copy(s).start()
                else:
                    row_copy(s).wait()
                return c2

            lax.fori_loop(padlo_ref[e], padhi_ref[e], row, 0)
            return c

        lax.fori_loop(0, n_exp, per_expert, 0)

        def blk(b, c):
            if start:
                blk_copy(b).start()
            else:
                blk_copy(b).wait()
            return c

        lax.fori_loop(padhi_ref[n_exp - 1] // tb, n_blocks, blk, 0)

    @pl.when(i == 0)
    def _():
        zbuf[...] = jnp.zeros_like(zbuf)
        zero_fill(start=True)

    base = i * td * TOP_K

    def start(t, c):
        for k in range(TOP_K):
            pltpu.make_async_copy(h_ref.at[pl.ds(t, 1)],
                                  xs_hbm.at[pl.ds(dest_ref[base + t * TOP_K + k], 1)],
                                  sem).start(priority=k % 2)
        return c

    lax.fori_loop(0, td, start, 0, unroll=2)
    for k in range(TOP_K):
        pltpu.make_async_copy(h_ref, xs_hbm.at[pl.ds(0, td)], sem).wait()

    @pl.when(i == 0)
    def _():
        zero_fill(start=False)


def _dispatch(dest, pad_lo, pad_hi, h2, n_slots, n_exp, tb):
    n, d = h2.shape
    td = DISPATCH_TD
    kern = functools.partial(_dispatch_kernel, td=td, tb=tb, n_exp=n_exp)
    return pl.pallas_call(
        kern,
        grid_spec=pltpu.PrefetchScalarGridSpec(
            num_scalar_prefetch=3,
            grid=(n // td,),
            in_specs=[pl.BlockSpec((td, d), lambda i, dr, lo, hi: (i, 0))],
            out_specs=pl.BlockSpec(memory_space=pl.ANY),
            scratch_shapes=[pltpu.VMEM((tb, d), F32), pltpu.SemaphoreType.DMA, pltpu.SemaphoreType.DMA],
        ),
        out_shape=jax.ShapeDtypeStruct((n_slots, d), F32),
        compiler_params=_cparams(("arbitrary",)),
        name="dispatch",
    )(dest, pad_lo, pad_hi, h2)


def _expert_changed(blk_e_ref, b):
    return (b == 0) | (blk_e_ref[b] != blk_e_ref[jnp.maximum(b - 1, 0)])


def _stream_expert_weights(blk_e_ref, nxt_ref, nvalid_ref, copies, cast):
    j = pl.program_id(0)
    b = pl.program_id(1)
    nvalid = nvalid_ref[0]

    @pl.when((j == 0) & (b == 0))
    def _():
        for c in copies(blk_e_ref[0], 0):
            c.start(priority=1)

    @pl.when((b < nvalid) & _expert_changed(blk_e_ref, b))
    def _():
        for c in copies(blk_e_ref[b], j):
            c.wait()
        cast()
        nxt = nxt_ref[b]
        more = nxt < nvalid

        @pl.when(more)
        def _():
            for c in copies(blk_e_ref[jnp.minimum(nxt, nvalid - 1)], j):
                c.start(priority=1)

        @pl.when(jnp.logical_not(more) & (j + 1 < pl.num_programs(0)))
        def _():
            for c in copies(blk_e_ref[0], j + 1):
                c.start(priority=1)


def _ffn_up_kernel(blk_e_ref, nxt_ref, nvalid_ref, xs_ref, w_hbm, bg_ref, bu_ref, act_ref,
                   wf32, wbf, sem, *, tf, ff):
    b = pl.program_id(1)
    valid = b < nvalid_ref[0]

    def copies(e, j):
        return [pltpu.make_async_copy(w_hbm.at[e, :, pl.ds(pl.multiple_of(h * ff + j * tf, tf), tf)],
                                      wf32.at[h], sem.at[h]) for h in range(2)]

    def cast():
        wbf[...] = wf32[...].astype(BF16)

    _stream_expert_weights(blk_e_ref, nxt_ref, nvalid_ref, copies, cast)

    @pl.when(valid)
    def _():
        x = xs_ref[...].astype(BF16)
        gate = jnp.dot(x, wbf[0], preferred_element_type=F32) + bg_ref[...]
        up = jnp.dot(x, wbf[1], preferred_element_type=F32) + bu_ref[...]
        gate = jnp.minimum(gate, SWIGLU_LIMIT)
        up = jnp.clip(up, -SWIGLU_LIMIT, SWIGLU_LIMIT)
        act_ref[...] = ((up + 1.0) * gate * jax.nn.sigmoid(gate * SWIGLU_ALPHA)).astype(BF16)

    @pl.when(jnp.logical_not(valid))
    def _():
        act_ref[...] = jnp.zeros_like(act_ref)


def _ffn_up(blk_e, nxt, nvalid, xs, w_gate_up, b_gate_up, tb):
    n_slots, d = xs.shape
    n_exp, _, f2 = w_gate_up.shape
    ff = f2 // 2
    tf = min(FFN_TF, ff)
    nf = ff // tf
    nb = n_slots // tb

    def xmap(j, b, be, nx, nv):
        return (jnp.minimum(b, nv[0] - 1), 0)

    kern = functools.partial(_ffn_up_kernel, tf=tf, ff=ff)
    return pl.pallas_call(
        kern,
        grid_spec=pltpu.PrefetchScalarGridSpec(
            num_scalar_prefetch=3,
            grid=(nf, nb),
            in_specs=[
                pl.BlockSpec((tb, d), xmap),
                pl.BlockSpec(memory_space=pl.ANY),
                pl.BlockSpec((None, 1, tf), lambda j, b, be, nx, nv: (be[b], 0, j)),
                pl.BlockSpec((None, 1, tf), lambda j, b, be, nx, nv: (be[b], 0, nf + j)),
            ],
            out_specs=pl.BlockSpec((tb, tf), lambda j, b, be, nx, nv: (b, j)),
            scratch_shapes=[pltpu.VMEM((2, d, tf), F32), pltpu.VMEM((2, d, tf), BF16),
                            pltpu.SemaphoreType.DMA((2,))],
        ),
        out_shape=jax.ShapeDtypeStruct((n_slots, ff), BF16),
        compiler_params=_cparams(("arbitrary", "arbitrary")),
        name="ffn_up",
    )(blk_e, nxt, nvalid, xs, w_gate_up, b_gate_up, b_gate_up)


def _ffn_down_kernel(blk_e_ref, nxt_ref, nvalid_ref, act_ref, w_hbm, bd_ref, out_ref,
                     wf32, wbf, sem, *, tn):
    b = pl.program_id(1)
    valid = b < nvalid_ref[0]

    def copies(e, j):
        return [pltpu.make_async_copy(w_hbm.at[e, :, pl.ds(pl.multiple_of(j * tn, tn), tn)],
                                      wf32, sem.at[0])]

    def cast():
        wbf[...] = wf32[...].astype(BF16)

    _stream_expert_weights(blk_e_ref, nxt_ref, nvalid_ref, copies, cast)

    @pl.when(valid)
    def _():
        out_ref[...] = jnp.dot(act_ref[...], wbf[...], preferred_element_type=F32) + bd_ref[...]

    @pl.when(jnp.logical_not(valid))
    def _():
        out_ref[...] = jnp.zeros_like(out_ref)


def _ffn_down(blk_e, nxt, nvalid, act, w_down, b_down, tb):
    n_slots, ff = act.shape
    n_exp, _, d = w_down.shape
    tn = min(FFN_TN, d)
    nn = d // tn
    nb = n_slots // tb

    def amap(j, b, be, nx, nv):
        return (jnp.minimum(b, nv[0] - 1), 0)

    kern = functools.partial(_ffn_down_kernel, tn=tn)
    return pl.pallas_call(
        kern,
        grid_spec=pltpu.PrefetchScalarGridSpec(
            num_scalar_prefetch=3,
            grid=(nn, nb),
            in_specs=[
                pl.BlockSpec((tb, ff), amap),
                pl.BlockSpec(memory_space=pl.ANY),
                pl.BlockSpec((None, 1, tn), lambda j, b, be, nx, nv: (be[b], 0, j)),
            ],
            out_specs=pl.BlockSpec((tb, tn), lambda j, b, be, nx, nv: (b, j)),
            scratch_shapes=[pltpu.VMEM((ff, tn), F32), pltpu.VMEM((ff, tn), BF16),
                            pltpu.SemaphoreType.DMA((1,))],
        ),
        out_shape=jax.ShapeDtypeStruct((n_slots, d), F32),
        compiler_params=_cparams(("arbitrary", "arbitrary")),
        name="ffn_down",
    )(blk_e, nxt, nvalid, act, w_down, b_down)


def _combine_kernel(dest_ref, outs_hbm, w_ref, x2_ref, gain_ref, y_ref, buf, sem, *, tc):
    i = pl.program_id(0)
    slot = i % 2

    def gather(step, into):
        base = step * tc * TOP_K

        def start(t, c):
            for k in range(TOP_K):
                pltpu.make_async_copy(outs_hbm.at[pl.ds(dest_ref[base + t * TOP_K + k], 1)],
                                      buf.at[into, k, pl.ds(t, 1)], sem.at[into]).start(priority=k % 2)
            return c

        lax.fori_loop(0, tc, start, 0, unroll=2)

    @pl.when(i == 0)
    def _():
        gather(0, 0)

    @pl.when(i + 1 < pl.num_programs(0))
    def _():
        gather(i + 1, 1 - slot)

    for k in range(TOP_K):
        pltpu.make_async_copy(outs_hbm.at[pl.ds(0, tc)], buf.at[slot, k], sem.at[slot]).wait()
    w = w_ref[...]
    y = x2_ref[...]
    for k in range(TOP_K):
        y = y + w[:, k:k + 1] * buf[slot, k]
    ms = jnp.mean(y * y, axis=-1, keepdims=True)
    y_ref[...] = y * lax.rsqrt(ms + NORM_EPS) * gain_ref[...]


def _combine(dest, outs, w_top, x2, gain):
    n, d = x2.shape
    tc = COMBINE_TC
    kern = functools.partial(_combine_kernel, tc=tc)
    return pl.pallas_call(
        kern,
        grid_spec=pltpu.PrefetchScalarGridSpec(
            num_scalar_prefetch=1,
            grid=(n // tc,),
            in_specs=[
                pl.BlockSpec(memory_space=pl.ANY),
                pl.BlockSpec((tc, LANES), lambda i, dr: (i, 0)),
                pl.BlockSpec((tc, d), lambda i, dr: (i, 0)),
                pl.BlockSpec((1, d), lambda i, dr: (0, 0)),
            ],
            out_specs=pl.BlockSpec((tc, d), lambda i, dr: (i, 0)),
            scratch_shapes=[pltpu.VMEM((2, TOP_K, tc, d), F32), pltpu.SemaphoreType.DMA((2,))],
        ),
        out_shape=jax.ShapeDtypeStruct((n, d), F32),
        compiler_params=_cparams(("arbitrary",)),
        name="combine",
    )(dest, outs, w_top, x2, gain)


def _window_bias(tq):
    nv = WINDOW // tq
    r = np.arange(tq)[None, :, None]
    c = np.arange(tq + WINDOW)[None, None, :]
    v = np.arange(nv + 1)[:, None, None]
    q0 = v * tq
    kstart = np.maximum(q0 - WINDOW, 0)
    rel = (kstart + c) - (q0 + r)
    return np.where((rel <= 0) & (rel > -WINDOW), 0.0, NEG_INF).astype(np.float32)


def _rope_tables(seq):
    half = DH // 2
    inv_freq = np.float32(ROPE_THETA) ** (-np.arange(half, dtype=np.float32) / np.float32(half))
    ang = np.arange(seq, dtype=np.float32)[:, None] * inv_freq[None, :].astype(np.float32)
    cos = np.cos(ang.astype(np.float64)).astype(np.float32)
    sin = np.sin(ang.astype(np.float64)).astype(np.float32)
    return np.concatenate([cos, cos], axis=1), np.concatenate([-sin, sin], axis=1)


def _layer(x2d, batch, seq, norm_mix, w_in, cmp_pos_k, cmp_pos_v, w_cmp_k1, w_cmp_k2, w_cmp_v1,
           w_cmp_v2, w_gla_alpha, b_gla_alpha, gla_norm, w_proj_nsa, w_proj_gla, w_merge_gate,
           b_merge_gate, w_out, norm_moe, w_router, b_router, w_gate_up, b_gate_up, w_down, b_down,
           final_gain):
    n, d = x2d.shape
    n_exp = w_router.shape[1]
    d_merge = w_merge_gate.shape[1]
    main0 = d_merge

    o_g = NSA_Q + 6 * NSA_KV
    o_q = o_g + NSA_HEADS * 3
    o_a = o_q + 2 * GLA_QK + 2 * GLA_V
    w_main = jnp.concatenate([w_in[:, :o_g], w_in[:, o_q:o_a]], axis=1)
    w_all = jnp.concatenate([w_merge_gate, w_main], axis=1).astype(BF16)
    n_gl = NSA_HPG * 3
    small_parts = []
    for g in range(NSA_GROUPS):
        small_parts += [w_in[:, o_g + g * n_gl:o_g + (g + 1) * n_gl], jnp.zeros((d, LANES - n_gl), F32)]
    small_parts += [w_in[:, o_a:], jnp.zeros((d, LANES - GLA_RANK), F32)]
    w_small = jnp.concatenate(small_parts, axis=1).astype(BF16)
    bias_all = jnp.concatenate([b_merge_gate, jnp.zeros((MAIN_W,), F32)])[None, :]
    cos2, sin2 = _rope_tables(seq)

    big, small = _proj(x2d, norm_mix[None, :], w_all, w_small, bias_all, cos2, sin2, seq, d_merge)

    pos = jnp.stack([cmp_pos_k, cmp_pos_v])
    w1 = jnp.stack([w_cmp_k1, w_cmp_v1]).astype(BF16)
    w2 = jnp.stack([w_cmp_k2, w_cmp_v2]).astype(BF16)
    cmp_kv = _compress(big, pos, w1, w2, batch, seq, (main0 + NSA_Q) // DH)
    ncp = cmp_kv.shape[3]
    n_blk = seq // SEL_LEN
    cstart = np.arange(ncp)[:, None] * CMP_STRIDE
    blk = np.arange(LANES)[None, :]
    cover = jnp.asarray((cstart < (blk + 1) * SEL_LEN) & (cstart + CMP_LEN > blk * SEL_LEN)
                        & (blk < n_blk) & (np.arange(ncp)[:, None] < seq // CMP_STRIDE - 1), BF16)
    et = jnp.asarray((np.arange(seq)[:, None] // SEL_LEN) == np.arange(LANES)[None, :], BF16)
    o_nsa = _nsa(big, cmp_kv, small, cos2, sin2, cover, et, batch, seq, main0)

    tri_c = np.tril(np.ones((GLA_CHUNK, GLA_CHUNK), np.float32))
    w_alpha = jnp.concatenate([w_gla_alpha, jnp.zeros((LANES - GLA_RANK, GLA_QK), F32)], axis=0)
    o_gla = _gla(big, small, w_alpha, b_gla_alpha[None, :], gla_norm[None, :], tri_c,
                 batch, seq, main0)

    wr = jnp.concatenate([w_router, jnp.zeros((d, LANES - n_exp), F32)], axis=1)
    wr_hi = wr.astype(BF16)
    wr_lo = (wr - wr_hi.astype(F32)).astype(BF16)
    br = jnp.concatenate([b_router, jnp.zeros((LANES - n_exp,), F32)])[None, :]
    x2, h2, logits = _mix(o_nsa, o_gla, big, x2d, w_proj_nsa.astype(BF16), w_proj_gla.astype(BF16),
                          w_out.astype(BF16), norm_moe[None, :], wr_hi, wr_lo, br)

    tr = ROUTE_TR
    tri_r = jnp.asarray(np.arange(tr)[:, None] > np.arange(tr)[None, :], BF16)
    ridx, w_top, counts = _route(logits, tri_r, n_exp)

    tb = MOE_TB
    nk = n * TOP_K
    n_blocks = -(-nk // tb) + n_exp
    n_slots = n_blocks * tb
    cnt = counts[0, :n_exp].astype(jnp.int32)
    padded = (cnt + tb - 1) // tb * tb
    pad_end = jnp.cumsum(padded)
    pad_start = pad_end - padded
    top_e = ridx[:, :TOP_K]
    e_ids = jnp.arange(n_exp, dtype=jnp.int32)
    start_of = jnp.sum(jnp.where(top_e[:, :, None] == e_ids, pad_start[None, None, :], 0), axis=-1)
    dest = (start_of + ridx[:, TOP_K:2 * TOP_K]).reshape(nk)
    blk_first = jnp.arange(n_blocks, dtype=jnp.int32) * tb
    blk_e = jnp.minimum(jnp.sum((pad_end[None, :] <= blk_first[:, None]).astype(jnp.int32), axis=1),
                        n_exp - 1)
    nxt = (pad_end[blk_e] // tb).astype(jnp.int32)
    nvalid = (pad_end[-1:] // tb).astype(jnp.int32)

    xs = _dispatch(dest, pad_start + cnt, pad_end, h2, n_slots, n_exp, tb)
    act = _ffn_up(blk_e, nxt, nvalid, xs, w_gate_up, b_gate_up[:, None, :], tb)
    outs = _ffn_down(blk_e, nxt, nvalid, act, w_down, b_down[:, None, :], tb)
    return _combine(dest, outs, w_top, x2, final_gain)


def kernel(x, norm_mix, w_in, cmp_pos_k, cmp_pos_v, w_cmp_k1, w_cmp_k2, w_cmp_v1, w_cmp_v2,
           w_gla_alpha, b_gla_alpha, gla_norm, w_proj_nsa, w_proj_gla, w_merge_gate, b_merge_gate,
           w_out, norm_moe, w_router, b_router, w_gate_up, b_gate_up, w_down, b_down, norm_final):
    batch, seq, d = x.shape
    depth = w_in.shape[0]
    assert depth == 1, "the final norm is fused into the (single) layer's combine stage"
    y = _layer(x.reshape(batch * seq, d), batch, seq, norm_mix[0], w_in[0], cmp_pos_k[0],
               cmp_pos_v[0], w_cmp_k1[0], w_cmp_k2[0], w_cmp_v1[0], w_cmp_v2[0], w_gla_alpha[0],
               b_gla_alpha[0], gla_norm[0], w_proj_nsa[0], w_proj_gla[0], w_merge_gate[0],
               b_merge_gate[0], w_out[0], norm_moe[0], w_router[0], b_router[0], w_gate_up[0],
               b_gate_up[0], w_down[0], b_down[0], norm_final[None, :])
    return y.reshape(batch, seq, d)
```
